```python
import jax, jax.numpy as jnp
from jax import lax
import numpy as np

D_MODEL = 2048
BATCH = 8
SEQ = 8192
DEPTH = 4

D_FF = 4096
GM_WIDTH = 1024
GM_GROUPS = 4
CHUNK = 128
CONV_WIDTH = 1024
CONV_K = 31
N_HEADS = 16
Q_RANK = 512
KV_RANK = 512
NOPE_DIM = 128
ROPE_DIM = 64
V_DIM = 128
ROPE_THETA = 10000.0
Q_BLOCK = 128
MLA_WIDTH = N_HEADS * V_DIM
N_BRANCH = 3
MIX_WIDTH = GM_WIDTH + CONV_WIDTH + MLA_WIDTH
IN_SPLITS = (GM_WIDTH, GM_WIDTH, CONV_WIDTH, CONV_WIDTH, Q_RANK, KV_RANK, ROPE_DIM, N_BRANCH * D_MODEL)
IN_OFFSETS = tuple(int(o) for o in np.cumsum(IN_SPLITS)[:-1])
N_IN = int(sum(IN_SPLITS))
EPS = 1e-6

kernel_name = "hybrid_gmlp_conv_mla_macaron"


def rms_norm(x, g):
    xf = x.astype(jnp.float32)
    y = xf * lax.rsqrt(jnp.mean(xf * xf, axis=-1, keepdims=True) + EPS)
    return (y * g.astype(jnp.float32)).astype(x.dtype)


def layer_norm(x, g, b):
    xf = x.astype(jnp.float32)
    mu = jnp.mean(xf, axis=-1, keepdims=True)
    xc = xf - mu
    y = xc * lax.rsqrt(jnp.mean(xc * xc, axis=-1, keepdims=True) + EPS)
    return (y * g.astype(jnp.float32) + b.astype(jnp.float32)).astype(x.dtype)


def swiglu_ffn(h, w_in, w_out):
    gate, up = jnp.split(h @ w_in, 2, axis=-1)
    return (jax.nn.silu(gate) * up) @ w_out


def apply_rope(x, cos, sin):
    xf = x.astype(jnp.float32)
    x1, x2 = jnp.split(xf, 2, axis=-1)
    return jnp.concatenate([x1 * cos - x2 * sin, x2 * cos + x1 * sin], axis=-1).astype(x.dtype)


def chunked_spatial_gating(u, v, ln_g, ln_b, w_s, b_s):
    v = layer_norm(v, ln_g, ln_b)
    bsz, s_len, _ = v.shape
    n_chunks = s_len // CHUNK
    vg = v.reshape(bsz, n_chunks, CHUNK, GM_GROUPS, GM_WIDTH // GM_GROUPS)
    causal = jnp.tril(jnp.ones((CHUNK, CHUNK), dtype=bool))
    w = jnp.where(causal[None], w_s, jnp.zeros_like(w_s)).astype(vg.dtype)
    s = jnp.einsum('gts,bcsgd->bctgd', w, vg) + b_s.T[None, None, :, :, None].astype(vg.dtype)
    return u * s.reshape(bsz, s_len, GM_WIDTH)


def conformer_conv(val, gate, conv_w, conv_b, ln_g, ln_b):
    a = val * jax.nn.sigmoid(gate)
    y = lax.conv_general_dilated(
        a, conv_w[:, None, :].astype(a.dtype), window_strides=(1,),
        padding=[(CONV_K - 1, 0)], dimension_numbers=('NWC', 'WIO', 'NWC'),
        feature_group_count=CONV_WIDTH) + conv_b.astype(a.dtype)
    return jax.nn.silu(layer_norm(y, ln_g, ln_b))


def latent_attention(c_q, c_kv, k_rope, cos, sin, q_norm, w_uq, kv_norm, w_ukv):
    bsz, s_len, _ = c_q.shape
    q = (rms_norm(c_q, q_norm) @ w_uq).reshape(bsz, s_len, N_HEADS, NOPE_DIM + ROPE_DIM)
    q_nope = q[..., :NOPE_DIM]
    q_rot = apply_rope(q[..., NOPE_DIM:], cos[:, :, None, :], sin[:, :, None, :])
    k_rot = apply_rope(k_rope, cos, sin)
    kv = (rms_norm(c_kv, kv_norm) @ w_ukv).reshape(bsz, s_len, N_HEADS, NOPE_DIM + V_DIM)
    k_nope = kv[..., :NOPE_DIM]
    v = kv[..., NOPE_DIM:]
    scale = (NOPE_DIM + ROPE_DIM) ** -0.5
    n_blocks = s_len // Q_BLOCK
    qn_b = q_nope.reshape(bsz, n_blocks, Q_BLOCK, N_HEADS, NOPE_DIM).transpose(1, 0, 2, 3, 4)
    qr_b = q_rot.reshape(bsz, n_blocks, Q_BLOCK, N_HEADS, ROPE_DIM).transpose(1, 0, 2, 3, 4)
    key_pos = jnp.arange(s_len)

    def one_block(args):
        qn, qr, start = args
        s = (jnp.einsum('bqhd,bkhd->bhqk', qn, k_nope)
             + jnp.einsum('bqhr,bkr->bhqk', qr, k_rot)).astype(jnp.float32) * scale
        q_pos = start + jnp.arange(Q_BLOCK)
        s = jnp.where(key_pos[None, :] <= q_pos[:, None], s, -jnp.inf)
        p = jax.nn.softmax(s, axis=-1).astype(v.dtype)
        return jnp.einsum('bhqk,bkhd->bqhd', p, v)

    o = lax.map(one_block, (qn_b, qr_b, jnp.arange(n_blocks) * Q_BLOCK))
    return o.transpose(1, 0, 2, 3, 4).reshape(bsz, s_len, MLA_WIDTH)


def hybrid_mixer(h, cos, sin, w_in, gm_ln_g, gm_ln_b, gm_w_s, gm_b_s,
                 conv_w, conv_b, conv_ln_g, conv_ln_b,
                 q_norm, w_uq, kv_norm, w_ukv, w_branch, w_out):
    a_u, a_v, b_val, b_gate, c_q, c_kv, k_rope, gate_logits = jnp.split(h @ w_in, IN_OFFSETS, axis=-1)
    o_a = chunked_spatial_gating(jax.nn.gelu(a_u), jax.nn.gelu(a_v), gm_ln_g, gm_ln_b, gm_w_s, gm_b_s)
    o_b = conformer_conv(b_val, b_gate, conv_w, conv_b, conv_ln_g, conv_ln_b)
    o_c = latent_attention(c_q, c_kv, k_rope, cos, sin, q_norm, w_uq, kv_norm, w_ukv)
    y_a = o_a @ w_branch[:GM_WIDTH]
    y_b = o_b @ w_branch[GM_WIDTH:GM_WIDTH + CONV_WIDTH]
    y_c = o_c @ w_branch[GM_WIDTH + CONV_WIDTH:]
    g = jax.nn.sigmoid(gate_logits.reshape(h.shape[0], h.shape[1], N_BRANCH, D_MODEL))
    merged = g[:, :, 0] * y_a + g[:, :, 1] * y_b + g[:, :, 2] * y_c
    return merged @ w_out


def _fwd_setup_inputs(seed: int = 0) -> dict:
    key = jax.random.key(seed)
    ks = iter(jax.random.split(key, 32))
    L = DEPTH

    def nrm(shape, scale):
        return jax.random.normal(next(ks), shape, jnp.float32) * scale

    def gain(shape):
        return 1.0 + 0.05 * jax.random.normal(next(ks), shape, jnp.float32)

    x = jax.random.normal(next(ks), (BATCH, SEQ, D_MODEL), jnp.float32)
    offsets = jax.random.randint(next(ks), (BATCH, 1), 0, 4096, dtype=jnp.int32)
    positions = (jnp.arange(SEQ, dtype=jnp.int32)[None, :] + offsets).astype(jnp.int32)
    return {
        "x": x,
        "positions": positions,
        "ffn1_norm_pre": gain((L, D_MODEL)),
        "ffn1_norm_post": gain((L, D_MODEL)),
        "ffn1_w_in": nrm((L, D_MODEL, 2 * D_FF), D_MODEL ** -0.5),
        "ffn1_w_out": nrm((L, D_FF, D_MODEL), D_FF ** -0.5),
        "mix_norm_pre": gain((L, D_MODEL)),
        "mix_norm_post": gain((L, D_MODEL)),
        "mix_w_in": nrm((L, D_MODEL, N_IN), D_MODEL ** -0.5),
        "gm_ln_g": gain((L, GM_WIDTH)),
        "gm_ln_b": nrm((L, GM_WIDTH), 0.02),
        "gm_w_s": nrm((L, GM_GROUPS, CHUNK, CHUNK), CHUNK ** -0.5),
        "gm_b_s": gain((L, GM_GROUPS, CHUNK)),
        "conv_w": nrm((L, CONV_K, CONV_WIDTH), CONV_K ** -0.5),
        "conv_b": nrm((L, CONV_WIDTH), 0.02),
        "conv_ln_g": gain((L, CONV_WIDTH)),
        "conv_ln_b": nrm((L, CONV_WIDTH), 0.02),
        "mla_q_norm": gain((L, Q_RANK)),
        "mla_w_uq": nrm((L, Q_RANK, N_HEADS * (NOPE_DIM + ROPE_DIM)), Q_RANK ** -0.5),
        "mla_kv_norm": gain((L, KV_RANK)),
        "mla_w_ukv": nrm((L, KV_RANK, N_HEADS * (NOPE_DIM + V_DIM)), KV_RANK ** -0.5),
        "mix_w_branch": nrm((L, MIX_WIDTH, D_MODEL), (MIX_WIDTH // N_BRANCH) ** -0.5),
        "mix_w_out": nrm((L, D_MODEL, D_MODEL), D_MODEL ** -0.5),
        "ffn2_norm_pre": gain((L, D_MODEL)),
        "ffn2_norm_post": gain((L, D_MODEL)),
        "ffn2_w_in": nrm((L, D_MODEL, 2 * D_FF), D_MODEL ** -0.5),
        "ffn2_w_out": nrm((L, D_FF, D_MODEL), D_FF ** -0.5),
    }


def _fwd_reference(x, positions, ffn1_norm_pre, ffn1_norm_post, ffn1_w_in, ffn1_w_out,
              mix_norm_pre, mix_norm_post, mix_w_in, gm_ln_g, gm_ln_b, gm_w_s, gm_b_s,
              conv_w, conv_b, conv_ln_g, conv_ln_b,
              mla_q_norm, mla_w_uq, mla_kv_norm, mla_w_ukv, mix_w_branch, mix_w_out,
              ffn2_norm_pre, ffn2_norm_post, ffn2_w_in, ffn2_w_out):
    inv_freq = ROPE_THETA ** (-jnp.arange(0, ROPE_DIM, 2, dtype=jnp.float32) / ROPE_DIM)
    ang = positions.astype(jnp.float32)[..., None] * inv_freq
    cos, sin = jnp.cos(ang), jnp.sin(ang)
    for l in range(DEPTH):
        h = rms_norm(x, ffn1_norm_pre[l])
        x = x + 0.5 * rms_norm(swiglu_ffn(h, ffn1_w_in[l], ffn1_w_out[l]), ffn1_norm_post[l])
        h = rms_norm(x, mix_norm_pre[l])
        m = hybrid_mixer(h, cos, sin, mix_w_in[l], gm_ln_g[l], gm_ln_b[l], gm_w_s[l], gm_b_s[l],
                         conv_w[l], conv_b[l], conv_ln_g[l], conv_ln_b[l],
                         mla_q_norm[l], mla_w_uq[l], mla_kv_norm[l], mla_w_ukv[l],
                         mix_w_branch[l], mix_w_out[l])
        x = x + rms_norm(m, mix_norm_post[l])
        h = rms_norm(x, ffn2_norm_pre[l])
        x = x + 0.5 * rms_norm(swiglu_ffn(h, ffn2_w_in[l], ffn2_w_out[l]), ffn2_norm_post[l])
    return x


import jax as _jax
import jax.numpy as _jnp

TWIN_FORMAT = 'train_step'
FWD_PARAMS = ['x', 'positions', 'ffn1_norm_pre', 'ffn1_norm_post', 'ffn1_w_in', 'ffn1_w_out', 'mix_norm_pre', 'mix_norm_post', 'mix_w_in', 'gm_ln_g', 'gm_ln_b', 'gm_w_s', 'gm_b_s', 'conv_w', 'conv_b', 'conv_ln_g', 'conv_ln_b', 'mla_q_norm', 'mla_w_uq', 'mla_kv_norm', 'mla_w_ukv', 'mix_w_branch', 'mix_w_out', 'ffn2_norm_pre', 'ffn2_norm_post', 'ffn2_w_in', 'ffn2_w_out']
TWIN_WEIGHTS = ['ffn1_norm_pre', 'ffn1_norm_post', 'ffn1_w_in', 'ffn1_w_out', 'mix_norm_pre', 'mix_norm_post', 'mix_w_in', 'gm_ln_g', 'gm_ln_b', 'gm_w_s', 'gm_b_s', 'conv_w', 'conv_b', 'conv_ln_g', 'conv_ln_b', 'mla_q_norm', 'mla_w_uq', 'mla_kv_norm', 'mla_w_ukv', 'mix_w_branch', 'mix_w_out', 'ffn2_norm_pre', 'ffn2_norm_post', 'ffn2_w_in', 'ffn2_w_out']
TWIN_DIFF_INPUT = 'x'
TWIN_INPUTS = ['x', 'positions', 'ffn1_norm_pre', 'ffn1_norm_post', 'ffn1_w_in', 'ffn1_w_out', 'mix_norm_pre', 'mix_norm_post', 'mix_w_in', 'gm_ln_g', 'gm_ln_b', 'gm_w_s', 'gm_b_s', 'conv_w', 'conv_b', 'conv_ln_g', 'conv_ln_b', 'mla_q_norm', 'mla_w_uq', 'mla_kv_norm', 'mla_w_ukv', 'mix_w_branch', 'mix_w_out', 'ffn2_norm_pre', 'ffn2_norm_post', 'ffn2_w_in', 'ffn2_w_out', 'loss_target', 'm_ffn1_norm_pre', 'm_ffn1_norm_post', 'm_ffn1_w_in', 'm_ffn1_w_out', 'm_mix_norm_pre', 'm_mix_norm_post', 'm_mix_w_in', 'm_gm_ln_g', 'm_gm_ln_b', 'm_gm_w_s', 'm_gm_b_s', 'm_conv_w', 'm_conv_b', 'm_conv_ln_g', 'm_conv_ln_b', 'm_mla_q_norm', 'm_mla_w_uq', 'm_mla_kv_norm', 'm_mla_w_ukv', 'm_mix_w_branch', 'm_mix_w_out', 'm_ffn2_norm_pre', 'm_ffn2_norm_post', 'm_ffn2_w_in', 'm_ffn2_w_out', 'v_ffn1_norm_pre', 'v_ffn1_norm_post', 'v_ffn1_w_in', 'v_ffn1_w_out', 'v_mix_norm_pre', 'v_mix_norm_post', 'v_mix_w_in', 'v_gm_ln_g', 'v_gm_ln_b', 'v_gm_w_s', 'v_gm_b_s', 'v_conv_w', 'v_conv_b', 'v_conv_ln_g', 'v_conv_ln_b', 'v_mla_q_norm', 'v_mla_w_uq', 'v_mla_kv_norm', 'v_mla_w_ukv', 'v_mix_w_branch', 'v_mix_w_out', 'v_ffn2_norm_pre', 'v_ffn2_norm_post', 'v_ffn2_w_in', 'v_ffn2_w_out']
TWIN_OUTPUTS = ['loss', 'grad_x', 'grad_ffn1_norm_pre', 'grad_ffn1_norm_post', 'grad_ffn1_w_in', 'grad_ffn1_w_out', 'grad_mix_norm_pre', 'grad_mix_norm_post', 'grad_mix_w_in', 'grad_gm_ln_g', 'grad_gm_ln_b', 'grad_gm_w_s', 'grad_gm_b_s', 'grad_conv_w', 'grad_conv_b', 'grad_conv_ln_g', 'grad_conv_ln_b', 'grad_mla_q_norm', 'grad_mla_w_uq', 'grad_mla_kv_norm', 'grad_mla_w_ukv', 'grad_mix_w_branch', 'grad_mix_w_out', 'grad_ffn2_norm_pre', 'grad_ffn2_norm_post', 'grad_ffn2_w_in', 'grad_ffn2_w_out', 'delta_ffn1_norm_pre', 'delta_ffn1_norm_post', 'delta_ffn1_w_in', 'delta_ffn1_w_out', 'delta_mix_norm_pre', 'delta_mix_norm_post', 'delta_mix_w_in', 'delta_gm_ln_g', 'delta_gm_ln_b', 'delta_gm_w_s', 'delta_gm_b_s', 'delta_conv_w', 'delta_conv_b', 'delta_conv_ln_g', 'delta_conv_ln_b', 'delta_mla_q_norm', 'delta_mla_w_uq', 'delta_mla_kv_norm', 'delta_mla_w_ukv', 'delta_mix_w_branch', 'delta_mix_w_out', 'delta_ffn2_norm_pre', 'delta_ffn2_norm_post', 'delta_ffn2_w_in', 'delta_ffn2_w_out', 'new_m_ffn1_norm_pre', 'new_m_ffn1_norm_post', 'new_m_ffn1_w_in', 'new_m_ffn1_w_out', 'new_m_mix_norm_pre', 'new_m_mix_norm_post', 'new_m_mix_w_in', 'new_m_gm_ln_g', 'new_m_gm_ln_b', 'new_m_gm_w_s', 'new_m_gm_b_s', 'new_m_conv_w', 'new_m_conv_b', 'new_m_conv_ln_g', 'new_m_conv_ln_b', 'new_m_mla_q_norm', 'new_m_mla_w_uq', 'new_m_mla_kv_norm', 'new_m_mla_w_ukv', 'new_m_mix_w_branch', 'new_m_mix_w_out', 'new_m_ffn2_norm_pre', 'new_m_ffn2_norm_post', 'new_m_ffn2_w_in', 'new_m_ffn2_w_out', 'new_v_ffn1_norm_pre', 'new_v_ffn1_norm_post', 'new_v_ffn1_w_in', 'new_v_ffn1_w_out', 'new_v_mix_norm_pre', 'new_v_mix_norm_post', 'new_v_mix_w_in', 'new_v_gm_ln_g', 'new_v_gm_ln_b', 'new_v_gm_w_s', 'new_v_gm_b_s', 'new_v_conv_w', 'new_v_conv_b', 'new_v_conv_ln_g', 'new_v_conv_ln_b', 'new_v_mla_q_norm', 'new_v_mla_w_uq', 'new_v_mla_kv_norm', 'new_v_mla_w_ukv', 'new_v_mix_w_branch', 'new_v_mix_w_out', 'new_v_ffn2_norm_pre', 'new_v_ffn2_norm_post', 'new_v_ffn2_w_in', 'new_v_ffn2_w_out']
TWIN_LEAF_KINDS = {'loss': 'loss', 'grad_x': 'grad_x', 'grad_ffn1_norm_pre': 'grad_w', 'grad_ffn1_norm_post': 'grad_w', 'grad_ffn1_w_in': 'grad_w', 'grad_ffn1_w_out': 'grad_w', 'grad_mix_norm_pre': 'grad_w', 'grad_mix_norm_post': 'grad_w', 'grad_mix_w_in': 'grad_w', 'grad_gm_ln_g': 'grad_w', 'grad_gm_ln_b': 'grad_w', 'grad_gm_w_s': 'grad_w', 'grad_gm_b_s': 'grad_w', 'grad_conv_w': 'grad_w', 'grad_conv_b': 'grad_w', 'grad_conv_ln_g': 'grad_w', 'grad_conv_ln_b': 'grad_w', 'grad_mla_q_norm': 'grad_w', 'grad_mla_w_uq': 'grad_w', 'grad_mla_kv_norm': 'grad_w', 'grad_mla_w_ukv': 'grad_w', 'grad_mix_w_branch': 'grad_w', 'grad_mix_w_out': 'grad_w', 'grad_ffn2_norm_pre': 'grad_w', 'grad_ffn2_norm_post': 'grad_w', 'grad_ffn2_w_in': 'grad_w', 'grad_ffn2_w_out': 'grad_w', 'delta_ffn1_norm_pre': 'delta_w', 'delta_ffn1_norm_post': 'delta_w', 'delta_ffn1_w_in': 'delta_w', 'delta_ffn1_w_out': 'delta_w', 'delta_mix_norm_pre': 'delta_w', 'delta_mix_norm_post': 'delta_w', 'delta_mix_w_in': 'delta_w', 'delta_gm_ln_g': 'delta_w', 'delta_gm_ln_b': 'delta_w', 'delta_gm_w_s': 'delta_w', 'delta_gm_b_s': 'delta_w', 'delta_conv_w': 'delta_w', 'delta_conv_b': 'delta_w', 'delta_conv_ln_g': 'delta_w', 'delta_conv_ln_b': 'delta_w', 'delta_mla_q_norm': 'delta_w', 'delta_mla_w_uq': 'delta_w', 'delta_mla_kv_norm': 'delta_w', 'delta_mla_w_ukv': 'delta_w', 'delta_mix_w_branch': 'delta_w', 'delta_mix_w_out': 'delta_w', 'delta_ffn2_norm_pre': 'delta_w', 'delta_ffn2_norm_post': 'delta_w', 'delta_ffn2_w_in': 'delta_w', 'delta_ffn2_w_out': 'delta_w', 'new_m_ffn1_norm_pre': 'new_m', 'new_m_ffn1_norm_post': 'new_m', 'new_m_ffn1_w_in': 'new_m', 'new_m_ffn1_w_out': 'new_m', 'new_m_mix_norm_pre': 'new_m', 'new_m_mix_norm_post': 'new_m', 'new_m_mix_w_in': 'new_m', 'new_m_gm_ln_g': 'new_m', 'new_m_gm_ln_b': 'new_m', 'new_m_gm_w_s': 'new_m', 'new_m_gm_b_s': 'new_m', 'new_m_conv_w': 'new_m', 'new_m_conv_b': 'new_m', 'new_m_conv_ln_g': 'new_m', 'new_m_conv_ln_b': 'new_m', 'new_m_mla_q_norm': 'new_m', 'new_m_mla_w_uq': 'new_m', 'new_m_mla_kv_norm': 'new_m', 'new_m_mla_w_ukv': 'new_m', 'new_m_mix_w_branch': 'new_m', 'new_m_mix_w_out': 'new_m', 'new_m_ffn2_norm_pre': 'new_m', 'new_m_ffn2_norm_post': 'new_m', 'new_m_ffn2_w_in': 'new_m', 'new_m_ffn2_w_out': 'new_m', 'new_v_ffn1_norm_pre': 'new_v', 'new_v_ffn1_norm_post': 'new_v', 'new_v_ffn1_w_in': 'new_v', 'new_v_ffn1_w_out': 'new_v', 'new_v_mix_norm_pre': 'new_v', 'new_v_mix_norm_post': 'new_v', 'new_v_mix_w_in': 'new_v', 'new_v_gm_ln_g': 'new_v', 'new_v_gm_ln_b': 'new_v', 'new_v_gm_w_s': 'new_v', 'new_v_gm_b_s': 'new_v', 'new_v_conv_w': 'new_v', 'new_v_conv_b': 'new_v', 'new_v_conv_ln_g': 'new_v', 'new_v_conv_ln_b': 'new_v', 'new_v_mla_q_norm': 'new_v', 'new_v_mla_w_uq': 'new_v', 'new_v_mla_kv_norm': 'new_v', 'new_v_mla_w_ukv': 'new_v', 'new_v_mix_w_branch': 'new_v', 'new_v_mix_w_out': 'new_v', 'new_v_ffn2_norm_pre': 'new_v', 'new_v_ffn2_norm_post': 'new_v', 'new_v_ffn2_w_in': 'new_v', 'new_v_ffn2_w_out': 'new_v'}


def _forward(args):
    return _fwd_reference(*[args[k] for k in FWD_PARAMS])


def _output_shape():
    def fwd():
        inp = _fwd_setup_inputs(0)
        return _fwd_reference(*[inp[k] for k in FWD_PARAMS])
    out = _jax.eval_shape(fwd)
    return out.shape, out.dtype

N_MICROBATCH = 1
ADAM_LR = 0.001
ADAM_B1 = 0.9
ADAM_B2 = 0.999
ADAM_EPS = 1e-08
ADAM_WD = 0.01
ADAM_STEP = 10
PER_EXAMPLE_BATCH_AXIS = {'x': 0, 'positions': 0, 'loss_target': 0}
SHARED_INPUTS = []
_WEIGHT_DTYPES = {'ffn1_norm_pre': _jnp.float32, 'ffn1_norm_post': _jnp.float32, 'ffn1_w_in': _jnp.float32, 'ffn1_w_out': _jnp.float32, 'mix_norm_pre': _jnp.float32, 'mix_norm_post': _jnp.float32, 'mix_w_in': _jnp.float32, 'gm_ln_g': _jnp.float32, 'gm_ln_b': _jnp.float32, 'gm_w_s': _jnp.float32, 'gm_b_s': _jnp.float32, 'conv_w': _jnp.float32, 'conv_b': _jnp.float32, 'conv_ln_g': _jnp.float32, 'conv_ln_b': _jnp.float32, 'mla_q_norm': _jnp.float32, 'mla_w_uq': _jnp.float32, 'mla_kv_norm': _jnp.float32, 'mla_w_ukv': _jnp.float32, 'mix_w_branch': _jnp.float32, 'mix_w_out': _jnp.float32, 'ffn2_norm_pre': _jnp.float32, 'ffn2_norm_post': _jnp.float32, 'ffn2_w_in': _jnp.float32, 'ffn2_w_out': _jnp.float32}
MOMENT_SCALE = {'ffn1_norm_pre': 1.786116e+00, 'ffn1_norm_post': 7.897774e+00, 'ffn1_w_in': 8.155978e-01, 'ffn1_w_out': 1.503797e+00, 'mix_norm_pre': 8.838302e+00, 'mix_norm_post': 3.574722e+01, 'mix_w_in': 3.690456e+00, 'gm_ln_g': 4.235149e-01, 'gm_ln_b': 6.114363e-01, 'gm_w_s': 5.513377e-01, 'gm_b_s': 1.279358e+00, 'conv_w': 4.774863e+00, 'conv_b': 4.224074e+01, 'conv_ln_g': 1.504162e+01, 'conv_ln_b': 2.083200e+01, 'mla_q_norm': 3.411569e-01, 'mla_w_uq': 1.351344e-01, 'mla_kv_norm': 1.481843e+01, 'mla_w_ukv': 5.154550e+00, 'mix_w_branch': 7.692007e+00, 'mix_w_out': 1.316849e+01, 'ffn2_norm_pre': 2.279925e+00, 'ffn2_norm_post': 8.245154e+00, 'ffn2_w_in': 1.149542e+00, 'ffn2_w_out': 2.085705e+00}


def _to_microbatches(a, axis):
    t = _jnp.moveaxis(a, axis, 0)
    t = t.reshape((N_MICROBATCH, t.shape[0] // N_MICROBATCH) + t.shape[1:])
    return _jnp.moveaxis(t, 1, axis + 1)


def setup_inputs(seed: int = 0) -> dict:
    inp = _fwd_setup_inputs(seed)
    key = _jax.random.fold_in(_jax.random.key(seed), 7919)
    shape, _ = _output_shape()
    out = dict(inp)
    out["loss_target"] = _jax.random.normal(_jax.random.fold_in(key, 0), shape, _jnp.float32)
    for i, name in enumerate(TWIN_WEIGHTS):
        w = inp[name].astype(_jnp.float32)
        if MOMENT_SCALE is None:
            s = _jnp.sqrt(_jnp.mean(_jnp.square(w)) + 1e-30)
        else:
            s = MOMENT_SCALE[name]
        km, kv = _jax.random.split(_jax.random.fold_in(key, i + 1))
        out[name] = w
        out["m_" + name] = s * _jax.random.normal(km, w.shape, _jnp.float32)
        out["v_" + name] = (s * s) * _jax.random.uniform(kv, w.shape, _jnp.float32, 0.5, 1.5)
    if N_MICROBATCH > 1:
        for name, axis in PER_EXAMPLE_BATCH_AXIS.items():
            out[name] = _to_microbatches(out[name], axis)
    return {'x': out['x'], 'positions': out['positions'], 'ffn1_norm_pre': out['ffn1_norm_pre'], 'ffn1_norm_post': out['ffn1_norm_post'], 'ffn1_w_in': out['ffn1_w_in'], 'ffn1_w_out': out['ffn1_w_out'], 'mix_norm_pre': out['mix_norm_pre'], 'mix_norm_post': out['mix_norm_post'], 'mix_w_in': out['mix_w_in'], 'gm_ln_g': out['gm_ln_g'], 'gm_ln_b': out['gm_ln_b'], 'gm_w_s': out['gm_w_s'], 'gm_b_s': out['gm_b_s'], 'conv_w': out['conv_w'], 'conv_b': out['conv_b'], 'conv_ln_g': out['conv_ln_g'], 'conv_ln_b': out['conv_ln_b'], 'mla_q_norm': out['mla_q_norm'], 'mla_w_uq': out['mla_w_uq'], 'mla_kv_norm': out['mla_kv_norm'], 'mla_w_ukv': out['mla_w_ukv'], 'mix_w_branch': out['mix_w_branch'], 'mix_w_out': out['mix_w_out'], 'ffn2_norm_pre': out['ffn2_norm_pre'], 'ffn2_norm_post': out['ffn2_norm_post'], 'ffn2_w_in': out['ffn2_w_in'], 'ffn2_w_out': out['ffn2_w_out'], 'loss_target': out['loss_target'], 'm_ffn1_norm_pre': out['m_ffn1_norm_pre'], 'm_ffn1_norm_post': out['m_ffn1_norm_post'], 'm_ffn1_w_in': out['m_ffn1_w_in'], 'm_ffn1_w_out': out['m_ffn1_w_out'], 'm_mix_norm_pre': out['m_mix_norm_pre'], 'm_mix_norm_post': out['m_mix_norm_post'], 'm_mix_w_in': out['m_mix_w_in'], 'm_gm_ln_g': out['m_gm_ln_g'], 'm_gm_ln_b': out['m_gm_ln_b'], 'm_gm_w_s': out['m_gm_w_s'], 'm_gm_b_s': out['m_gm_b_s'], 'm_conv_w': out['m_conv_w'], 'm_conv_b': out['m_conv_b'], 'm_conv_ln_g': out['m_conv_ln_g'], 'm_conv_ln_b': out['m_conv_ln_b'], 'm_mla_q_norm': out['m_mla_q_norm'], 'm_mla_w_uq': out['m_mla_w_uq'], 'm_mla_kv_norm': out['m_mla_kv_norm'], 'm_mla_w_ukv': out['m_mla_w_ukv'], 'm_mix_w_branch': out['m_mix_w_branch'], 'm_mix_w_out': out['m_mix_w_out'], 'm_ffn2_norm_pre': out['m_ffn2_norm_pre'], 'm_ffn2_norm_post': out['m_ffn2_norm_post'], 'm_ffn2_w_in': out['m_ffn2_w_in'], 'm_ffn2_w_out': out['m_ffn2_w_out'], 'v_ffn1_norm_pre': out['v_ffn1_norm_pre'], 'v_ffn1_norm_post': out['v_ffn1_norm_post'], 'v_ffn1_w_in': out['v_ffn1_w_in'], 'v_ffn1_w_out': out['v_ffn1_w_out'], 'v_mix_norm_pre': out['v_mix_norm_pre'], 'v_mix_norm_post': out['v_mix_norm_post'], 'v_mix_w_in': out['v_mix_w_in'], 'v_gm_ln_g': out['v_gm_ln_g'], 'v_gm_ln_b': out['v_gm_ln_b'], 'v_gm_w_s': out['v_gm_w_s'], 'v_gm_b_s': out['v_gm_b_s'], 'v_conv_w': out['v_conv_w'], 'v_conv_b': out['v_conv_b'], 'v_conv_ln_g': out['v_conv_ln_g'], 'v_conv_ln_b': out['v_conv_ln_b'], 'v_mla_q_norm': out['v_mla_q_norm'], 'v_mla_w_uq': out['v_mla_w_uq'], 'v_mla_kv_norm': out['v_mla_kv_norm'], 'v_mla_w_ukv': out['v_mla_w_ukv'], 'v_mix_w_branch': out['v_mix_w_branch'], 'v_mix_w_out': out['v_mix_w_out'], 'v_ffn2_norm_pre': out['v_ffn2_norm_pre'], 'v_ffn2_norm_post': out['v_ffn2_norm_post'], 'v_ffn2_w_in': out['v_ffn2_w_in'], 'v_ffn2_w_out': out['v_ffn2_w_out']}


def _loss(weights, diff, rest, loss_target):
    with _jax.named_scope("forward"):
        args = {**rest, TWIN_DIFF_INPUT: diff, **{k: w.astype(_WEIGHT_DTYPES[k]) for k, w in weights.items()}}
        y = _forward(args)
    with _jax.named_scope("loss_head"):
        err = _jnp.square(y.astype(_jnp.float32) - loss_target)
        return 0.5 * _jnp.sum(_jnp.mean(err, axis=-1)) if err.ndim else 0.5 * err


def _adamw(w, g, m, v):
    m = ADAM_B1 * m + (1.0 - ADAM_B1) * g
    v = ADAM_B2 * v + (1.0 - ADAM_B2) * _jnp.square(g)
    m_hat = m / (1.0 - ADAM_B1 ** ADAM_STEP)
    v_hat = v / (1.0 - ADAM_B2 ** ADAM_STEP)
    delta = -ADAM_LR * (m_hat / (_jnp.sqrt(v_hat) + ADAM_EPS) + ADAM_WD * w)
    return delta, m, v


def reference(x, positions, ffn1_norm_pre, ffn1_norm_post, ffn1_w_in, ffn1_w_out, mix_norm_pre, mix_norm_post, mix_w_in, gm_ln_g, gm_ln_b, gm_w_s, gm_b_s, conv_w, conv_b, conv_ln_g, conv_ln_b, mla_q_norm, mla_w_uq, mla_kv_norm, mla_w_ukv, mix_w_branch, mix_w_out, ffn2_norm_pre, ffn2_norm_post, ffn2_w_in, ffn2_w_out, loss_target, m_ffn1_norm_pre, m_ffn1_norm_post, m_ffn1_w_in, m_ffn1_w_out, m_mix_norm_pre, m_mix_norm_post, m_mix_w_in, m_gm_ln_g, m_gm_ln_b, m_gm_w_s, m_gm_b_s, m_conv_w, m_conv_b, m_conv_ln_g, m_conv_ln_b, m_mla_q_norm, m_mla_w_uq, m_mla_kv_norm, m_mla_w_ukv, m_mix_w_branch, m_mix_w_out, m_ffn2_norm_pre, m_ffn2_norm_post, m_ffn2_w_in, m_ffn2_w_out, v_ffn1_norm_pre, v_ffn1_norm_post, v_ffn1_w_in, v_ffn1_w_out, v_mix_norm_pre, v_mix_norm_post, v_mix_w_in, v_gm_ln_g, v_gm_ln_b, v_gm_w_s, v_gm_b_s, v_conv_w, v_conv_b, v_conv_ln_g, v_conv_ln_b, v_mla_q_norm, v_mla_w_uq, v_mla_kv_norm, v_mla_w_ukv, v_mix_w_branch, v_mix_w_out, v_ffn2_norm_pre, v_ffn2_norm_post, v_ffn2_w_in, v_ffn2_w_out):
    given = dict(x=x, positions=positions, ffn1_norm_pre=ffn1_norm_pre, ffn1_norm_post=ffn1_norm_post, ffn1_w_in=ffn1_w_in, ffn1_w_out=ffn1_w_out, mix_norm_pre=mix_norm_pre, mix_norm_post=mix_norm_post, mix_w_in=mix_w_in, gm_ln_g=gm_ln_g, gm_ln_b=gm_ln_b, gm_w_s=gm_w_s, gm_b_s=gm_b_s, conv_w=conv_w, conv_b=conv_b, conv_ln_g=conv_ln_g, conv_ln_b=conv_ln_b, mla_q_norm=mla_q_norm, mla_w_uq=mla_w_uq, mla_kv_norm=mla_kv_norm, mla_w_ukv=mla_w_ukv, mix_w_branch=mix_w_branch, mix_w_out=mix_w_out, ffn2_norm_pre=ffn2_norm_pre, ffn2_norm_post=ffn2_norm_post, ffn2_w_in=ffn2_w_in, ffn2_w_out=ffn2_w_out, loss_target=loss_target, m_ffn1_norm_pre=m_ffn1_norm_pre, m_ffn1_norm_post=m_ffn1_norm_post, m_ffn1_w_in=m_ffn1_w_in, m_ffn1_w_out=m_ffn1_w_out, m_mix_norm_pre=m_mix_norm_pre, m_mix_norm_post=m_mix_norm_post, m_mix_w_in=m_mix_w_in, m_gm_ln_g=m_gm_ln_g, m_gm_ln_b=m_gm_ln_b, m_gm_w_s=m_gm_w_s, m_gm_b_s=m_gm_b_s, m_conv_w=m_conv_w, m_conv_b=m_conv_b, m_conv_ln_g=m_conv_ln_g, m_conv_ln_b=m_conv_ln_b, m_mla_q_norm=m_mla_q_norm, m_mla_w_uq=m_mla_w_uq, m_mla_kv_norm=m_mla_kv_norm, m_mla_w_ukv=m_mla_w_ukv, m_mix_w_branch=m_mix_w_branch, m_mix_w_out=m_mix_w_out, m_ffn2_norm_pre=m_ffn2_norm_pre, m_ffn2_norm_post=m_ffn2_norm_post, m_ffn2_w_in=m_ffn2_w_in, m_ffn2_w_out=m_ffn2_w_out, v_ffn1_norm_pre=v_ffn1_norm_pre, v_ffn1_norm_post=v_ffn1_norm_post, v_ffn1_w_in=v_ffn1_w_in, v_ffn1_w_out=v_ffn1_w_out, v_mix_norm_pre=v_mix_norm_pre, v_mix_norm_post=v_mix_norm_post, v_mix_w_in=v_mix_w_in, v_gm_ln_g=v_gm_ln_g, v_gm_ln_b=v_gm_ln_b, v_gm_w_s=v_gm_w_s, v_gm_b_s=v_gm_b_s, v_conv_w=v_conv_w, v_conv_b=v_conv_b, v_conv_ln_g=v_conv_ln_g, v_conv_ln_b=v_conv_ln_b, v_mla_q_norm=v_mla_q_norm, v_mla_w_uq=v_mla_w_uq, v_mla_kv_norm=v_mla_kv_norm, v_mla_w_ukv=v_mla_w_ukv, v_mix_w_branch=v_mix_w_branch, v_mix_w_out=v_mix_w_out, v_ffn2_norm_pre=v_ffn2_norm_pre, v_ffn2_norm_post=v_ffn2_norm_post, v_ffn2_w_in=v_ffn2_w_in, v_ffn2_w_out=v_ffn2_w_out)
    weights = {n: given[n] for n in TWIN_WEIGHTS}
    shared = {n: given[n] for n in SHARED_INPUTS}
    per_example = {n: given[n] for n in ['x', 'positions']}
    grad_fn = _jax.value_and_grad(_loss, argnums=(0, 1))

    def one_microbatch(ex, loss_target):
        ex = dict(ex)
        diff = ex.pop(TWIN_DIFF_INPUT)
        return grad_fn(weights, diff, {**shared, **ex}, loss_target)

    if N_MICROBATCH == 1:
        loss, (grad_w, grad_x) = one_microbatch(per_example, given["loss_target"])
    else:
        def body(carry, xs):
            loss_sum, grad_sum = carry
            l_k, (gw_k, gx_k) = one_microbatch(xs[0], xs[1])
            with _jax.named_scope("update"):
                return (loss_sum + l_k, _jax.tree.map(_jnp.add, grad_sum, gw_k)), gx_k

        init = (_jnp.zeros((), _jnp.float32), _jax.tree.map(_jnp.zeros_like, weights))
        (loss, grad_w), grad_x = _jax.lax.scan(body, init, (per_example, given["loss_target"]))
    with _jax.named_scope("update"):
        delta_w, new_m, new_v = {}, {}, {}
        for n in TWIN_WEIGHTS:
            delta_w[n], new_m[n], new_v[n] = _adamw(weights[n], grad_w[n], given["m_" + n], given["v_" + n])
    return (loss, grad_x, *[grad_w[n] for n in TWIN_WEIGHTS], *[delta_w[n] for n in TWIN_WEIGHTS],
            *[new_m[n] for n in TWIN_WEIGHTS], *[new_v[n] for n in TWIN_WEIGHTS])
```

```python
import functools

import numpy as np
import jax
import jax.numpy as jnp
from jax import lax
from jax.experimental import pallas as pl
from jax.experimental.pallas import tpu as pltpu

F32 = jnp.float32
BF16 = jnp.bfloat16

N_DEV = 8
N_HEADS = 16
NOPE = 128
ROPE = 64
VDIM = 128
HQ = 256
GROUPS = 4
CHUNK = 128
CONV_K = 31
CONV_HALO = 32
EPS = 1e-6
ROPE_THETA = 10000.0
ADAM_LR = 0.001
ADAM_B1 = 0.9
ADAM_B2 = 0.999
ADAM_EPS = 1e-08
ADAM_WD = 0.01
ADAM_STEP = 10
NEG = -1e30

V7X_VMEM_LIMIT = 56 * 1024 * 1024
MM_TM, MM_TN, MM_TK = 1024, 1024, 512
ROW_TILE = 256
ROW_TILE_WIDE = 128
ATT_TILE = 512
CONV_TILE = 256
GM_TILE = 256
ADAM_BLOCK_ELEMS = 128 * 1024

BIG = ("ffn1_w_in", "ffn1_w_out", "mix_w_in", "mla_w_uq", "mla_w_ukv", "mix_w_branch", "mix_w_out",
       "ffn2_w_in", "ffn2_w_out")
COL_SHARDED = ("ffn1_w_in", "mix_w_in", "mla_w_uq", "mla_w_ukv", "ffn2_w_in", "conv_w")
SMALL = ("ffn1_norm_pre", "ffn1_norm_post", "mix_norm_pre", "mix_norm_post", "gm_ln_g", "gm_ln_b", "gm_w_s",
         "gm_b_s", "conv_b", "conv_ln_g", "conv_ln_b", "mla_q_norm", "mla_kv_norm", "ffn2_norm_pre",
         "ffn2_norm_post")
WEIGHTS = ("ffn1_norm_pre", "ffn1_norm_post", "ffn1_w_in", "ffn1_w_out", "mix_norm_pre", "mix_norm_post",
           "mix_w_in", "gm_ln_g", "gm_ln_b", "gm_w_s", "gm_b_s", "conv_w", "conv_b", "conv_ln_g", "conv_ln_b",
           "mla_q_norm", "mla_w_uq", "mla_kv_norm", "mla_w_ukv", "mix_w_branch", "mix_w_out", "ffn2_norm_pre",
           "ffn2_norm_post", "ffn2_w_in", "ffn2_w_out")


def _cparams(sem):
    return pltpu.CompilerParams(dimension_semantics=sem, vmem_limit_bytes=V7X_VMEM_LIMIT)


def _pick(dim, pref):
    if dim <= pref:
        return dim
    t = pref
    while t >= 128:
        if dim % t == 0:
            return t
        t -= 128
    return dim


def _sigmoid(x):
    return 1.0 / (1.0 + jnp.exp(-x))


_GELU_C = 0.7978845608028654


def _gelu(x):
    t = jnp.tanh(_GELU_C * (x + 0.044715 * x * x * x))
    return 0.5 * x * (1.0 + t)


def _gelu_grad(x):
    t = jnp.tanh(_GELU_C * (x + 0.044715 * x * x * x))
    return 0.5 * (1.0 + t) + 0.5 * x * (1.0 - t * t) * _GELU_C * (1.0 + 3.0 * 0.044715 * x * x)


def _rms(x):
    r = lax.rsqrt(jnp.mean(x * x, axis=-1, keepdims=True) + EPS)
    return x * r, r


def _rms_bwd(xn, r, t):
    return r * (t - xn * jnp.mean(t * xn, axis=-1, keepdims=True))


def _colsum(x):
    return jnp.sum(x, axis=0, keepdims=True)


_DIMS = {"nn": (((1,), (0,)), ((), ())), "nt": (((1,), (1,)), ((), ())), "tn": (((0,), (0,)), ((), ()))}


def _mm(a, b, mode, out_dtype, name):
    if mode == "tn":
        k_dim, m_dim = a.shape
    else:
        m_dim, k_dim = a.shape
    n_dim = b.shape[0] if mode == "nt" else b.shape[1]
    tm, tn, tk = _pick(m_dim, MM_TM), _pick(n_dim, MM_TN), _pick(k_dim, MM_TK)
    nk = k_dim // tk
    dims = _DIMS[mode]

    def body(a_ref, b_ref, o_ref, acc_ref):
        k = pl.program_id(2)

        @pl.when(k == 0)
        def _():
            acc_ref[...] = jnp.zeros_like(acc_ref)

        acc_ref[...] += lax.dot_general(a_ref[...], b_ref[...], dims, preferred_element_type=F32)

        @pl.when(k == nk - 1)
        def _():
            o_ref[...] = acc_ref[...].astype(o_ref.dtype)

    if mode == "tn":
        a_spec = pl.BlockSpec((tk, tm), lambda i, j, k: (k, i))
    else:
        a_spec = pl.BlockSpec((tm, tk), lambda i, j, k: (i, k))
    if mode == "nt":
        b_spec = pl.BlockSpec((tn, tk), lambda i, j, k: (j, k))
    else:
        b_spec = pl.BlockSpec((tk, tn), lambda i, j, k: (k, j))
    return pl.pallas_call(
        body, out_shape=jax.ShapeDtypeStruct((m_dim, n_dim), out_dtype),
        grid=(m_dim // tm, n_dim // tn, nk), in_specs=[a_spec, b_spec],
        out_specs=pl.BlockSpec((tm, tn), lambda i, j, k: (i, j)),
        scratch_shapes=[pltpu.VMEM((tm, tn), F32)],
        compiler_params=_cparams(("parallel", "parallel", "arbitrary")), name=name)(a, b)


def _rowwise(fn, name, n_rows, tr, row_ins, full_ins, row_outs, acc_outs=()):
    tr = min(tr, n_rows)
    n_ri, n_fi, n_ro = len(row_ins), len(full_ins), len(row_outs)

    def body(*refs):
        i = pl.program_id(0)
        ri, fi = refs[:n_ri], refs[n_ri:n_ri + n_fi]
        ro, ao = refs[n_ri + n_fi:n_ri + n_fi + n_ro], refs[n_ri + n_fi + n_ro:]

        @pl.when(i == 0)
        def _():
            for r in ao:
                r[...] = jnp.zeros_like(r)

        fn(i, ri, fi, ro, ao)

    in_specs = [pl.BlockSpec((tr, w), functools.partial(lambda c, i: (i, c), cb)) for _, w, cb in row_ins]
    in_specs += [pl.BlockSpec(a.shape, functools.partial(lambda nd, i: (0,) * nd, a.ndim)) for a in full_ins]
    out_specs = [pl.BlockSpec((tr, w), lambda i: (i, 0)) for w, _ in row_outs]
    out_specs += [pl.BlockSpec(s, functools.partial(lambda nd, i: (0,) * nd, len(s))) for s, _ in acc_outs]
    out_shape = [jax.ShapeDtypeStruct((n_rows, w), d) for w, d in row_outs]
    out_shape += [jax.ShapeDtypeStruct(s, d) for s, d in acc_outs]
    return pl.pallas_call(
        body, out_shape=out_shape, grid=(n_rows // tr,), in_specs=in_specs, out_specs=out_specs,
        compiler_params=_cparams(("arbitrary",)), name=name)(*[a for a, _, _ in row_ins], *full_ins)


def _norm_fwd(x, g, name):
    t_dim, d = x.shape

    def fn(i, ri, fi, ro, ao):
        xn, _ = _rms(ri[0][...])
        ro[0][...] = (xn * fi[0][...]).astype(BF16)

    return _rowwise(fn, name, t_dim, ROW_TILE, [(x, d, 0)], [g], [(d, BF16)])[0]


def _resid_fwd(x, y, g, coef, name):
    t_dim, d = x.shape

    def fn(i, ri, fi, ro, ao):
        yn, _ = _rms(ri[1][...])
        ro[0][...] = ri[0][...] + coef * (yn * fi[0][...])

    return _rowwise(fn, name, t_dim, ROW_TILE, [(x, d, 0), (y, d, 0)], [g], [(d, F32)])[0]


def _resid_bwd(y, dxo, g, coef, name):
    t_dim, d = y.shape

    def fn(i, ri, fi, ro, ao):
        yn, r = _rms(ri[0][...])
        dyn = coef * ri[1][...]
        ao[0][...] += _colsum(dyn * yn)
        ro[0][...] = _rms_bwd(yn, r, dyn * fi[0][...]).astype(BF16)

    return _rowwise(fn, name, t_dim, ROW_TILE, [(y, d, 0), (dxo, d, 0)], [g], [(d, BF16)], [((1, d), F32)])


def _norm_bwd(x, dhs, dxo, g, name):
    t_dim, d = x.shape
    n = len(dhs)

    def fn(i, ri, fi, ro, ao):
        xn, r = _rms(ri[0][...])
        dh = ri[2][...]
        for q in range(1, n):
            dh = dh + ri[2 + q][...]
        ao[0][...] += _colsum(dh * xn)
        ro[0][...] = ri[1][...] + _rms_bwd(xn, r, dh * fi[0][...])

    return _rowwise(fn, name, t_dim, ROW_TILE, [(x, d, 0), (dxo, d, 0)] + [(a, d, 0) for a in dhs], [g],
                    [(d, F32)], [((1, d), F32)])


def _swiglu_fwd(gu, name):
    t_dim, f2 = gu.shape
    f = f2 // 2

    def fn(i, ri, fi, ro, ao):
        gate, up = ri[0][:, :f], ri[0][:, f:]
        ro[0][...] = (gate * _sigmoid(gate) * up).astype(BF16)

    return _rowwise(fn, name, t_dim, ROW_TILE_WIDE, [(gu, f2, 0)], [], [(f, BF16)])[0]


def _swiglu_bwd(gu, dact, name):
    t_dim, f2 = gu.shape
    f = f2 // 2

    def fn(i, ri, fi, ro, ao):
        gate, up = ri[0][:, :f], ri[0][:, f:]
        da = ri[1][...]
        s = _sigmoid(gate)
        ro[0][:, :f] = (da * up * (s * (1.0 + gate * (1.0 - s)))).astype(BF16)
        ro[0][:, f:] = (da * (gate * s)).astype(BF16)

    return _rowwise(fn, name, t_dim, ROW_TILE_WIDE, [(gu, f2, 0), (dact, f, 0)], [], [(f2, BF16)])[0]


def _loss_fwd_bwd(y, target, name):
    t_dim, d = y.shape

    def fn(i, ri, fi, ro, ao):
        err = ri[0][...] - ri[1][...]
        ao[0][...] += _colsum(jnp.sum(err * err, axis=1, keepdims=True)) * (0.5 / d)
        ro[0][...] = err * (1.0 / d)

    return _rowwise(fn, name, t_dim, ROW_TILE, [(y, d, 0), (target, d, 0)], [], [(d, F32)], [((1, 1), F32)])


def _ln_stats(v):
    mu = jnp.mean(v, axis=-1, keepdims=True)
    xc = v - mu
    rstd = lax.rsqrt(jnp.mean(xc * xc, axis=-1, keepdims=True) + EPS)
    return xc * rstd, rstd


def _tril_mask(upper=False):
    row = lax.broadcasted_iota(jnp.int32, (CHUNK, CHUNK), 0)
    col = lax.broadcasted_iota(jnp.int32, (CHUNK, CHUNK), 1)
    return row <= col if upper else row >= col


def _gm_fwd(z_a, ln_g, ln_b, w_s, b_s, gmw, name):
    t_dim = z_a.shape[0]
    gw = gmw // GROUPS
    tr = min(GM_TILE, t_dim)

    def fn(i, ri, fi, ro, ao):
        lng, lnb, ws_ref, bs_ref = fi
        u = _gelu(ri[0][...])
        vhat, _ = _ln_stats(_gelu(ri[1][...]))
        vn = (vhat * lng[...] + lnb[...]).astype(BF16)
        mask = _tril_mask()
        for g in range(GROUPS):
            wg = jnp.where(mask, ws_ref[g], 0.0).astype(BF16)
            for c in range(tr // CHUNK):
                rows, cols = slice(c * CHUNK, (c + 1) * CHUNK), slice(g * gw, (g + 1) * gw)
                s = jnp.dot(wg, vn[rows, cols], preferred_element_type=F32) + bs_ref[g]
                ro[0][rows, cols] = (u[rows, cols] * s).astype(BF16)

    return _rowwise(fn, name, t_dim, tr, [(z_a, gmw, 0), (z_a, gmw, 1)], [ln_g, ln_b, w_s, b_s], [(gmw, BF16)])[0]


def _gm_bwd(z_a, do_a, ln_g, ln_b, w_s, w_s_t, b_s, gmw, name):
    t_dim = z_a.shape[0]
    gw = gmw // GROUPS
    tr = min(GM_TILE, t_dim)

    def fn(i, ri, fi, ro, ao):
        lng, lnb, ws_ref, wst_ref, bs_ref = fi
        d_lng, d_lnb, d_ws, d_bs = ao
        a_u, a_v, do = ri[0][...], ri[1][...], ri[2][...]
        u = _gelu(a_u)
        vhat, rstd = _ln_stats(_gelu(a_v))
        vn = (vhat * lng[...] + lnb[...]).astype(BF16)
        mask = _tril_mask()
        wgs = [jnp.where(mask, ws_ref[g], 0.0).astype(BF16) for g in range(GROUPS)]
        wgts = [jnp.where(_tril_mask(upper=True), wst_ref[g], 0.0).astype(BF16) for g in range(GROUPS)]
        for c in range(tr // CHUNK):
            rows = slice(c * CHUNK, (c + 1) * CHUNK)
            dvn_parts = []
            for g in range(GROUPS):
                cols = slice(g * gw, (g + 1) * gw)
                vn_blk = vn[rows, cols]
                s = jnp.dot(wgs[g], vn_blk, preferred_element_type=F32) + bs_ref[g]
                ro[0][rows, cols] = (do[rows, cols] * s * _gelu_grad(a_u[rows, cols])).astype(BF16)
                ds = do[rows, cols] * u[rows, cols]
                d_bs[g] += jnp.sum(ds, axis=1, keepdims=True)
                dsb = ds.astype(BF16)
                dw = lax.dot_general(dsb, vn_blk, _DIMS["nt"], preferred_element_type=F32)
                d_ws[g] += jnp.where(mask, dw, 0.0)
                dvn_parts.append(jnp.dot(wgts[g], dsb, preferred_element_type=F32))
            dvn = jnp.concatenate(dvn_parts, axis=1)
            vh, rs = vhat[rows], rstd[rows]
            d_lng[...] += _colsum(dvn * vh)
            d_lnb[...] += _colsum(dvn)
            dvh = dvn * lng[...]
            dv = rs * (dvh - jnp.mean(dvh, axis=-1, keepdims=True) - vh * jnp.mean(dvh * vh, axis=-1, keepdims=True))
            ro[1][rows, :] = (dv * _gelu_grad(a_v[rows])).astype(BF16)

    return _rowwise(fn, name, t_dim, tr, [(z_a, gmw, 0), (z_a, gmw, 1), (do_a, gmw, 0)],
                    [ln_g, ln_b, w_s, w_s_t, b_s], [(gmw, BF16), (gmw, BF16)],
                    [((1, gmw), F32), ((1, gmw), F32), ((GROUPS, CHUNK, CHUNK), F32), ((GROUPS, CHUNK, 1), F32)])


def _glu_fwd(z_a, gmw, cw, name):
    t_dim = z_a.shape[0]
    cb = (2 * gmw) // cw

    def fn(i, ri, fi, ro, ao):
        ro[0][...] = ri[0][...] * _sigmoid(ri[1][...])

    return _rowwise(fn, name, t_dim, ROW_TILE, [(z_a, cw, cb), (z_a, cw, cb + 1)], [], [(cw, F32)])[0]


def _conv_fwd(a, w, b, ln_g, ln_b, name):
    t_dim, c_dim = a.shape
    tr = min(CONV_TILE, t_dim)
    hb = tr // CONV_HALO

    def body(cur_ref, prev_ref, w_ref, b_ref, g_ref, be_ref, y_ref, o_ref, buf):
        i = pl.program_id(0)
        buf[0:CONV_HALO, :] = jnp.where(i > 0, prev_ref[...], 0.0)
        buf[CONV_HALO:, :] = cur_ref[...]
        for cs in range(c_dim // 128):
            lanes = pl.ds(cs * 128, 128)
            acc = jnp.zeros((tr, 128), F32)
            for k in range(CONV_K):
                acc = acc + w_ref[k:k + 1, lanes] * buf[pl.ds(k + 2, tr), lanes]
            y_ref[:, lanes] = acc + b_ref[:, lanes]
        n_hat, _ = _ln_stats(y_ref[...])
        n = n_hat * g_ref[...] + be_ref[...]
        o_ref[...] = (n * _sigmoid(n)).astype(BF16)

    full = lambda arr: pl.BlockSpec(arr.shape, lambda i: (0, 0))
    return pl.pallas_call(
        body, out_shape=[jax.ShapeDtypeStruct((t_dim, c_dim), F32), jax.ShapeDtypeStruct((t_dim, c_dim), BF16)],
        grid=(t_dim // tr,),
        in_specs=[pl.BlockSpec((tr, c_dim), lambda i: (i, 0)),
                  pl.BlockSpec((CONV_HALO, c_dim), lambda i: (jnp.maximum(i * hb - 1, 0), 0)),
                  full(w), full(b), full(ln_g), full(ln_b)],
        out_specs=[pl.BlockSpec((tr, c_dim), lambda i: (i, 0)), pl.BlockSpec((tr, c_dim), lambda i: (i, 0))],
        scratch_shapes=[pltpu.VMEM((tr + CONV_HALO, c_dim), F32)],
        compiler_params=_cparams(("arbitrary",)), name=name)(a, a, w, b, ln_g, ln_b)


def _conv_ln_bwd(y, do_b, ln_g, ln_b, name):
    t_dim, c_dim = y.shape

    def fn(i, ri, fi, ro, ao):
        n_hat, rstd = _ln_stats(ri[0][...])
        n = n_hat * fi[0][...] + fi[1][...]
        s = _sigmoid(n)
        dn = ri[1][...] * (s * (1.0 + n * (1.0 - s)))
        ao[0][...] += _colsum(dn * n_hat)
        ao[1][...] += _colsum(dn)
        dnh = dn * fi[0][...]
        ro[0][...] = rstd * (dnh - jnp.mean(dnh, axis=-1, keepdims=True)
                             - n_hat * jnp.mean(dnh * n_hat, axis=-1, keepdims=True))

    return _rowwise(fn, name, t_dim, ROW_TILE, [(y, c_dim, 0), (do_b, c_dim, 0)], [ln_g, ln_b], [(c_dim, F32)],
                    [((1, c_dim), F32), ((1, c_dim), F32)])


def _conv_bwd(dy, a, z_a, w, gmw, name):
    t_dim, c_dim = a.shape
    tr = min(CONV_TILE, t_dim)
    hb = tr // CONV_HALO
    n_halo = t_dim // CONV_HALO
    nb = t_dim // tr
    cb = (2 * gmw) // c_dim

    def body(dy_ref, dyn_ref, a_ref, ap_ref, val_ref, gate_ref, w_ref, dval_ref, dgate_ref, dw_ref, db_ref,
             dbuf, abuf, da_buf):
        i = pl.program_id(0)

        @pl.when(i == 0)
        def _():
            dw_ref[...] = jnp.zeros_like(dw_ref)
            db_ref[...] = jnp.zeros_like(db_ref)

        dbuf[0:tr, :] = dy_ref[...]
        dbuf[tr:, :] = jnp.where(i < nb - 1, dyn_ref[...], 0.0)
        abuf[0:CONV_HALO, :] = jnp.where(i > 0, ap_ref[...], 0.0)
        abuf[CONV_HALO:, :] = a_ref[...]
        db_ref[...] += _colsum(dy_ref[...])
        for cs in range(c_dim // 128):
            lanes = pl.ds(cs * 128, 128)
            dyc = dy_ref[:, lanes]
            acc = jnp.zeros((tr, 128), F32)
            for k in range(CONV_K):
                acc = acc + w_ref[k:k + 1, lanes] * dbuf[pl.ds(CONV_K - 1 - k, tr), lanes]
                dw_ref[k:k + 1, lanes] += _colsum(dyc * abuf[pl.ds(k + 2, tr), lanes])
            da_buf[:, lanes] = acc
        da = da_buf[...]
        s = _sigmoid(gate_ref[...])
        dval_ref[...] = (da * s).astype(BF16)
        dgate_ref[...] = (da * val_ref[...] * s * (1.0 - s)).astype(BF16)

    row = lambda cblk: pl.BlockSpec((tr, c_dim), functools.partial(lambda c, i: (i, c), cblk))
    return pl.pallas_call(
        body,
        out_shape=[jax.ShapeDtypeStruct((t_dim, c_dim), BF16), jax.ShapeDtypeStruct((t_dim, c_dim), BF16),
                   jax.ShapeDtypeStruct((CONV_HALO, c_dim), F32), jax.ShapeDtypeStruct((1, c_dim), F32)],
        grid=(nb,),
        in_specs=[row(0), pl.BlockSpec((CONV_HALO, c_dim), lambda i: (jnp.minimum((i + 1) * hb, n_halo - 1), 0)),
                  row(0), pl.BlockSpec((CONV_HALO, c_dim), lambda i: (jnp.maximum(i * hb - 1, 0), 0)),
                  row(cb), row(cb + 1), pl.BlockSpec(w.shape, lambda i: (0, 0))],
        out_specs=[row(0), row(0), pl.BlockSpec((CONV_HALO, c_dim), lambda i: (0, 0)),
                   pl.BlockSpec((1, c_dim), lambda i: (0, 0))],
        scratch_shapes=[pltpu.VMEM((tr + CONV_HALO, c_dim), F32), pltpu.VMEM((tr + CONV_HALO, c_dim), F32),
                        pltpu.VMEM((tr, c_dim), F32)],
        compiler_params=_cparams(("arbitrary",)), name=name)(dy, dy, a, a, z_a, z_a, w)


def _mla_norm_fwd(z_c, q_g, kv_g, qr, kvr, name):
    t_dim, cwid = z_c.shape

    def fn(i, ri, fi, ro, ao):
        cq, _ = _rms(ri[0][:, :qr])
        ckv, _ = _rms(ri[0][:, qr:qr + kvr])
        ro[0][...] = (cq * fi[0][...]).astype(BF16)
        ro[1][...] = (ckv * fi[1][...]).astype(BF16)

    return _rowwise(fn, name, t_dim, ROW_TILE, [(z_c, cwid, 0)], [q_g, kv_g], [(qr, BF16), (kvr, BF16)])


def _rope(t, cf, s1, s2):
    return t * cf + pltpu.roll(t, 96, 1) * s1 + pltpu.roll(t, 32, 1) * s2


def _rope_t(g, cf, s1, s2):
    return g * cf + pltpu.roll(g * s1, 32, 1) + pltpu.roll(g * s2, 96, 1)


def _rope_fwd(q_pre, k_nope, z_c, cf, s1, s2, rope_blk, scale, name):
    t_dim = q_pre.shape[0]

    def fn(i, ri, fi, ro, ao):
        c, a, b = ri[3][...], ri[4][...], ri[5][...]
        kt = _rope(ri[2][...], c, a, b).astype(BF16)
        for h in range(N_HEADS):
            ro[0][:, h * HQ:h * HQ + NOPE] = (ri[0][:, h * HQ:h * HQ + NOPE] * scale).astype(BF16)
            ro[0][:, h * HQ + NOPE:(h + 1) * HQ] = (_rope(ri[0][:, h * HQ + NOPE:(h + 1) * HQ], c, a, b) * scale).astype(BF16)
            ro[1][:, h * HQ:h * HQ + NOPE] = ri[1][:, h * NOPE:(h + 1) * NOPE].astype(BF16)
            ro[1][:, h * HQ + NOPE:(h + 1) * HQ] = kt

    return _rowwise(fn, name, t_dim, ROW_TILE_WIDE,
                    [(q_pre, N_HEADS * HQ, 0), (k_nope, N_HEADS * NOPE, 0), (z_c, 128, rope_blk),
                     (cf, 128, 0), (s1, 128, 0), (s2, 128, 0)], [],
                    [(N_HEADS * HQ, BF16), (N_HEADS * HQ, BF16)])


def _rope_bwd(dq_cat, dk_cat, cf, s1, s2, scale, name):
    t_dim = dq_cat.shape[0]

    def fn(i, ri, fi, ro, ao):
        c, a, b = ri[2][...], ri[3][...], ri[4][...]
        dkt = jnp.zeros((ri[0].shape[0], 128), F32)
        for h in range(N_HEADS):
            ro[0][:, h * HQ:h * HQ + NOPE] = (ri[0][:, h * HQ:h * HQ + NOPE] * scale).astype(BF16)
            ro[0][:, h * HQ + NOPE:(h + 1) * HQ] = _rope_t(ri[0][:, h * HQ + NOPE:(h + 1) * HQ] * scale, c, a, b).astype(BF16)
            ro[1][:, h * NOPE:(h + 1) * NOPE] = ri[1][:, h * HQ:h * HQ + NOPE].astype(BF16)
            dkt = dkt + ri[1][:, h * HQ + NOPE:(h + 1) * HQ]
        ro[2][...] = _rope_t(dkt, c, a, b)

    return _rowwise(fn, name, t_dim, ROW_TILE_WIDE,
                    [(dq_cat, N_HEADS * HQ, 0), (dk_cat, N_HEADS * HQ, 0), (cf, 128, 0), (s1, 128, 0), (s2, 128, 0)],
                    [], [(N_HEADS * HQ, BF16), (N_HEADS * NOPE, BF16), (128, F32)])


def _mla_norm_bwd(z_c, dcq, dckv_k, dckv_v, dkr, q_g, kv_g, qr, kvr, name):
    t_dim, cwid = z_c.shape

    def fn(i, ri, fi, ro, ao):
        cq, rq = _rms(ri[0][:, :qr])
        ckv, rkv = _rms(ri[0][:, qr:qr + kvr])
        dq = ri[1][...]
        dkv = ri[2][...] + ri[3][...]
        ao[0][...] += _colsum(dq * cq)
        ao[1][...] += _colsum(dkv * ckv)
        ro[0][:, :qr] = _rms_bwd(cq, rq, dq * fi[0][...]).astype(BF16)
        ro[0][:, qr:qr + kvr] = _rms_bwd(ckv, rkv, dkv * fi[1][...]).astype(BF16)
        ro[0][:, qr + kvr:] = ri[4][...].astype(BF16)

    return _rowwise(fn, name, t_dim, ROW_TILE,
                    [(z_c, cwid, 0), (dcq, qr, 0), (dckv_k, kvr, 0), (dckv_v, kvr, 0), (dkr, 128, 0)], [q_g, kv_g],
                    [(cwid, BF16)], [((1, qr), F32), ((1, kvr), F32)])


def _causal_pairs(n, by_key):
    if by_key:
        pairs = [(i, j) for j in range(n) for i in range(j, n)]
    else:
        pairs = [(i, j) for i in range(n) for j in range(i + 1)]
    return (np.array([p[0] for p in pairs], np.int32), np.array([p[1] for p in pairs], np.int32))


def _attn_fwd(q, k, v, name):
    t_dim = q.shape[0]
    tq = min(ATT_TILE, t_dim)
    qi, kj = _causal_pairs(t_dim // tq, by_key=False)

    def body(qi_ref, kj_ref, q_ref, k_ref, v_ref, o_ref, lse_ref, m_sc, l_sc, acc_sc):
        s_id = pl.program_id(1)
        i, j = qi_ref[s_id], kj_ref[s_id]

        @pl.when(j == 0)
        def _():
            m_sc[...] = jnp.full_like(m_sc, NEG)
            l_sc[...] = jnp.zeros_like(l_sc)
            acc_sc[...] = jnp.zeros_like(acc_sc)

        s = lax.dot_general(q_ref[...], k_ref[...], _DIMS["nt"], preferred_element_type=F32)
        row = lax.broadcasted_iota(jnp.int32, (tq, tq), 0)
        col = lax.broadcasted_iota(jnp.int32, (tq, tq), 1)
        s = jnp.where(col - row <= (i - j) * tq, s, NEG)
        m_new = jnp.maximum(m_sc[...], jnp.max(s, axis=1, keepdims=True))
        alpha = jnp.exp(m_sc[...] - m_new)
        p = jnp.exp(s - m_new)
        l_sc[...] = alpha * l_sc[...] + jnp.sum(p, axis=1, keepdims=True)
        acc_sc[...] = alpha * acc_sc[...] + jnp.dot(p.astype(BF16), v_ref[...], preferred_element_type=F32)
        m_sc[...] = m_new

        @pl.when(j == i)
        def _():
            o_ref[...] = (acc_sc[...] / l_sc[...]).astype(BF16)
            lse_ref[0] = m_sc[...] + jnp.log(l_sc[...])

    grid_spec = pltpu.PrefetchScalarGridSpec(
        num_scalar_prefetch=2, grid=(N_HEADS, len(qi)),
        in_specs=[pl.BlockSpec((tq, HQ), lambda h, s, qi, kj: (qi[s], h)),
                  pl.BlockSpec((tq, HQ), lambda h, s, qi, kj: (kj[s], h)),
                  pl.BlockSpec((tq, VDIM), lambda h, s, qi, kj: (kj[s], h))],
        out_specs=[pl.BlockSpec((tq, VDIM), lambda h, s, qi, kj: (qi[s], h)),
                   pl.BlockSpec((1, tq, 1), lambda h, s, qi, kj: (h, qi[s], 0))],
        scratch_shapes=[pltpu.VMEM((tq, 1), F32), pltpu.VMEM((tq, 1), F32), pltpu.VMEM((tq, VDIM), F32)])
    return pl.pallas_call(
        body, grid_spec=grid_spec,
        out_shape=[jax.ShapeDtypeStruct((t_dim, N_HEADS * VDIM), BF16), jax.ShapeDtypeStruct((N_HEADS, t_dim, 1), F32)],
        compiler_params=_cparams(("parallel", "arbitrary")), name=name)(jnp.asarray(qi), jnp.asarray(kj), q, k, v)


def _attn_delta(do, o, name):
    t_dim = do.shape[0]
    tr = min(ATT_TILE, t_dim)

    def body(do_ref, o_ref, d_ref):
        d_ref[0] = jnp.sum(do_ref[...].astype(F32) * o_ref[...].astype(F32), axis=1, keepdims=True)

    return pl.pallas_call(
        body, out_shape=jax.ShapeDtypeStruct((N_HEADS, t_dim, 1), F32), grid=(N_HEADS, t_dim // tr),
        in_specs=[pl.BlockSpec((tr, VDIM), lambda h, i: (i, h)), pl.BlockSpec((tr, VDIM), lambda h, i: (i, h))],
        out_specs=pl.BlockSpec((1, tr, 1), lambda h, i: (h, i, 0)),
        compiler_params=_cparams(("parallel", "parallel")), name=name)(do, o)


def _attn_bwd(q, k, v, do, lse_row, delta_row, name):
    t_dim = q.shape[0]
    tq = min(ATT_TILE, t_dim)
    nq = t_dim // tq
    qi, kj = _causal_pairs(nq, by_key=True)

    def body(qi_ref, kj_ref, q_ref, k_ref, v_ref, do_ref, lse_ref, dl_ref, dq_ref, dk_ref, dv_ref, dk_sc, dv_sc):
        s_id = pl.program_id(1)
        i, j = qi_ref[s_id], kj_ref[s_id]

        @pl.when(s_id == 0)
        def _():
            dq_ref[...] = jnp.zeros_like(dq_ref)

        @pl.when(i == j)
        def _():
            dk_sc[...] = jnp.zeros_like(dk_sc)
            dv_sc[...] = jnp.zeros_like(dv_sc)

        s_t = lax.dot_general(k_ref[...], q_ref[...], _DIMS["nt"], preferred_element_type=F32)
        row = lax.broadcasted_iota(jnp.int32, (tq, tq), 0)
        col = lax.broadcasted_iota(jnp.int32, (tq, tq), 1)
        p_t = jnp.where(row - col <= (i - j) * tq, jnp.exp(s_t - lse_ref[0]), 0.0)
        do = do_ref[...]
        dv_sc[...] += jnp.dot(p_t.astype(BF16), do, preferred_element_type=F32)
        dp_t = lax.dot_general(v_ref[...], do, _DIMS["nt"], preferred_element_type=F32)
        ds_t = (p_t * (dp_t - dl_ref[0])).astype(BF16)
        dk_sc[...] += jnp.dot(ds_t, q_ref[...], preferred_element_type=F32)
        rows = pl.ds(pl.multiple_of(i * tq, tq), tq)
        dq_ref[rows, :] += lax.dot_general(ds_t, k_ref[...], _DIMS["tn"], preferred_element_type=F32)

        @pl.when(i == nq - 1)
        def _():
            dk_ref[...] = dk_sc[...]
            dv_ref[...] = dv_sc[...].astype(BF16)

    grid_spec = pltpu.PrefetchScalarGridSpec(
        num_scalar_prefetch=2, grid=(N_HEADS, len(qi)),
        in_specs=[pl.BlockSpec((tq, HQ), lambda h, s, qi, kj: (qi[s], h)),
                  pl.BlockSpec((tq, HQ), lambda h, s, qi, kj: (kj[s], h)),
                  pl.BlockSpec((tq, VDIM), lambda h, s, qi, kj: (kj[s], h)),
                  pl.BlockSpec((tq, VDIM), lambda h, s, qi, kj: (qi[s], h)),
                  pl.BlockSpec((1, 1, tq), lambda h, s, qi, kj: (h, 0, qi[s])),
                  pl.BlockSpec((1, 1, tq), lambda h, s, qi, kj: (h, 0, qi[s]))],
        out_specs=[pl.BlockSpec((t_dim, HQ), lambda h, s, qi, kj: (0, h)),
                   pl.BlockSpec((tq, HQ), lambda h, s, qi, kj: (kj[s], h)),
                   pl.BlockSpec((tq, VDIM), lambda h, s, qi, kj: (kj[s], h))],
        scratch_shapes=[pltpu.VMEM((tq, HQ), F32), pltpu.VMEM((tq, VDIM), F32)])
    return pl.pallas_call(
        body, grid_spec=grid_spec,
        out_shape=[jax.ShapeDtypeStruct((t_dim, N_HEADS * HQ), F32), jax.ShapeDtypeStruct((t_dim, N_HEADS * HQ), F32),
                   jax.ShapeDtypeStruct((t_dim, N_HEADS * VDIM), BF16)],
        compiler_params=_cparams(("parallel", "arbitrary")), name=name)(
            jnp.asarray(qi), jnp.asarray(kj), q, k, v, do, lse_row, delta_row)


def _merge_fwd(z_g, y_a, y_b, y_c, name):
    t_dim, d = y_a.shape

    def fn(i, ri, fi, ro, ao):
        acc = _sigmoid(ri[0][:, :d]) * ri[1][...]
        acc = acc + _sigmoid(ri[0][:, d:2 * d]) * ri[2][...]
        acc = acc + _sigmoid(ri[0][:, 2 * d:]) * ri[3][...]
        ro[0][...] = acc.astype(BF16)

    return _rowwise(fn, name, t_dim, ROW_TILE_WIDE, [(z_g, 3 * d, 0), (y_a, d, 0), (y_b, d, 0), (y_c, d, 0)], [],
                    [(d, BF16)])[0]


def _merge_bwd(z_g, y_a, y_b, y_c, dmerged, name):
    t_dim, d = y_a.shape

    def fn(i, ri, fi, ro, ao):
        dm = ri[4][...]
        for q in range(3):
            s = _sigmoid(ri[0][:, q * d:(q + 1) * d])
            ro[q][...] = (s * dm).astype(BF16)
            ro[3][:, q * d:(q + 1) * d] = (dm * ri[1 + q][...] * s * (1.0 - s)).astype(BF16)

    return _rowwise(fn, name, t_dim, ROW_TILE_WIDE,
                    [(z_g, 3 * d, 0), (y_a, d, 0), (y_b, d, 0), (y_c, d, 0), (dmerged, d, 0)], [],
                    [(d, BF16), (d, BF16), (d, BF16), (3 * d, BF16)])


def _adam(w, parts, m, v, name):
    r_dim, c_dim = w.shape
    limit = max(8, ADAM_BLOCK_ELEMS // c_dim // 8 * 8)
    tr = r_dim
    if r_dim > limit:
        tr = next((t for t in range(limit, 7, -8) if r_dim % t == 0), r_dim)

    def body(w_ref, p_ref, m_ref, v_ref, g_out, d_out, m_out, v_out):
        g = p_ref[0]
        for s in range(1, N_DEV):
            g = g + p_ref[s]
        m_new = ADAM_B1 * m_ref[...] + (1.0 - ADAM_B1) * g
        v_new = ADAM_B2 * v_ref[...] + (1.0 - ADAM_B2) * (g * g)
        m_hat = m_new / (1.0 - ADAM_B1 ** ADAM_STEP)
        v_hat = v_new / (1.0 - ADAM_B2 ** ADAM_STEP)
        g_out[...] = g
        d_out[...] = -ADAM_LR * (m_hat / (jnp.sqrt(v_hat) + ADAM_EPS) + ADAM_WD * w_ref[...])
        m_out[...] = m_new
        v_out[...] = v_new

    blk = pl.BlockSpec((tr, c_dim), lambda i: (i, 0))
    return pl.pallas_call(
        body, out_shape=[jax.ShapeDtypeStruct((r_dim, c_dim), F32)] * 4, grid=(r_dim // tr,),
        in_specs=[blk, pl.BlockSpec((N_DEV, tr, c_dim), lambda i: (0, i, 0)), blk, blk], out_specs=[blk] * 4,
        compiler_params=_cparams(("parallel",)), name=name)(w, parts, m, v)


def _me_and_peers():
    x, y, c = lax.axis_index("x"), lax.axis_index("y"), lax.axis_index("c")
    me = 4 * x + 2 * y + c
    peers = []
    for k in range(1, N_DEV):
        px, py, pc = x ^ (k >> 2), y ^ ((k >> 1) & 1), c ^ (k & 1)
        peers.append(((px, py, pc), 4 * px + 2 * py + pc))
    return me, peers


def _comm(arrays, exchange, name):
    n = len(arrays)
    hbm = pl.BlockSpec(memory_space=pltpu.HBM)

    def body(*refs):
        ins, outs = refs[:n], refs[n:2 * n]
        send_sems, recv_sems, local_sems = refs[2 * n:]
        me, peers = _me_and_peers()

        def src(w, dest_idx):
            return ins[w].at[dest_idx] if exchange else ins[w]

        local = [pltpu.make_async_copy(src(w, me), outs[w].at[me], local_sems.at[w]) for w in range(n)]
        for cp in local:
            cp.start()
        sends = []
        for w in range(n):
            for k, (dev, idx) in enumerate(peers):
                cp = pltpu.make_async_remote_copy(
                    src_ref=src(w, idx), dst_ref=outs[w].at[me], send_sem=send_sems.at[w * (N_DEV - 1) + k],
                    recv_sem=recv_sems.at[w * (N_DEV - 1) + k], device_id=dev, device_id_type=pl.DeviceIdType.MESH)
                cp.start()
                sends.append(cp)
        for w in range(n):
            for k, (dev, idx) in enumerate(peers):
                pltpu.make_async_remote_copy(
                    src_ref=src(w, idx), dst_ref=outs[w].at[idx], send_sem=send_sems.at[w * (N_DEV - 1) + k],
                    recv_sem=recv_sems.at[w * (N_DEV - 1) + k], device_id=dev, device_id_type=pl.DeviceIdType.MESH).wait_recv()
        for cp in sends:
            cp.wait_send()
        for cp in local:
            cp.wait()

    out_shape = [jax.ShapeDtypeStruct(a.shape if exchange else (N_DEV,) + a.shape, a.dtype) for a in arrays]
    return pl.pallas_call(
        body, out_shape=out_shape, in_specs=[hbm] * n, out_specs=[hbm] * n,
        scratch_shapes=[pltpu.SemaphoreType.DMA((n * (N_DEV - 1),)), pltpu.SemaphoreType.DMA((n * (N_DEV - 1),)),
                        pltpu.SemaphoreType.DMA((n,))],
        name=name)(*arrays)


def _unshard(name, g):
    if name in COL_SHARDED:
        return jnp.transpose(g, (1, 0, 2)).reshape(g.shape[1], N_DEV * g.shape[2])
    return g.reshape(N_DEV * g.shape[1], g.shape[2])


def _to_shards(name, full):
    if name in COL_SHARDED:
        r, c = full.shape
        return jnp.transpose(full.reshape(r, N_DEV, c // N_DEV), (1, 0, 2))
    return full.reshape(N_DEV, full.shape[0] // N_DEV, full.shape[1])


def _ffn_fwd(x, p, tag):
    h = _norm_fwd(x, p["norm_pre"], f"{tag}_norm")
    gu = _mm(h, p["w_in"], "nn", F32, f"{tag}_in")
    act = _swiglu_fwd(gu, f"{tag}_act")
    y = _mm(act, p["w_out"], "nn", F32, f"{tag}_out")
    x_new = _resid_fwd(x, y, p["norm_post"], 0.5, f"{tag}_resid")
    return x_new, dict(x=x, h=h, gu=gu, act=act, y=y)


def _ffn_bwd(dxo, s, p, tag):
    dy, dg_post = _resid_bwd(s["y"], dxo, p["norm_post"], 0.5, f"{tag}_resid_bwd")
    dact = _mm(dy, p["w_out"], "nt", F32, f"{tag}_out_dx")
    dw_out = _mm(s["act"], dy, "tn", F32, f"{tag}_out_dw")
    dgu = _swiglu_bwd(s["gu"], dact, f"{tag}_act_bwd")
    dh = _mm(dgu, p["w_in"], "nt", F32, f"{tag}_in_dx")
    dw_in = _mm(s["h"], dgu, "tn", F32, f"{tag}_in_dw")
    dx, dg_pre = _norm_bwd(s["x"], [dh], dxo, p["norm_pre"], f"{tag}_norm_bwd")
    return dx, dict(norm_pre=dg_pre, norm_post=dg_post, w_in=dw_in, w_out=dw_out)


def _mixer_fwd(x, p, rope_tabs, dims):
    gmw, cw, qr, kvr = dims["gmw"], dims["cw"], dims["qr"], dims["kvr"]
    cf, s1, s2 = rope_tabs
    scale = (NOPE + ROPE) ** -0.5
    h = _norm_fwd(x, p["norm_pre"], "mix_norm")
    z_a = _mm(h, p["w_a"], "nn", F32, "mix_in_a")
    z_c = _mm(h, p["w_c"], "nn", F32, "mix_in_c")
    z_g = _mm(h, p["w_g"], "nn", F32, "mix_in_g")
    o_a = _gm_fwd(z_a, p["gm_ln_g"], p["gm_ln_b"], p["gm_w_s"], p["gm_b_s"], gmw, "gm_fwd")
    a = _glu_fwd(z_a, gmw, cw, "glu_fwd")
    y_conv, o_b = _conv_fwd(a, p["conv_w"], p["conv_b"], p["conv_ln_g"], p["conv_ln_b"], "conv_fwd")
    cqn, ckvn = _mla_norm_fwd(z_c, p["q_norm"], p["kv_norm"], qr, kvr, "mla_norm")
    q_pre = _mm(cqn, p["w_uq"], "nn", F32, "mla_uq")
    k_nope = _mm(ckvn, p["w_uk"], "nn", F32, "mla_uk")
    v = _mm(ckvn, p["w_uv"], "nn", BF16, "mla_uv")
    q_cat, k_cat = _rope_fwd(q_pre, k_nope, z_c, cf, s1, s2, (qr + kvr) // 128, scale, "rope_fwd")
    o_c, lse = _attn_fwd(q_cat, k_cat, v, "attn_fwd")
    y_a = _mm(o_a, p["wb_a"], "nn", F32, "branch_a")
    y_b = _mm(o_b, p["wb_b"], "nn", F32, "branch_b")
    y_c = _mm(o_c, p["wb_c"], "nn", F32, "branch_c")
    merged = _merge_fwd(z_g, y_a, y_b, y_c, "merge_fwd")
    m = _mm(merged, p["w_out"], "nn", F32, "mix_out")
    x_new = _resid_fwd(x, m, p["norm_post"], 1.0, "mix_resid")
    saved = dict(x=x, h=h, z_a=z_a, z_c=z_c, z_g=z_g, o_a=o_a, a=a, y_conv=y_conv, o_b=o_b, cqn=cqn, ckvn=ckvn,
                 v=v, q_cat=q_cat, k_cat=k_cat, o_c=o_c, lse=lse, y_a=y_a, y_b=y_b, y_c=y_c, merged=merged, m=m)
    return x_new, saved


def _mixer_bwd(dxo, s, p, rope_tabs, dims):
    gmw, cw, qr, kvr = dims["gmw"], dims["cw"], dims["qr"], dims["kvr"]
    cf, s1, s2 = rope_tabs
    scale = (NOPE + ROPE) ** -0.5
    t_dim = dxo.shape[0]
    g = {}
    dm, g["norm_post"] = _resid_bwd(s["m"], dxo, p["norm_post"], 1.0, "mix_resid_bwd")
    dmerged = _mm(dm, p["w_out"], "nt", F32, "mix_out_dx")
    g["w_out"] = _mm(s["merged"], dm, "tn", F32, "mix_out_dw")
    dy_a, dy_b, dy_c, dz_g = _merge_bwd(s["z_g"], s["y_a"], s["y_b"], s["y_c"], dmerged, "merge_bwd")
    do_a = _mm(dy_a, p["wb_a"], "nt", F32, "branch_a_dx")
    do_b = _mm(dy_b, p["wb_b"], "nt", F32, "branch_b_dx")
    do_c = _mm(dy_c, p["wb_c"], "nt", BF16, "branch_c_dx")
    g["w_branch"] = jnp.concatenate([_mm(s["o_a"], dy_a, "tn", F32, "branch_a_dw"),
                                     _mm(s["o_b"], dy_b, "tn", F32, "branch_b_dw"),
                                     _mm(s["o_c"], dy_c, "tn", F32, "branch_c_dw")], axis=0)
    delta = _attn_delta(do_c, s["o_c"], "attn_delta")
    dq_cat, dk_cat, dv = _attn_bwd(s["q_cat"], s["k_cat"], s["v"], do_c, s["lse"].reshape(N_HEADS, 1, t_dim),
                                   delta.reshape(N_HEADS, 1, t_dim), "attn_bwd")
    dq_pre, dk_nope, dkr = _rope_bwd(dq_cat, dk_cat, cf, s1, s2, scale, "rope_bwd")
    dcq = _mm(dq_pre, p["w_uq"], "nt", F32, "mla_uq_dx")
    g["w_uq"] = _mm(s["cqn"], dq_pre, "tn", F32, "mla_uq_dw")
    dckv_k = _mm(dk_nope, p["w_uk"], "nt", F32, "mla_uk_dx")
    dckv_v = _mm(dv, p["w_uv"], "nt", F32, "mla_uv_dx")
    g["w_uk"] = _mm(s["ckvn"], dk_nope, "tn", F32, "mla_uk_dw")
    g["w_uv"] = _mm(s["ckvn"], dv, "tn", F32, "mla_uv_dw")
    dz_c, g["q_norm"], g["kv_norm"] = _mla_norm_bwd(s["z_c"], dcq, dckv_k, dckv_v, dkr, p["q_norm"], p["kv_norm"],
                                                    qr, kvr, "mla_norm_bwd")
    dy_conv, g["conv_ln_g"], g["conv_ln_b"] = _conv_ln_bwd(s["y_conv"], do_b, p["conv_ln_g"], p["conv_ln_b"], "conv_ln_bwd")
    dval, dgate, g["conv_w"], g["conv_b"] = _conv_bwd(dy_conv, s["a"], s["z_a"], p["conv_w"], gmw, "conv_bwd")
    da_u, da_v, g["gm_ln_g"], g["gm_ln_b"], g["gm_w_s"], g["gm_b_s"] = _gm_bwd(
        s["z_a"], do_a, p["gm_ln_g"], p["gm_ln_b"], p["gm_w_s"], p["gm_w_s_t"], p["gm_b_s"], gmw, "gm_bwd")
    dz_a = jnp.concatenate([da_u, da_v, dval, dgate], axis=1)
    dh_a = _mm(dz_a, p["w_a"], "nt", F32, "mix_in_a_dx")
    dh_c = _mm(dz_c, p["w_c"], "nt", F32, "mix_in_c_dx")
    dh_g = _mm(dz_g, p["w_g"], "nt", F32, "mix_in_g_dx")
    g["w_a"] = _mm(s["h"], dz_a, "tn", F32, "mix_in_a_dw")
    g["w_c"] = _mm(s["h"], dz_c, "tn", F32, "mix_in_c_dw")
    g["w_g"] = _mm(s["h"], dz_g, "tn", F32, "mix_in_g_dw")
    dx, g["norm_pre"] = _norm_bwd(s["x"], [dh_a, dh_c, dh_g], dxo, p["norm_pre"], "mix_norm_bwd")
    return dx, g


def _layer_params(full, small, conv_w_full, dims, l):
    gmw, cw, qr, kvr, d = dims["gmw"], dims["cw"], dims["qr"], dims["kvr"], dims["d"]
    row = lambda a: a[l][None, :]
    ffn = lambda k: dict(norm_pre=row(small[f"{k}_norm_pre"]), norm_post=row(small[f"{k}_norm_post"]),
                         w_in=full[f"{k}_w_in"], w_out=full[f"{k}_w_out"])
    w_in = full["mix_w_in"]
    a_end = 2 * gmw + 2 * cw
    c_end = a_end + qr + kvr + ROPE
    w_c = jnp.concatenate([w_in[:, a_end:c_end], jnp.zeros((d, 128 - ROPE), w_in.dtype)], axis=1)
    w_uq = full["mla_w_uq"].reshape(qr, N_HEADS, NOPE + ROPE)
    w_uq = jnp.concatenate([w_uq, jnp.zeros((qr, N_HEADS, HQ - NOPE - ROPE), w_uq.dtype)], axis=2).reshape(qr, N_HEADS * HQ)
    w_ukv = full["mla_w_ukv"].reshape(kvr, N_HEADS, NOPE + VDIM)
    w_b = full["mix_w_branch"]
    w_s = small["gm_w_s"][l]
    mix = dict(norm_pre=row(small["mix_norm_pre"]), norm_post=row(small["mix_norm_post"]),
               w_a=w_in[:, :a_end], w_c=w_c, w_g=w_in[:, c_end:],
               gm_ln_g=row(small["gm_ln_g"]), gm_ln_b=row(small["gm_ln_b"]), gm_w_s=w_s,
               gm_w_s_t=jnp.transpose(w_s, (0, 2, 1)), gm_b_s=small["gm_b_s"][l][:, :, None],
               conv_w=jnp.concatenate([conv_w_full, jnp.zeros((CONV_HALO - CONV_K, cw), F32)], axis=0),
               conv_b=row(small["conv_b"]), conv_ln_g=row(small["conv_ln_g"]), conv_ln_b=row(small["conv_ln_b"]),
               q_norm=row(small["mla_q_norm"]), kv_norm=row(small["mla_kv_norm"]),
               w_uq=w_uq, w_uk=w_ukv[:, :, :NOPE].reshape(kvr, N_HEADS * NOPE),
               w_uv=w_ukv[:, :, NOPE:].reshape(kvr, N_HEADS * VDIM),
               wb_a=w_b[:gmw], wb_b=w_b[gmw:gmw + cw], wb_c=w_b[gmw + cw:], w_out=full["mix_w_out"])
    return dict(ffn1=ffn("ffn1"), mix=mix, ffn2=ffn("ffn2"))


def _layer_grads(g1, gm, g2, dims):
    qr, kvr = dims["qr"], dims["kvr"]
    big = {
        "ffn1_w_in": g1["w_in"], "ffn1_w_out": g1["w_out"], "ffn2_w_in": g2["w_in"], "ffn2_w_out": g2["w_out"],
        "mix_w_in": jnp.concatenate([gm["w_a"], gm["w_c"][:, :qr + kvr + ROPE], gm["w_g"]], axis=1),
        "mla_w_uq": gm["w_uq"].reshape(qr, N_HEADS, HQ)[:, :, :NOPE + ROPE].reshape(qr, N_HEADS * (NOPE + ROPE)),
        "mla_w_ukv": jnp.concatenate([gm["w_uk"].reshape(kvr, N_HEADS, NOPE), gm["w_uv"].reshape(kvr, N_HEADS, VDIM)],
                                     axis=2).reshape(kvr, N_HEADS * (NOPE + VDIM)),
        "mix_w_branch": gm["w_branch"], "mix_w_out": gm["w_out"], "conv_w": gm["conv_w"][:CONV_K],
    }
    small = {
        "ffn1_norm_pre": g1["norm_pre"][0], "ffn1_norm_post": g1["norm_post"][0],
        "ffn2_norm_pre": g2["norm_pre"][0], "ffn2_norm_post": g2["norm_post"][0],
        "mix_norm_pre": gm["norm_pre"][0], "mix_norm_post": gm["norm_post"][0],
        "gm_ln_g": gm["gm_ln_g"][0], "gm_ln_b": gm["gm_ln_b"][0], "gm_w_s": gm["gm_w_s"], "gm_b_s": gm["gm_b_s"][:, :, 0],
        "conv_b": gm["conv_b"][0], "conv_ln_g": gm["conv_ln_g"][0], "conv_ln_b": gm["conv_ln_b"][0],
        "mla_q_norm": gm["q_norm"][0], "mla_kv_norm": gm["kv_norm"][0],
    }
    return big, small


def _rope_tables(positions):
    inv_freq = ROPE_THETA ** (-jnp.arange(0, ROPE, 2, dtype=F32) / ROPE)
    ang = positions.astype(F32)[:, None] * inv_freq
    cos, sin = jnp.cos(ang), jnp.sin(ang)
    z = lambda w: jnp.zeros((positions.shape[0], w), F32)
    return (jnp.concatenate([cos, cos, z(64)], axis=1), jnp.concatenate([-sin, z(96)], axis=1),
            jnp.concatenate([z(32), sin, z(64)], axis=1))


def _pad_rows(flat, mult):
    n = flat.shape[0]
    pad = (-n) % mult
    return jnp.concatenate([flat, jnp.zeros((pad,), flat.dtype)]) if pad else flat


def kernel(x, positions, ffn1_norm_pre, ffn1_norm_post, ffn1_w_in, ffn1_w_out, mix_norm_pre, mix_norm_post, mix_w_in, gm_ln_g, gm_ln_b, gm_w_s, gm_b_s, conv_w, conv_b, conv_ln_g, conv_ln_b, mla_q_norm, mla_w_uq, mla_kv_norm, mla_w_ukv, mix_w_branch, mix_w_out, ffn2_norm_pre, ffn2_norm_post, ffn2_w_in, ffn2_w_out, loss_target, m_ffn1_norm_pre, m_ffn1_norm_post, m_ffn1_w_in, m_ffn1_w_out, m_mix_norm_pre, m_mix_norm_post, m_mix_w_in, m_gm_ln_g, m_gm_ln_b, m_gm_w_s, m_gm_b_s, m_conv_w, m_conv_b, m_conv_ln_g, m_conv_ln_b, m_mla_q_norm, m_mla_w_uq, m_mla_kv_norm, m_mla_w_ukv, m_mix_w_branch, m_mix_w_out, m_ffn2_norm_pre, m_ffn2_norm_post, m_ffn2_w_in, m_ffn2_w_out, v_ffn1_norm_pre, v_ffn1_norm_post, v_ffn1_w_in, v_ffn1_w_out, v_mix_norm_pre, v_mix_norm_post, v_mix_w_in, v_gm_ln_g, v_gm_ln_b, v_gm_w_s, v_gm_b_s, v_conv_w, v_conv_b, v_conv_ln_g, v_conv_ln_b, v_mla_q_norm, v_mla_w_uq, v_mla_kv_norm, v_mla_w_ukv, v_mix_w_branch, v_mix_w_out, v_ffn2_norm_pre, v_ffn2_norm_post, v_ffn2_w_in, v_ffn2_w_out):
    w = dict(zip(WEIGHTS, (ffn1_norm_pre, ffn1_norm_post, ffn1_w_in, ffn1_w_out, mix_norm_pre, mix_norm_post, mix_w_in, gm_ln_g, gm_ln_b, gm_w_s, gm_b_s, conv_w, conv_b, conv_ln_g, conv_ln_b, mla_q_norm, mla_w_uq, mla_kv_norm, mla_w_ukv, mix_w_branch, mix_w_out, ffn2_norm_pre, ffn2_norm_post, ffn2_w_in, ffn2_w_out)))
    mom_m = dict(zip(WEIGHTS, (m_ffn1_norm_pre, m_ffn1_norm_post, m_ffn1_w_in, m_ffn1_w_out, m_mix_norm_pre, m_mix_norm_post, m_mix_w_in, m_gm_ln_g, m_gm_ln_b, m_gm_w_s, m_gm_b_s, m_conv_w, m_conv_b, m_conv_ln_g, m_conv_ln_b, m_mla_q_norm, m_mla_w_uq, m_mla_kv_norm, m_mla_w_ukv, m_mix_w_branch, m_mix_w_out, m_ffn2_norm_pre, m_ffn2_norm_post, m_ffn2_w_in, m_ffn2_w_out)))
    mom_v = dict(zip(WEIGHTS, (v_ffn1_norm_pre, v_ffn1_norm_post, v_ffn1_w_in, v_ffn1_w_out, v_mix_norm_pre, v_mix_norm_post, v_mix_w_in, v_gm_ln_g, v_gm_ln_b, v_gm_w_s, v_gm_b_s, v_conv_w, v_conv_b, v_conv_ln_g, v_conv_ln_b, v_mla_q_norm, v_mla_w_uq, v_mla_kv_norm, v_mla_w_ukv, v_mix_w_branch, v_mix_w_out, v_ffn2_norm_pre, v_ffn2_norm_post, v_ffn2_w_in, v_ffn2_w_out)))
    n_layers = ffn1_norm_pre.shape[0]
    t_dim, d = x.shape[1], x.shape[2]
    dims = dict(d=d, gmw=gm_ln_g.shape[1], cw=conv_ln_g.shape[1], qr=mla_q_norm.shape[1], kvr=mla_kv_norm.shape[1])
    x0 = x.reshape(t_dim, d)
    target = loss_target.reshape(t_dim, d)
    rope_tabs = _rope_tables(positions.reshape(t_dim))

    conv_all = _unshard_conv(_comm([conv_w], False, "gather_conv")[0])
    params = []
    for l in range(n_layers):
        gathered = _comm([w[k][l].astype(BF16) for k in BIG], False, "gather_layer")
        full = {k: _unshard(k, gathered[q]) for q, k in enumerate(BIG)}
        params.append(_layer_params(full, w, conv_all[l], dims, l))

    saved = []
    xc = x0
    for l in range(n_layers):
        xc, s1 = _ffn_fwd(xc, params[l]["ffn1"], "ffn1")
        xc, sm = _mixer_fwd(xc, params[l]["mix"], rope_tabs, dims)
        xc, s2 = _ffn_fwd(xc, params[l]["ffn2"], "ffn2")
        saved.append((s1, sm, s2))
    dx, loss_part = _loss_fwd_bwd(xc, target, "loss")
    loss = lax.psum(loss_part[0, 0], ("x", "y", "c"))

    big_out = {k: [None] * n_layers for k in BIG + ("conv_w",)}
    small_parts = [None] * n_layers
    for l in reversed(range(n_layers)):
        s1, sm, s2 = saved[l]
        dx, g2 = _ffn_bwd(dx, s2, params[l]["ffn2"], "ffn2")
        dx, gm = _mixer_bwd(dx, sm, params[l]["mix"], rope_tabs, dims)
        dx, g1 = _ffn_bwd(dx, s1, params[l]["ffn1"], "ffn1")
        big, small_parts[l] = _layer_grads(g1, gm, g2, dims)
        names = BIG + ("conv_w",)
        recv = _comm([_to_shards(k, big[k]) for k in names], True, "exchange_layer")
        for q, k in enumerate(names):
            wl = w[k][l]
            r2 = (lambda a: a.reshape(-1, a.shape[-1]))
            outs = _adam(r2(wl), recv[q].reshape(N_DEV, -1, wl.shape[-1]), r2(mom_m[k][l]), r2(mom_v[k][l]), f"adam_{k}")
            big_out[k][l] = [o.reshape(wl.shape) for o in outs]
    grad_x = dx.reshape(x.shape)

    flat = lambda tree: _pad_rows(jnp.concatenate([tree[k].reshape(-1) for k in SMALL]), 256 * 128).reshape(-1, 128)
    g_small = flat({k: jnp.stack([small_parts[l][k] for l in range(n_layers)]) for k in SMALL})
    parts = _comm([g_small], False, "gather_small_grads")[0]
    s_outs = _adam(flat(w), parts, flat(mom_m), flat(mom_v), "adam_small")
    small_out = {k: [] for k in SMALL}
    for o in s_outs:
        o = o.reshape(-1)
        off = 0
        for k in SMALL:
            n = int(np.prod(w[k].shape))
            small_out[k].append(o[off:off + n].reshape(w[k].shape))
            off += n

    def out(which, k):
        if k in SMALL:
            return small_out[k][which]
        return jnp.stack([big_out[k][l][which] for l in range(n_layers)])

    return (loss, grad_x, *[out(0, k) for k in WEIGHTS], *[out(1, k) for k in WEIGHTS],
            *[out(2, k) for k in WEIGHTS], *[out(3, k) for k in WEIGHTS])


def _unshard_conv(g):
    n_dev, n_layers, k, c = g.shape
    return jnp.transpose(g, (1, 2, 0, 3)).reshape(n_layers, k, n_dev * c)
```

```python
import functools

import numpy as np
import jax
import jax.numpy as jnp
from jax import lax
from jax.experimental import pallas as pl
from jax.experimental.pallas import tpu as pltpu

F32 = jnp.float32
BF16 = jnp.bfloat16

N_DEV = 8
N_HEADS = 16
NOPE = 128
ROPE = 64
VDIM = 128
HQ = 256
GROUPS = 4
CHUNK = 128
CONV_K = 31
CONV_HALO = 32
EPS = 1e-6
ROPE_THETA = 10000.0
ADAM_LR = 0.001
ADAM_B1 = 0.9
ADAM_B2 = 0.999
ADAM_EPS = 1e-08
ADAM_WD = 0.01
ADAM_STEP = 10
NEG = -1e30

V7X_VMEM_LIMIT = 56 * 1024 * 1024
MM_TM, MM_TN, MM_TK = 1024, 1024, 2048
ROW_TILE = 256
ROW_TILE_WIDE = 128
ATT_TILE = 1024
ATT_FWD_HEADS = 2
CONV_TILE = 256
GM_TILE = 256
ADAM_BLOCK_ELEMS = 128 * 1024

BIG = ("ffn1_w_in", "ffn1_w_out", "mix_w_in", "mla_w_uq", "mla_w_ukv", "mix_w_branch", "mix_w_out",
       "ffn2_w_in", "ffn2_w_out")
COL_SHARDED = ("ffn1_w_in", "mix_w_in", "mla_w_uq", "mla_w_ukv", "ffn2_w_in", "conv_w")
SMALL = ("ffn1_norm_pre", "ffn1_norm_post", "mix_norm_pre", "mix_norm_post", "gm_ln_g", "gm_ln_b", "gm_w_s",
         "gm_b_s", "conv_b", "conv_ln_g", "conv_ln_b", "mla_q_norm", "mla_kv_norm", "ffn2_norm_pre",
         "ffn2_norm_post")
WEIGHTS = ("ffn1_norm_pre", "ffn1_norm_post", "ffn1_w_in", "ffn1_w_out", "mix_norm_pre", "mix_norm_post",
           "mix_w_in", "gm_ln_g", "gm_ln_b", "gm_w_s", "gm_b_s", "conv_w", "conv_b", "conv_ln_g", "conv_ln_b",
           "mla_q_norm", "mla_w_uq", "mla_kv_norm", "mla_w_ukv", "mix_w_branch", "mix_w_out", "ffn2_norm_pre",
           "ffn2_norm_post", "ffn2_w_in", "ffn2_w_out")


def _cparams(sem):
    return pltpu.CompilerParams(dimension_semantics=sem, vmem_limit_bytes=V7X_VMEM_LIMIT)


def _pick(dim, pref):
    if dim <= pref:
        return dim
    t = pref
    while t >= 128:
        if dim % t == 0:
            return t
        t -= 128
    return dim


def _sigmoid(x):
    return 1.0 / (1.0 + jnp.exp(-x))


_GELU_C = 0.7978845608028654


def _gelu(x):
    t = jnp.tanh(_GELU_C * (x + 0.044715 * x * x * x))
    return 0.5 * x * (1.0 + t)


def _gelu_grad(x):
    t = jnp.tanh(_GELU_C * (x + 0.044715 * x * x * x))
    return 0.5 * (1.0 + t) + 0.5 * x * (1.0 - t * t) * _GELU_C * (1.0 + 3.0 * 0.044715 * x * x)


def _rms(x):
    r = lax.rsqrt(jnp.mean(x * x, axis=-1, keepdims=True) + EPS)
    return x * r, r


def _rms_bwd(xn, r, t):
    return r * (t - xn * jnp.mean(t * xn, axis=-1, keepdims=True))


def _colsum(x):
    return jnp.sum(x, axis=0, keepdims=True)


_DIMS = {"nn": (((1,), (0,)), ((), ())), "nt": (((1,), (1,)), ((), ())), "tn": (((0,), (0,)), ((), ()))}


def _mm(a, b, mode, out_dtype, name):
    if mode == "tn":
        k_dim, m_dim = a.shape
    else:
        m_dim, k_dim = a.shape
    n_dim = b.shape[0] if mode == "nt" else b.shape[1]
    tm, tn, tk = _pick(m_dim, MM_TM), _pick(n_dim, MM_TN), _pick(k_dim, MM_TK)
    nk = k_dim // tk
    dims = _DIMS[mode]

    def body(a_ref, b_ref, o_ref, *acc):
        prod = lax.dot_general(a_ref[...], b_ref[...], dims, preferred_element_type=F32)
        if nk == 1:
            o_ref[...] = prod.astype(o_ref.dtype)
            return
        acc_ref, k = acc[0], pl.program_id(2)

        @pl.when(k == 0)
        def _():
            acc_ref[...] = prod

        @pl.when(jnp.logical_and(k > 0, k < nk - 1))
        def _():
            acc_ref[...] += prod

        @pl.when(k == nk - 1)
        def _():
            o_ref[...] = (acc_ref[...] + prod).astype(o_ref.dtype)

    if mode == "tn":
        a_spec = pl.BlockSpec((tk, tm), lambda i, j, k: (k, i))
    else:
        a_spec = pl.BlockSpec((tm, tk), lambda i, j, k: (i, k))
    if mode == "nt":
        b_spec = pl.BlockSpec((tn, tk), lambda i, j, k: (j, k))
    else:
        b_spec = pl.BlockSpec((tk, tn), lambda i, j, k: (k, j))
    return pl.pallas_call(
        body, out_shape=jax.ShapeDtypeStruct((m_dim, n_dim), out_dtype),
        grid=(m_dim // tm, n_dim // tn, nk), in_specs=[a_spec, b_spec],
        out_specs=pl.BlockSpec((tm, tn), lambda i, j, k: (i, j)),
        scratch_shapes=[pltpu.VMEM((tm, tn), F32)] if nk > 1 else [],
        compiler_params=_cparams(("parallel", "parallel", "arbitrary")), name=name)(a, b)


def _rowwise(fn, name, n_rows, tr, row_ins, full_ins, row_outs, acc_outs=()):
    tr = min(tr, n_rows)
    n_ri, n_fi, n_ro = len(row_ins), len(full_ins), len(row_outs)

    def body(*refs):
        i = pl.program_id(0)
        ri, fi = refs[:n_ri], refs[n_ri:n_ri + n_fi]
        ro, ao = refs[n_ri + n_fi:n_ri + n_fi + n_ro], refs[n_ri + n_fi + n_ro:]

        @pl.when(i == 0)
        def _():
            for r in ao:
                r[...] = jnp.zeros_like(r)

        fn(i, ri, fi, ro, ao)

    in_specs = [pl.BlockSpec((tr, w), functools.partial(lambda c, i: (i, c), cb)) for _, w, cb in row_ins]
    in_specs += [pl.BlockSpec(a.shape, functools.partial(lambda nd, i: (0,) * nd, a.ndim)) for a in full_ins]
    out_specs = [pl.BlockSpec((tr, w), lambda i: (i, 0)) for w, _ in row_outs]
    out_specs += [pl.BlockSpec(s, functools.partial(lambda nd, i: (0,) * nd, len(s))) for s, _ in acc_outs]
    out_shape = [jax.ShapeDtypeStruct((n_rows, w), d) for w, d in row_outs]
    out_shape += [jax.ShapeDtypeStruct(s, d) for s, d in acc_outs]
    return pl.pallas_call(
        body, out_shape=out_shape, grid=(n_rows // tr,), in_specs=in_specs, out_specs=out_specs,
        compiler_params=_cparams(("arbitrary",)), name=name)(*[a for a, _, _ in row_ins], *full_ins)


def _norm_fwd(x, g, name):
    t_dim, d = x.shape

    def fn(i, ri, fi, ro, ao):
        xn, _ = _rms(ri[0][...])
        ro[0][...] = (xn * fi[0][...]).astype(BF16)

    return _rowwise(fn, name, t_dim, ROW_TILE, [(x, d, 0)], [g], [(d, BF16)])[0]


def _resid_fwd(x, y, g, coef, name):
    t_dim, d = x.shape

    def fn(i, ri, fi, ro, ao):
        yn, _ = _rms(ri[1][...])
        ro[0][...] = ri[0][...] + coef * (yn * fi[0][...])

    return _rowwise(fn, name, t_dim, ROW_TILE, [(x, d, 0), (y, d, 0)], [g], [(d, F32)])[0]


def _resid_bwd(y, dxo, g, coef, name):
    t_dim, d = y.shape

    def fn(i, ri, fi, ro, ao):
        yn, r = _rms(ri[0][...])
        dyn = coef * ri[1][...]
        ao[0][...] += _colsum(dyn * yn)
        ro[0][...] = _rms_bwd(yn, r, dyn * fi[0][...]).astype(BF16)

    return _rowwise(fn, name, t_dim, ROW_TILE, [(y, d, 0), (dxo, d, 0)], [g], [(d, BF16)], [((1, d), F32)])


def _norm_bwd(x, dhs, dxo, g, name):
    t_dim, d = x.shape
    n = len(dhs)

    def fn(i, ri, fi, ro, ao):
        xn, r = _rms(ri[0][...])
        dh = ri[2][...]
        for q in range(1, n):
            dh = dh + ri[2 + q][...]
        ao[0][...] += _colsum(dh * xn)
        ro[0][...] = ri[1][...] + _rms_bwd(xn, r, dh * fi[0][...])

    return _rowwise(fn, name, t_dim, ROW_TILE, [(x, d, 0), (dxo, d, 0)] + [(a, d, 0) for a in dhs], [g],
                    [(d, F32)], [((1, d), F32)])


def _swiglu_fwd(gu, name):
    t_dim, f2 = gu.shape
    f = f2 // 2

    def fn(i, ri, fi, ro, ao):
        gate, up = ri[0][:, :f], ri[0][:, f:]
        ro[0][...] = (gate * _sigmoid(gate) * up).astype(BF16)

    return _rowwise(fn, name, t_dim, ROW_TILE_WIDE, [(gu, f2, 0)], [], [(f, BF16)])[0]


def _swiglu_bwd(gu, dact, name):
    t_dim, f2 = gu.shape
    f = f2 // 2

    def fn(i, ri, fi, ro, ao):
        gate, up = ri[0][:, :f], ri[0][:, f:]
        da = ri[1][...]
        s = _sigmoid(gate)
        ro[0][:, :f] = (da * up * (s * (1.0 + gate * (1.0 - s)))).astype(BF16)
        ro[0][:, f:] = (da * (gate * s)).astype(BF16)

    return _rowwise(fn, name, t_dim, ROW_TILE_WIDE, [(gu, f2, 0), (dact, f, 0)], [], [(f2, BF16)])[0]


def _loss_fwd_bwd(y, target, name):
    t_dim, d = y.shape

    def fn(i, ri, fi, ro, ao):
        err = ri[0][...] - ri[1][...]
        ao[0][...] += _colsum(jnp.sum(err * err, axis=1, keepdims=True)) * (0.5 / d)
        ro[0][...] = err * (1.0 / d)

    return _rowwise(fn, name, t_dim, ROW_TILE, [(y, d, 0), (target, d, 0)], [], [(d, F32)], [((1, 1), F32)])


def _ln_stats(v):
    mu = jnp.mean(v, axis=-1, keepdims=True)
    xc = v - mu
    rstd = lax.rsqrt(jnp.mean(xc * xc, axis=-1, keepdims=True) + EPS)
    return xc * rstd, rstd


def _tril_mask(upper=False):
    row = lax.broadcasted_iota(jnp.int32, (CHUNK, CHUNK), 0)
    col = lax.broadcasted_iota(jnp.int32, (CHUNK, CHUNK), 1)
    return row <= col if upper else row >= col


def _gm_fwd(z_a, ln_g, ln_b, w_s, b_s, gmw, name):
    t_dim = z_a.shape[0]
    gw = gmw // GROUPS
    tr = min(GM_TILE, t_dim)

    def fn(i, ri, fi, ro, ao):
        lng, lnb, ws_ref, bs_ref = fi
        u = _gelu(ri[0][...])
        vhat, _ = _ln_stats(_gelu(ri[1][...]))
        vn = (vhat * lng[...] + lnb[...]).astype(BF16)
        mask = _tril_mask()
        for g in range(GROUPS):
            wg = jnp.where(mask, ws_ref[g], 0.0).astype(BF16)
            for c in range(tr // CHUNK):
                rows, cols = slice(c * CHUNK, (c + 1) * CHUNK), slice(g * gw, (g + 1) * gw)
                s = jnp.dot(wg, vn[rows, cols], preferred_element_type=F32) + bs_ref[g]
                ro[0][rows, cols] = (u[rows, cols] * s).astype(BF16)

    return _rowwise(fn, name, t_dim, tr, [(z_a, gmw, 0), (z_a, gmw, 1)], [ln_g, ln_b, w_s, b_s], [(gmw, BF16)])[0]


def _gm_bwd(z_a, do_a, ln_g, ln_b, w_s, w_s_t, b_s, gmw, name):
    t_dim = z_a.shape[0]
    gw = gmw // GROUPS
    tr = min(GM_TILE, t_dim)

    def fn(i, ri, fi, ro, ao):
        lng, lnb, ws_ref, wst_ref, bs_ref = fi
        d_lng, d_lnb, d_ws, d_bs = ao
        a_u, a_v, do = ri[0][...], ri[1][...], ri[2][...]
        u = _gelu(a_u)
        vhat, rstd = _ln_stats(_gelu(a_v))
        vn = (vhat * lng[...] + lnb[...]).astype(BF16)
        mask = _tril_mask()
        wgs = [jnp.where(mask, ws_ref[g], 0.0).astype(BF16) for g in range(GROUPS)]
        wgts = [jnp.where(_tril_mask(upper=True), wst_ref[g], 0.0).astype(BF16) for g in range(GROUPS)]
        for c in range(tr // CHUNK):
            rows = slice(c * CHUNK, (c + 1) * CHUNK)
            dvn_parts = []
            for g in range(GROUPS):
                cols = slice(g * gw, (g + 1) * gw)
                vn_blk = vn[rows, cols]
                s = jnp.dot(wgs[g], vn_blk, preferred_element_type=F32) + bs_ref[g]
                ro[0][rows, cols] = (do[rows, cols] * s * _gelu_grad(a_u[rows, cols])).astype(BF16)
                ds = do[rows, cols] * u[rows, cols]
                d_bs[g] += jnp.sum(ds, axis=1, keepdims=True)
                dsb = ds.astype(BF16)
                dw = lax.dot_general(dsb, vn_blk, _DIMS["nt"], preferred_element_type=F32)
                d_ws[g] += jnp.where(mask, dw, 0.0)
                dvn_parts.append(jnp.dot(wgts[g], dsb, preferred_element_type=F32))
            dvn = jnp.concatenate(dvn_parts, axis=1)
            vh, rs = vhat[rows], rstd[rows]
            d_lng[...] += _colsum(dvn * vh)
            d_lnb[...] += _colsum(dvn)
            dvh = dvn * lng[...]
            dv = rs * (dvh - jnp.mean(dvh, axis=-1, keepdims=True) - vh * jnp.mean(dvh * vh, axis=-1, keepdims=True))
            ro[1][rows, :] = (dv * _gelu_grad(a_v[rows])).astype(BF16)

    return _rowwise(fn, name, t_dim, tr, [(z_a, gmw, 0), (z_a, gmw, 1), (do_a, gmw, 0)],
                    [ln_g, ln_b, w_s, w_s_t, b_s], [(gmw, BF16), (gmw, BF16)],
                    [((1, gmw), F32), ((1, gmw), F32), ((GROUPS, CHUNK, CHUNK), F32), ((GROUPS, CHUNK, 1), F32)])


def _glu_fwd(z_a, gmw, cw, name):
    t_dim = z_a.shape[0]
    cb = (2 * gmw) // cw

    def fn(i, ri, fi, ro, ao):
        ro[0][...] = ri[0][...] * _sigmoid(ri[1][...])

    return _rowwise(fn, name, t_dim, ROW_TILE, [(z_a, cw, cb), (z_a, cw, cb + 1)], [], [(cw, F32)])[0]


def _conv_fwd(a, w, b, ln_g, ln_b, name):
    t_dim, c_dim = a.shape
    tr = min(CONV_TILE, t_dim)
    hb = tr // CONV_HALO

    def body(cur_ref, prev_ref, w_ref, b_ref, g_ref, be_ref, y_ref, o_ref, buf):
        i = pl.program_id(0)
        buf[0:CONV_HALO, :] = jnp.where(i > 0, prev_ref[...], 0.0)
        buf[CONV_HALO:, :] = cur_ref[...]
        for cs in range(c_dim // 128):
            lanes = pl.ds(cs * 128, 128)
            acc = jnp.zeros((tr, 128), F32)
            for k in range(CONV_K):
                acc = acc + w_ref[k:k + 1, lanes] * buf[pl.ds(k + 2, tr), lanes]
            y_ref[:, lanes] = acc + b_ref[:, lanes]
        n_hat, _ = _ln_stats(y_ref[...])
        n = n_hat * g_ref[...] + be_ref[...]
        o_ref[...] = (n * _sigmoid(n)).astype(BF16)

    full = lambda arr: pl.BlockSpec(arr.shape, lambda i: (0, 0))
    return pl.pallas_call(
        body, out_shape=[jax.ShapeDtypeStruct((t_dim, c_dim), F32), jax.ShapeDtypeStruct((t_dim, c_dim), BF16)],
        grid=(t_dim // tr,),
        in_specs=[pl.BlockSpec((tr, c_dim), lambda i: (i, 0)),
                  pl.BlockSpec((CONV_HALO, c_dim), lambda i: (jnp.maximum(i * hb - 1, 0), 0)),
                  full(w), full(b), full(ln_g), full(ln_b)],
        out_specs=[pl.BlockSpec((tr, c_dim), lambda i: (i, 0)), pl.BlockSpec((tr, c_dim), lambda i: (i, 0))],
        scratch_shapes=[pltpu.VMEM((tr + CONV_HALO, c_dim), F32)],
        compiler_params=_cparams(("arbitrary",)), name=name)(a, a, w, b, ln_g, ln_b)


def _conv_ln_bwd(y, do_b, ln_g, ln_b, name):
    t_dim, c_dim = y.shape

    def fn(i, ri, fi, ro, ao):
        n_hat, rstd = _ln_stats(ri[0][...])
        n = n_hat * fi[0][...] + fi[1][...]
        s = _sigmoid(n)
        dn = ri[1][...] * (s * (1.0 + n * (1.0 - s)))
        ao[0][...] += _colsum(dn * n_hat)
        ao[1][...] += _colsum(dn)
        dnh = dn * fi[0][...]
        ro[0][...] = rstd * (dnh - jnp.mean(dnh, axis=-1, keepdims=True)
                             - n_hat * jnp.mean(dnh * n_hat, axis=-1, keepdims=True))

    return _rowwise(fn, name, t_dim, ROW_TILE, [(y, c_dim, 0), (do_b, c_dim, 0)], [ln_g, ln_b], [(c_dim, F32)],
                    [((1, c_dim), F32), ((1, c_dim), F32)])


def _conv_bwd(dy, a, z_a, w, gmw, name):
    t_dim, c_dim = a.shape
    tr = min(CONV_TILE, t_dim)
    hb = tr // CONV_HALO
    n_halo = t_dim // CONV_HALO
    nb = t_dim // tr
    cb = (2 * gmw) // c_dim

    def body(dy_ref, dyn_ref, a_ref, ap_ref, val_ref, gate_ref, w_ref, dval_ref, dgate_ref, dw_ref, db_ref,
             dbuf, abuf, da_buf):
        i = pl.program_id(0)

        @pl.when(i == 0)
        def _():
            dw_ref[...] = jnp.zeros_like(dw_ref)
            db_ref[...] = jnp.zeros_like(db_ref)

        dbuf[0:tr, :] = dy_ref[...]
        dbuf[tr:, :] = jnp.where(i < nb - 1, dyn_ref[...], 0.0)
        abuf[0:CONV_HALO, :] = jnp.where(i > 0, ap_ref[...], 0.0)
        abuf[CONV_HALO:, :] = a_ref[...]
        db_ref[...] += _colsum(dy_ref[...])
        for cs in range(c_dim // 128):
            lanes = pl.ds(cs * 128, 128)
            dyc = dy_ref[:, lanes]
            acc = jnp.zeros((tr, 128), F32)
            for k in range(CONV_K):
                acc = acc + w_ref[k:k + 1, lanes] * dbuf[pl.ds(CONV_K - 1 - k, tr), lanes]
                dw_ref[k:k + 1, lanes] += _colsum(dyc * abuf[pl.ds(k + 2, tr), lanes])
            da_buf[:, lanes] = acc
        da = da_buf[...]
        s = _sigmoid(gate_ref[...])
        dval_ref[...] = (da * s).astype(BF16)
        dgate_ref[...] = (da * val_ref[...] * s * (1.0 - s)).astype(BF16)

    row = lambda cblk: pl.BlockSpec((tr, c_dim), functools.partial(lambda c, i: (i, c), cblk))
    return pl.pallas_call(
        body,
        out_shape=[jax.ShapeDtypeStruct((t_dim, c_dim), BF16), jax.ShapeDtypeStruct((t_dim, c_dim), BF16),
                   jax.ShapeDtypeStruct((CONV_HALO, c_dim), F32), jax.ShapeDtypeStruct((1, c_dim), F32)],
        grid=(nb,),
        in_specs=[row(0), pl.BlockSpec((CONV_HALO, c_dim), lambda i: (jnp.minimum((i + 1) * hb, n_halo - 1), 0)),
                  row(0), pl.BlockSpec((CONV_HALO, c_dim), lambda i: (jnp.maximum(i * hb - 1, 0), 0)),
                  row(cb), row(cb + 1), pl.BlockSpec(w.shape, lambda i: (0, 0))],
        out_specs=[row(0), row(0), pl.BlockSpec((CONV_HALO, c_dim), lambda i: (0, 0)),
                   pl.BlockSpec((1, c_dim), lambda i: (0, 0))],
        scratch_shapes=[pltpu.VMEM((tr + CONV_HALO, c_dim), F32), pltpu.VMEM((tr + CONV_HALO, c_dim), F32),
                        pltpu.VMEM((tr, c_dim), F32)],
        compiler_params=_cparams(("arbitrary",)), name=name)(dy, dy, a, a, z_a, z_a, w)


def _mla_norm_fwd(z_c, q_g, kv_g, qr, kvr, name):
    t_dim, cwid = z_c.shape

    def fn(i, ri, fi, ro, ao):
        cq, _ = _rms(ri[0][:, :qr])
        ckv, _ = _rms(ri[0][:, qr:qr + kvr])
        ro[0][...] = (cq * fi[0][...]).astype(BF16)
        ro[1][...] = (ckv * fi[1][...]).astype(BF16)

    return _rowwise(fn, name, t_dim, ROW_TILE, [(z_c, cwid, 0)], [q_g, kv_g], [(qr, BF16), (kvr, BF16)])


def _rope(t, cf, s1, s2):
    return t * cf + pltpu.roll(t, 96, 1) * s1 + pltpu.roll(t, 32, 1) * s2


def _rope_t(g, cf, s1, s2):
    return g * cf + pltpu.roll(g * s1, 32, 1) + pltpu.roll(g * s2, 96, 1)


def _rope_fwd(q_pre, k_nope, z_c, cf, s1, s2, rope_blk, scale, name):
    t_dim = q_pre.shape[0]

    def fn(i, ri, fi, ro, ao):
        c, a, b = ri[3][...], ri[4][...], ri[5][...]
        kt = _rope(ri[2][...], c, a, b).astype(BF16)
        for h in range(N_HEADS):
            ro[0][:, h * HQ:h * HQ + NOPE] = (ri[0][:, h * HQ:h * HQ + NOPE] * scale).astype(BF16)
            ro[0][:, h * HQ + NOPE:(h + 1) * HQ] = (_rope(ri[0][:, h * HQ + NOPE:(h + 1) * HQ], c, a, b) * scale).astype(BF16)
            ro[1][:, h * HQ:h * HQ + NOPE] = ri[1][:, h * NOPE:(h + 1) * NOPE].astype(BF16)
            ro[1][:, h * HQ + NOPE:(h + 1) * HQ] = kt

    return _rowwise(fn, name, t_dim, ROW_TILE_WIDE,
                    [(q_pre, N_HEADS * HQ, 0), (k_nope, N_HEADS * NOPE, 0), (z_c, 128, rope_blk),
                     (cf, 128, 0), (s1, 128, 0), (s2, 128, 0)], [],
                    [(N_HEADS * HQ, BF16), (N_HEADS * HQ, BF16)])


def _rope_bwd(dq_cat, dk_cat, cf, s1, s2, scale, name):
    t_dim = dq_cat.shape[0]

    def fn(i, ri, fi, ro, ao):
        c, a, b = ri[2][...], ri[3][...], ri[4][...]
        dkt = jnp.zeros((ri[0].shape[0], 128), F32)
        for h in range(N_HEADS):
            ro[0][:, h * HQ:h * HQ + NOPE] = (ri[0][:, h * HQ:h * HQ + NOPE] * scale).astype(BF16)
            ro[0][:, h * HQ + NOPE:(h + 1) * HQ] = _rope_t(ri[0][:, h * HQ + NOPE:(h + 1) * HQ] * scale, c, a, b).astype(BF16)
            ro[1][:, h * NOPE:(h + 1) * NOPE] = ri[1][:, h * HQ:h * HQ + NOPE].astype(BF16)
            dkt = dkt + ri[1][:, h * HQ + NOPE:(h + 1) * HQ]
        ro[2][...] = _rope_t(dkt, c, a, b)

    return _rowwise(fn, name, t_dim, ROW_TILE_WIDE,
                    [(dq_cat, N_HEADS * HQ, 0), (dk_cat, N_HEADS * HQ, 0), (cf, 128, 0), (s1, 128, 0), (s2, 128, 0)],
                    [], [(N_HEADS * HQ, BF16), (N_HEADS * NOPE, BF16), (128, F32)])


def _mla_norm_bwd(z_c, dcq, dckv_k, dckv_v, dkr, q_g, kv_g, qr, kvr, name):
    t_dim, cwid = z_c.shape

    def fn(i, ri, fi, ro, ao):
        cq, rq = _rms(ri[0][:, :qr])
        ckv, rkv = _rms(ri[0][:, qr:qr + kvr])
        dq = ri[1][...]
        dkv = ri[2][...] + ri[3][...]
        ao[0][...] += _colsum(dq * cq)
        ao[1][...] += _colsum(dkv * ckv)
        ro[0][:, :qr] = _rms_bwd(cq, rq, dq * fi[0][...]).astype(BF16)
        ro[0][:, qr:qr + kvr] = _rms_bwd(ckv, rkv, dkv * fi[1][...]).astype(BF16)
        ro[0][:, qr + kvr:] = ri[4][...].astype(BF16)

    return _rowwise(fn, name, t_dim, ROW_TILE,
                    [(z_c, cwid, 0), (dcq, qr, 0), (dckv_k, kvr, 0), (dckv_v, kvr, 0), (dkr, 128, 0)], [q_g, kv_g],
                    [(cwid, BF16)], [((1, qr), F32), ((1, kvr), F32)])


def _causal_pairs(n, by_key):
    if by_key:
        pairs = [(i, j) for j in range(n) for i in range(j, n)]
    else:
        pairs = [(i, j) for i in range(n) for j in range(i + 1)]
    return (np.array([p[0] for p in pairs], np.int32), np.array([p[1] for p in pairs], np.int32))


def _attn_fwd(q, k, v, name, carry=None):
    t_dim = q.shape[0]
    tq = min(ATT_TILE, t_dim)
    nh = ATT_FWD_HEADS
    qi, kj = _causal_pairs(t_dim // tq, by_key=False)
    n_steps = len(qi)
    c_arrays, c_exchange = carry if carry else ((), False)
    nc = len(c_arrays)

    def body(qi_ref, kj_ref, q_ref, k_ref, v_ref, *rest):
        c_ins, (o_ref, lse_ref), c_outs = rest[:nc], rest[nc:nc + 2], rest[nc + 2:2 * nc + 2]
        m_sc, l_sc, acc_sc = rest[2 * nc + 2:2 * nc + 5]
        sems = rest[2 * nc + 5:]
        s_id = pl.program_id(1)
        i, j = qi_ref[s_id], kj_ref[s_id]
        if nc:
            @pl.when(jnp.logical_and(pl.program_id(0) == 0, s_id == 0))
            def _():
                _comm_start(c_ins, c_outs, *sems, c_exchange)

        @pl.when(j == 0)
        def _():
            m_sc[...] = jnp.full_like(m_sc, NEG)
            l_sc[...] = jnp.zeros_like(l_sc)
            acc_sc[...] = jnp.zeros_like(acc_sc)

        def step(masked):
            for hh in range(nh):
                s = lax.dot_general(q_ref[:, hh * HQ:(hh + 1) * HQ], k_ref[:, hh * HQ:(hh + 1) * HQ], _DIMS["nt"],
                                    preferred_element_type=F32)
                if masked:
                    row = lax.broadcasted_iota(jnp.int32, (tq, tq), 0)
                    col = lax.broadcasted_iota(jnp.int32, (tq, tq), 1)
                    s = jnp.where(col <= row, s, NEG)
                m_prev = m_sc[hh]
                m_new = jnp.maximum(m_prev, jnp.max(s, axis=1, keepdims=True))
                alpha = jnp.exp(m_prev - m_new)
                p = jnp.exp(s - m_new)
                l_sc[hh] = alpha * l_sc[hh] + jnp.sum(p, axis=1, keepdims=True)
                acc_sc[hh] = alpha * acc_sc[hh] + jnp.dot(p.astype(BF16), v_ref[:, hh * VDIM:(hh + 1) * VDIM],
                                                          preferred_element_type=F32)
                m_sc[hh] = m_new

        @pl.when(j < i)
        def _():
            step(False)

        @pl.when(j == i)
        def _():
            step(True)
            for hh in range(nh):
                o_ref[:, hh * VDIM:(hh + 1) * VDIM] = (acc_sc[hh] / l_sc[hh]).astype(BF16)
                lse_ref[hh] = m_sc[hh] + jnp.log(l_sc[hh])

        if nc:
            @pl.when(jnp.logical_and(pl.program_id(0) == N_HEADS // nh - 1, s_id == n_steps - 1))
            def _():
                _comm_wait(c_ins, c_outs, *sems, c_exchange)

    any_spec = pl.BlockSpec(memory_space=pl.ANY)
    grid_spec = pltpu.PrefetchScalarGridSpec(
        num_scalar_prefetch=2, grid=(N_HEADS // nh, n_steps),
        in_specs=[pl.BlockSpec((tq, nh * HQ), lambda h, s, qi, kj: (qi[s], h)),
                  pl.BlockSpec((tq, nh * HQ), lambda h, s, qi, kj: (kj[s], h)),
                  pl.BlockSpec((tq, nh * VDIM), lambda h, s, qi, kj: (kj[s], h))] + [any_spec] * nc,
        out_specs=[pl.BlockSpec((tq, nh * VDIM), lambda h, s, qi, kj: (qi[s], h)),
                   pl.BlockSpec((nh, tq, 1), lambda h, s, qi, kj: (h, qi[s], 0))] + [any_spec] * nc,
        scratch_shapes=[pltpu.VMEM((nh, tq, 1), F32), pltpu.VMEM((nh, tq, 1), F32), pltpu.VMEM((nh, tq, VDIM), F32)]
        + _comm_sems(nc))
    return pl.pallas_call(
        body, grid_spec=grid_spec,
        out_shape=[jax.ShapeDtypeStruct((t_dim, N_HEADS * VDIM), BF16), jax.ShapeDtypeStruct((N_HEADS, t_dim, 1), F32)]
        + _comm_out_shape(c_arrays, c_exchange),
        compiler_params=_cparams(("arbitrary", "arbitrary")), name=name)(
            jnp.asarray(qi), jnp.asarray(kj), q, k, v, *c_arrays)


def _attn_delta(do, o, name):
    t_dim = do.shape[0]
    tr = min(ATT_TILE, t_dim)

    def body(do_ref, o_ref, d_ref):
        d_ref[0] = jnp.sum(do_ref[...].astype(F32) * o_ref[...].astype(F32), axis=1, keepdims=True)

    return pl.pallas_call(
        body, out_shape=jax.ShapeDtypeStruct((N_HEADS, t_dim, 1), F32), grid=(N_HEADS, t_dim // tr),
        in_specs=[pl.BlockSpec((tr, VDIM), lambda h, i: (i, h)), pl.BlockSpec((tr, VDIM), lambda h, i: (i, h))],
        out_specs=pl.BlockSpec((1, tr, 1), lambda h, i: (h, i, 0)),
        compiler_params=_cparams(("parallel", "parallel")), name=name)(do, o)


def _attn_bwd(q, k, v, do, lse_row, delta_row, name, carry=None):
    t_dim = q.shape[0]
    tq = min(ATT_TILE, t_dim)
    nq = t_dim // tq
    qi, kj = _causal_pairs(nq, by_key=True)
    n_steps = len(qi)
    c_arrays, c_exchange = carry if carry else ((), False)
    nc = len(c_arrays)

    def body(qi_ref, kj_ref, q_ref, k_ref, v_ref, do_ref, lse_ref, dl_ref, *rest):
        c_ins, (dq_ref, dk_ref, dv_ref), c_outs = rest[:nc], rest[nc:nc + 3], rest[nc + 3:2 * nc + 3]
        dk_sc, dv_sc = rest[2 * nc + 3:2 * nc + 5]
        sems = rest[2 * nc + 5:]
        s_id = pl.program_id(1)
        i, j = qi_ref[s_id], kj_ref[s_id]
        if nc:
            @pl.when(jnp.logical_and(pl.program_id(0) == 0, s_id == 0))
            def _():
                _comm_start(c_ins, c_outs, *sems, c_exchange)

        @pl.when(s_id == 0)
        def _():
            dq_ref[...] = jnp.zeros_like(dq_ref)

        rows = pl.ds(pl.multiple_of(i * tq, tq), tq)

        def step(masked):
            s_t = lax.dot_general(k_ref[...], q_ref[...], _DIMS["nt"], preferred_element_type=F32)
            p_t = jnp.exp(s_t - lse_ref[0])
            if masked:
                row = lax.broadcasted_iota(jnp.int32, (tq, tq), 0)
                col = lax.broadcasted_iota(jnp.int32, (tq, tq), 1)
                p_t = jnp.where(row <= col, p_t, 0.0)
            do = do_ref[...]
            dv = jnp.dot(p_t.astype(BF16), do, preferred_element_type=F32)
            dp_t = lax.dot_general(v_ref[...], do, _DIMS["nt"], preferred_element_type=F32)
            ds_t = (p_t * (dp_t - dl_ref[0])).astype(BF16)
            dk = jnp.dot(ds_t, q_ref[...], preferred_element_type=F32)
            if masked:
                dv_sc[...] = dv
                dk_sc[...] = dk
            else:
                dv_sc[...] += dv
                dk_sc[...] += dk
            dq_ref[rows, :] += lax.dot_general(ds_t, k_ref[...], _DIMS["tn"], preferred_element_type=F32)

        @pl.when(i == j)
        def _():
            step(True)

        @pl.when(i != j)
        def _():
            step(False)

        @pl.when(i == nq - 1)
        def _():
            dk_ref[...] = dk_sc[...]
            dv_ref[...] = dv_sc[...].astype(BF16)

        if nc:
            @pl.when(jnp.logical_and(pl.program_id(0) == N_HEADS - 1, s_id == n_steps - 1))
            def _():
                _comm_wait(c_ins, c_outs, *sems, c_exchange)

    any_spec = pl.BlockSpec(memory_space=pl.ANY)
    grid_spec = pltpu.PrefetchScalarGridSpec(
        num_scalar_prefetch=2, grid=(N_HEADS, n_steps),
        in_specs=[pl.BlockSpec((tq, HQ), lambda h, s, qi, kj: (qi[s], h)),
                  pl.BlockSpec((tq, HQ), lambda h, s, qi, kj: (kj[s], h)),
                  pl.BlockSpec((tq, VDIM), lambda h, s, qi, kj: (kj[s], h)),
                  pl.BlockSpec((tq, VDIM), lambda h, s, qi, kj: (qi[s], h)),
                  pl.BlockSpec((1, 1, tq), lambda h, s, qi, kj: (h, 0, qi[s])),
                  pl.BlockSpec((1, 1, tq), lambda h, s, qi, kj: (h, 0, qi[s]))] + [any_spec] * nc,
        out_specs=[pl.BlockSpec((t_dim, HQ), lambda h, s, qi, kj: (0, h)),
                   pl.BlockSpec((tq, HQ), lambda h, s, qi, kj: (kj[s], h)),
                   pl.BlockSpec((tq, VDIM), lambda h, s, qi, kj: (kj[s], h))] + [any_spec] * nc,
        scratch_shapes=[pltpu.VMEM((tq, HQ), F32), pltpu.VMEM((tq, VDIM), F32)] + _comm_sems(nc))
    return pl.pallas_call(
        body, grid_spec=grid_spec,
        out_shape=[jax.ShapeDtypeStruct((t_dim, N_HEADS * HQ), F32), jax.ShapeDtypeStruct((t_dim, N_HEADS * HQ), F32),
                   jax.ShapeDtypeStruct((t_dim, N_HEADS * VDIM), BF16)] + _comm_out_shape(c_arrays, c_exchange),
        compiler_params=_cparams(("arbitrary", "arbitrary")), name=name)(
            jnp.asarray(qi), jnp.asarray(kj), q, k, v, do, lse_row, delta_row, *c_arrays)


def _merge_fwd(z_g, y_a, y_b, y_c, name):
    t_dim, d = y_a.shape

    def fn(i, ri, fi, ro, ao):
        acc = _sigmoid(ri[0][:, :d]) * ri[1][...]
        acc = acc + _sigmoid(ri[0][:, d:2 * d]) * ri[2][...]
        acc = acc + _sigmoid(ri[0][:, 2 * d:]) * ri[3][...]
        ro[0][...] = acc.astype(BF16)

    return _rowwise(fn, name, t_dim, ROW_TILE_WIDE, [(z_g, 3 * d, 0), (y_a, d, 0), (y_b, d, 0), (y_c, d, 0)], [],
                    [(d, BF16)])[0]


def _merge_bwd(z_g, y_a, y_b, y_c, dmerged, name):
    t_dim, d = y_a.shape

    def fn(i, ri, fi, ro, ao):
        dm = ri[4][...]
        for q in range(3):
            s = _sigmoid(ri[0][:, q * d:(q + 1) * d])
            ro[q][...] = (s * dm).astype(BF16)
            ro[3][:, q * d:(q + 1) * d] = (dm * ri[1 + q][...] * s * (1.0 - s)).astype(BF16)

    return _rowwise(fn, name, t_dim, ROW_TILE_WIDE,
                    [(z_g, 3 * d, 0), (y_a, d, 0), (y_b, d, 0), (y_c, d, 0), (dmerged, d, 0)], [],
                    [(d, BF16), (d, BF16), (d, BF16), (3 * d, BF16)])


def _adam(w, parts, m, v, name):
    r_dim, c_dim = w.shape
    limit = max(16, ADAM_BLOCK_ELEMS // c_dim // 16 * 16)
    tr = r_dim
    if r_dim > limit:
        tr = next((t for t in range(limit, 15, -16) if r_dim % t == 0), r_dim)

    def body(w_ref, p_ref, m_ref, v_ref, g_out, d_out, m_out, v_out):
        g = p_ref[0].astype(F32)
        for s in range(1, N_DEV):
            g = g + p_ref[s].astype(F32)
        m_new = ADAM_B1 * m_ref[...] + (1.0 - ADAM_B1) * g
        v_new = ADAM_B2 * v_ref[...] + (1.0 - ADAM_B2) * (g * g)
        m_hat = m_new / (1.0 - ADAM_B1 ** ADAM_STEP)
        v_hat = v_new / (1.0 - ADAM_B2 ** ADAM_STEP)
        g_out[...] = g
        d_out[...] = -ADAM_LR * (m_hat / (jnp.sqrt(v_hat) + ADAM_EPS) + ADAM_WD * w_ref[...])
        m_out[...] = m_new
        v_out[...] = v_new

    blk = pl.BlockSpec((tr, c_dim), lambda i: (i, 0))
    return pl.pallas_call(
        body, out_shape=[jax.ShapeDtypeStruct((r_dim, c_dim), F32)] * 4, grid=(r_dim // tr,),
        in_specs=[blk, pl.BlockSpec((N_DEV, tr, c_dim), lambda i: (0, i, 0)), blk, blk], out_specs=[blk] * 4,
        compiler_params=_cparams(("parallel",)), name=name)(w, parts, m, v)


def _me_and_peers():
    x, y, c = lax.axis_index("x"), lax.axis_index("y"), lax.axis_index("c")
    me = 4 * x + 2 * y + c
    peers = []
    for k in range(1, N_DEV):
        px, py, pc = x ^ (k >> 2), y ^ ((k >> 1) & 1), c ^ (k & 1)
        peers.append(((px, py, pc), 4 * px + 2 * py + pc))
    return me, peers


def _comm_copies(ins, outs, send_sems, recv_sems, local_sems, exchange):
    me, peers = _me_and_peers()

    def src(w, dest_idx):
        return ins[w].at[dest_idx] if exchange else ins[w]

    def remote(w, k, dev, src_ref, dst_ref):
        return pltpu.make_async_remote_copy(
            src_ref=src_ref, dst_ref=dst_ref, send_sem=send_sems.at[w * (N_DEV - 1) + k],
            recv_sem=recv_sems.at[w * (N_DEV - 1) + k], device_id=dev, device_id_type=pl.DeviceIdType.MESH)

    local = [pltpu.make_async_copy(src(w, me), outs[w].at[me], local_sems.at[w]) for w in range(len(ins))]
    sends, arrivals = [], []
    for w in range(len(ins)):
        for k, (dev, idx) in enumerate(peers):
            sends.append(remote(w, k, dev, src(w, idx), outs[w].at[me]))
            arrivals.append(remote(w, k, dev, src(w, idx), outs[w].at[idx]))
    return local, sends, arrivals


def _comm_start(ins, outs, send_sems, recv_sems, local_sems, exchange):
    local, sends, _ = _comm_copies(ins, outs, send_sems, recv_sems, local_sems, exchange)
    for cp in local + sends:
        cp.start()


def _comm_wait(ins, outs, send_sems, recv_sems, local_sems, exchange):
    local, sends, arrivals = _comm_copies(ins, outs, send_sems, recv_sems, local_sems, exchange)
    for cp in arrivals:
        cp.wait_recv()
    for cp in sends:
        cp.wait_send()
    for cp in local:
        cp.wait()


def _comm_sems(n):
    if not n:
        return []
    return [pltpu.SemaphoreType.DMA((n * (N_DEV - 1),)), pltpu.SemaphoreType.DMA((n * (N_DEV - 1),)),
            pltpu.SemaphoreType.DMA((n,))]


def _comm_out_shape(arrays, exchange):
    return [jax.ShapeDtypeStruct(a.shape if exchange else (N_DEV,) + a.shape, a.dtype) for a in arrays]


def _comm(arrays, exchange, name):
    n = len(arrays)
    hbm = pl.BlockSpec(memory_space=pltpu.HBM)

    def body(*refs):
        ins, outs, sems = refs[:n], refs[n:2 * n], refs[2 * n:]
        _comm_start(ins, outs, *sems, exchange)
        _comm_wait(ins, outs, *sems, exchange)

    return pl.pallas_call(
        body, out_shape=_comm_out_shape(arrays, exchange), in_specs=[hbm] * n, out_specs=[hbm] * n,
        scratch_shapes=_comm_sems(n), name=name)(*arrays)


def _unshard(name, g):
    if name in COL_SHARDED:
        return jnp.transpose(g, (1, 0, 2)).reshape(g.shape[1], N_DEV * g.shape[2])
    return g.reshape(N_DEV * g.shape[1], g.shape[2])


def _to_shards(name, full):
    if name in COL_SHARDED:
        r, c = full.shape
        return jnp.transpose(full.reshape(r, N_DEV, c // N_DEV), (1, 0, 2))
    return full.reshape(N_DEV, full.shape[0] // N_DEV, full.shape[1])


def _ffn_fwd(x, p, tag):
    h = _norm_fwd(x, p["norm_pre"], f"{tag}_norm")
    gu = _mm(h, p["w_in"], "nn", F32, f"{tag}_in")
    act = _swiglu_fwd(gu, f"{tag}_act")
    y = _mm(act, p["w_out"], "nn", F32, f"{tag}_out")
    x_new = _resid_fwd(x, y, p["norm_post"], 0.5, f"{tag}_resid")
    return x_new, dict(x=x, h=h, gu=gu, act=act, y=y)


def _ffn_bwd(dxo, s, p, tag):
    dy, dg_post = _resid_bwd(s["y"], dxo, p["norm_post"], 0.5, f"{tag}_resid_bwd")
    dact = _mm(dy, p["w_out"], "nt", F32, f"{tag}_out_dx")
    dw_out = _mm(s["act"], dy, "tn", F32, f"{tag}_out_dw")
    dgu = _swiglu_bwd(s["gu"], dact, f"{tag}_act_bwd")
    dh = _mm(dgu, p["w_in"], "nt", F32, f"{tag}_in_dx")
    dw_in = _mm(s["h"], dgu, "tn", F32, f"{tag}_in_dw")
    dx, dg_pre = _norm_bwd(s["x"], [dh], dxo, p["norm_pre"], f"{tag}_norm_bwd")
    return dx, dict(norm_pre=dg_pre, norm_post=dg_post, w_in=dw_in, w_out=dw_out)


def _mixer_fwd(x, p, rope_tabs, dims, carry):
    gmw, cw, qr, kvr = dims["gmw"], dims["cw"], dims["qr"], dims["kvr"]
    cf, s1, s2 = rope_tabs
    scale = (NOPE + ROPE) ** -0.5
    h = _norm_fwd(x, p["norm_pre"], "mix_norm")
    z_a = _mm(h, p["w_a"], "nn", F32, "mix_in_a")
    z_c = _mm(h, p["w_c"], "nn", F32, "mix_in_c")
    z_g = _mm(h, p["w_g"], "nn", F32, "mix_in_g")
    o_a = _gm_fwd(z_a, p["gm_ln_g"], p["gm_ln_b"], p["gm_w_s"], p["gm_b_s"], gmw, "gm_fwd")
    a = _glu_fwd(z_a, gmw, cw, "glu_fwd")
    y_conv, o_b = _conv_fwd(a, p["conv_w"], p["conv_b"], p["conv_ln_g"], p["conv_ln_b"], "conv_fwd")
    cqn, ckvn = _mla_norm_fwd(z_c, p["q_norm"], p["kv_norm"], qr, kvr, "mla_norm")
    q_pre = _mm(cqn, p["w_uq"], "nn", F32, "mla_uq")
    k_nope = _mm(ckvn, p["w_uk"], "nn", F32, "mla_uk")
    v = _mm(ckvn, p["w_uv"], "nn", BF16, "mla_uv")
    q_cat, k_cat = _rope_fwd(q_pre, k_nope, z_c, cf, s1, s2, (qr + kvr) // 128, scale, "rope_fwd")
    o_c, lse, *carried = _attn_fwd(q_cat, k_cat, v, "attn_fwd_gather" if carry else "attn_fwd", carry)
    y_a = _mm(o_a, p["wb_a"], "nn", F32, "branch_a")
    y_b = _mm(o_b, p["wb_b"], "nn", F32, "branch_b")
    y_c = _mm(o_c, p["wb_c"], "nn", F32, "branch_c")
    merged = _merge_fwd(z_g, y_a, y_b, y_c, "merge_fwd")
    m = _mm(merged, p["w_out"], "nn", F32, "mix_out")
    x_new = _resid_fwd(x, m, p["norm_post"], 1.0, "mix_resid")
    saved = dict(x=x, h=h, z_a=z_a, z_c=z_c, z_g=z_g, o_a=o_a, a=a, y_conv=y_conv, o_b=o_b, cqn=cqn, ckvn=ckvn,
                 v=v, q_cat=q_cat, k_cat=k_cat, o_c=o_c, lse=lse, y_a=y_a, y_b=y_b, y_c=y_c, merged=merged, m=m)
    return x_new, saved, carried


def _mixer_bwd(dxo, s, p, rope_tabs, dims, carry):
    gmw, cw, qr, kvr = dims["gmw"], dims["cw"], dims["qr"], dims["kvr"]
    cf, s1, s2 = rope_tabs
    scale = (NOPE + ROPE) ** -0.5
    t_dim = dxo.shape[0]
    g = {}
    dm, g["norm_post"] = _resid_bwd(s["m"], dxo, p["norm_post"], 1.0, "mix_resid_bwd")
    dmerged = _mm(dm, p["w_out"], "nt", F32, "mix_out_dx")
    g["w_out"] = _mm(s["merged"], dm, "tn", F32, "mix_out_dw")
    dy_a, dy_b, dy_c, dz_g = _merge_bwd(s["z_g"], s["y_a"], s["y_b"], s["y_c"], dmerged, "merge_bwd")
    do_a = _mm(dy_a, p["wb_a"], "nt", F32, "branch_a_dx")
    do_b = _mm(dy_b, p["wb_b"], "nt", F32, "branch_b_dx")
    do_c = _mm(dy_c, p["wb_c"], "nt", BF16, "branch_c_dx")
    g["w_branch"] = jnp.concatenate([_mm(s["o_a"], dy_a, "tn", F32, "branch_a_dw"),
                                     _mm(s["o_b"], dy_b, "tn", F32, "branch_b_dw"),
                                     _mm(s["o_c"], dy_c, "tn", F32, "branch_c_dw")], axis=0)
    delta = _attn_delta(do_c, s["o_c"], "attn_delta")
    dq_cat, dk_cat, dv, *carried = _attn_bwd(s["q_cat"], s["k_cat"], s["v"], do_c, s["lse"].reshape(N_HEADS, 1, t_dim),
                                             delta.reshape(N_HEADS, 1, t_dim),
                                             "attn_bwd_exchange" if carry else "attn_bwd", carry)
    dq_pre, dk_nope, dkr = _rope_bwd(dq_cat, dk_cat, cf, s1, s2, scale, "rope_bwd")
    dcq = _mm(dq_pre, p["w_uq"], "nt", F32, "mla_uq_dx")
    g["w_uq"] = _mm(s["cqn"], dq_pre, "tn", F32, "mla_uq_dw")
    dckv_k = _mm(dk_nope, p["w_uk"], "nt", F32, "mla_uk_dx")
    dckv_v = _mm(dv, p["w_uv"], "nt", F32, "mla_uv_dx")
    g["w_uk"] = _mm(s["ckvn"], dk_nope, "tn", F32, "mla_uk_dw")
    g["w_uv"] = _mm(s["ckvn"], dv, "tn", F32, "mla_uv_dw")
    dz_c, g["q_norm"], g["kv_norm"] = _mla_norm_bwd(s["z_c"], dcq, dckv_k, dckv_v, dkr, p["q_norm"], p["kv_norm"],
                                                    qr, kvr, "mla_norm_bwd")
    dy_conv, g["conv_ln_g"], g["conv_ln_b"] = _conv_ln_bwd(s["y_conv"], do_b, p["conv_ln_g"], p["conv_ln_b"], "conv_ln_bwd")
    dval, dgate, g["conv_w"], g["conv_b"] = _conv_bwd(dy_conv, s["a"], s["z_a"], p["conv_w"], gmw, "conv_bwd")
    da_u, da_v, g["gm_ln_g"], g["gm_ln_b"], g["gm_w_s"], g["gm_b_s"] = _gm_bwd(
        s["z_a"], do_a, p["gm_ln_g"], p["gm_ln_b"], p["gm_w_s"], p["gm_w_s_t"], p["gm_b_s"], gmw, "gm_bwd")
    dz_a = jnp.concatenate([da_u, da_v, dval, dgate], axis=1)
    dh_a = _mm(dz_a, p["w_a"], "nt", F32, "mix_in_a_dx")
    dh_c = _mm(dz_c, p["w_c"], "nt", F32, "mix_in_c_dx")
    dh_g = _mm(dz_g, p["w_g"], "nt", F32, "mix_in_g_dx")
    g["w_a"] = _mm(s["h"], dz_a, "tn", F32, "mix_in_a_dw")
    g["w_c"] = _mm(s["h"], dz_c, "tn", F32, "mix_in_c_dw")
    g["w_g"] = _mm(s["h"], dz_g, "tn", F32, "mix_in_g_dw")
    dx, g["norm_pre"] = _norm_bwd(s["x"], [dh_a, dh_c, dh_g], dxo, p["norm_pre"], "mix_norm_bwd")
    return dx, g, carried


def _layer_params(full, small, conv_w_full, dims, l):
    gmw, cw, qr, kvr, d = dims["gmw"], dims["cw"], dims["qr"], dims["kvr"], dims["d"]
    row = lambda a: a[l][None, :]
    ffn = lambda k: dict(norm_pre=row(small[f"{k}_norm_pre"]), norm_post=row(small[f"{k}_norm_post"]),
                         w_in=full[f"{k}_w_in"], w_out=full[f"{k}_w_out"])
    w_in = full["mix_w_in"]
    a_end = 2 * gmw + 2 * cw
    c_end = a_end + qr + kvr + ROPE
    w_c = jnp.concatenate([w_in[:, a_end:c_end], jnp.zeros((d, 128 - ROPE), w_in.dtype)], axis=1)
    w_uq = full["mla_w_uq"].reshape(qr, N_HEADS, NOPE + ROPE)
    w_uq = jnp.concatenate([w_uq, jnp.zeros((qr, N_HEADS, HQ - NOPE - ROPE), w_uq.dtype)], axis=2).reshape(qr, N_HEADS * HQ)
    w_ukv = full["mla_w_ukv"].reshape(kvr, N_HEADS, NOPE + VDIM)
    w_b = full["mix_w_branch"]
    w_s = small["gm_w_s"][l]
    mix = dict(norm_pre=row(small["mix_norm_pre"]), norm_post=row(small["mix_norm_post"]),
               w_a=w_in[:, :a_end], w_c=w_c, w_g=w_in[:, c_end:],
               gm_ln_g=row(small["gm_ln_g"]), gm_ln_b=row(small["gm_ln_b"]), gm_w_s=w_s,
               gm_w_s_t=jnp.transpose(w_s, (0, 2, 1)), gm_b_s=small["gm_b_s"][l][:, :, None],
               conv_w=jnp.concatenate([conv_w_full, jnp.zeros((CONV_HALO - CONV_K, cw), F32)], axis=0),
               conv_b=row(small["conv_b"]), conv_ln_g=row(small["conv_ln_g"]), conv_ln_b=row(small["conv_ln_b"]),
               q_norm=row(small["mla_q_norm"]), kv_norm=row(small["mla_kv_norm"]),
               w_uq=w_uq, w_uk=w_ukv[:, :, :NOPE].reshape(kvr, N_HEADS * NOPE),
               w_uv=w_ukv[:, :, NOPE:].reshape(kvr, N_HEADS * VDIM),
               wb_a=w_b[:gmw], wb_b=w_b[gmw:gmw + cw], wb_c=w_b[gmw + cw:], w_out=full["mix_w_out"])
    return dict(ffn1=ffn("ffn1"), mix=mix, ffn2=ffn("ffn2"))


def _layer_grads(g1, gm, g2, dims):
    qr, kvr = dims["qr"], dims["kvr"]
    big = {
        "ffn1_w_in": g1["w_in"], "ffn1_w_out": g1["w_out"], "ffn2_w_in": g2["w_in"], "ffn2_w_out": g2["w_out"],
        "mix_w_in": jnp.concatenate([gm["w_a"], gm["w_c"][:, :qr + kvr + ROPE], gm["w_g"]], axis=1),
        "mla_w_uq": gm["w_uq"].reshape(qr, N_HEADS, HQ)[:, :, :NOPE + ROPE].reshape(qr, N_HEADS * (NOPE + ROPE)),
        "mla_w_ukv": jnp.concatenate([gm["w_uk"].reshape(kvr, N_HEADS, NOPE), gm["w_uv"].reshape(kvr, N_HEADS, VDIM)],
                                     axis=2).reshape(kvr, N_HEADS * (NOPE + VDIM)),
        "mix_w_branch": gm["w_branch"], "mix_w_out": gm["w_out"], "conv_w": gm["conv_w"][:CONV_K],
    }
    small = {
        "ffn1_norm_pre": g1["norm_pre"][0], "ffn1_norm_post": g1["norm_post"][0],
        "ffn2_norm_pre": g2["norm_pre"][0], "ffn2_norm_post": g2["norm_post"][0],
        "mix_norm_pre": gm["norm_pre"][0], "mix_norm_post": gm["norm_post"][0],
        "gm_ln_g": gm["gm_ln_g"][0], "gm_ln_b": gm["gm_ln_b"][0], "gm_w_s": gm["gm_w_s"], "gm_b_s": gm["gm_b_s"][:, :, 0],
        "conv_b": gm["conv_b"][0], "conv_ln_g": gm["conv_ln_g"][0], "conv_ln_b": gm["conv_ln_b"][0],
        "mla_q_norm": gm["q_norm"][0], "mla_kv_norm": gm["kv_norm"][0],
    }
    return big, small


def _rope_tables(positions):
    inv_freq = ROPE_THETA ** (-jnp.arange(0, ROPE, 2, dtype=F32) / ROPE)
    ang = positions.astype(F32)[:, None] * inv_freq
    cos, sin = jnp.cos(ang), jnp.sin(ang)
    z = lambda w: jnp.zeros((positions.shape[0], w), F32)
    return (jnp.concatenate([cos, cos, z(64)], axis=1), jnp.concatenate([-sin, z(96)], axis=1),
            jnp.concatenate([z(32), sin, z(64)], axis=1))


def _pad_rows(flat, mult):
    n = flat.shape[0]
    pad = (-n) % mult
    return jnp.concatenate([flat, jnp.zeros((pad,), flat.dtype)]) if pad else flat


def kernel(x, positions, ffn1_norm_pre, ffn1_norm_post, ffn1_w_in, ffn1_w_out, mix_norm_pre, mix_norm_post, mix_w_in, gm_ln_g, gm_ln_b, gm_w_s, gm_b_s, conv_w, conv_b, conv_ln_g, conv_ln_b, mla_q_norm, mla_w_uq, mla_kv_norm, mla_w_ukv, mix_w_branch, mix_w_out, ffn2_norm_pre, ffn2_norm_post, ffn2_w_in, ffn2_w_out, loss_target, m_ffn1_norm_pre, m_ffn1_norm_post, m_ffn1_w_in, m_ffn1_w_out, m_mix_norm_pre, m_mix_norm_post, m_mix_w_in, m_gm_ln_g, m_gm_ln_b, m_gm_w_s, m_gm_b_s, m_conv_w, m_conv_b, m_conv_ln_g, m_conv_ln_b, m_mla_q_norm, m_mla_w_uq, m_mla_kv_norm, m_mla_w_ukv, m_mix_w_branch, m_mix_w_out, m_ffn2_norm_pre, m_ffn2_norm_post, m_ffn2_w_in, m_ffn2_w_out, v_ffn1_norm_pre, v_ffn1_norm_post, v_ffn1_w_in, v_ffn1_w_out, v_mix_norm_pre, v_mix_norm_post, v_mix_w_in, v_gm_ln_g, v_gm_ln_b, v_gm_w_s, v_gm_b_s, v_conv_w, v_conv_b, v_conv_ln_g, v_conv_ln_b, v_mla_q_norm, v_mla_w_uq, v_mla_kv_norm, v_mla_w_ukv, v_mix_w_branch, v_mix_w_out, v_ffn2_norm_pre, v_ffn2_norm_post, v_ffn2_w_in, v_ffn2_w_out):
    w = dict(zip(WEIGHTS, (ffn1_norm_pre, ffn1_norm_post, ffn1_w_in, ffn1_w_out, mix_norm_pre, mix_norm_post, mix_w_in, gm_ln_g, gm_ln_b, gm_w_s, gm_b_s, conv_w, conv_b, conv_ln_g, conv_ln_b, mla_q_norm, mla_w_uq, mla_kv_norm, mla_w_ukv, mix_w_branch, mix_w_out, ffn2_norm_pre, ffn2_norm_post, ffn2_w_in, ffn2_w_out)))
    mom_m = dict(zip(WEIGHTS, (m_ffn1_norm_pre, m_ffn1_norm_post, m_ffn1_w_in, m_ffn1_w_out, m_mix_norm_pre, m_mix_norm_post, m_mix_w_in, m_gm_ln_g, m_gm_ln_b, m_gm_w_s, m_gm_b_s, m_conv_w, m_conv_b, m_conv_ln_g, m_conv_ln_b, m_mla_q_norm, m_mla_w_uq, m_mla_kv_norm, m_mla_w_ukv, m_mix_w_branch, m_mix_w_out, m_ffn2_norm_pre, m_ffn2_norm_post, m_ffn2_w_in, m_ffn2_w_out)))
    mom_v = dict(zip(WEIGHTS, (v_ffn1_norm_pre, v_ffn1_norm_post, v_ffn1_w_in, v_ffn1_w_out, v_mix_norm_pre, v_mix_norm_post, v_mix_w_in, v_gm_ln_g, v_gm_ln_b, v_gm_w_s, v_gm_b_s, v_conv_w, v_conv_b, v_conv_ln_g, v_conv_ln_b, v_mla_q_norm, v_mla_w_uq, v_mla_kv_norm, v_mla_w_ukv, v_mix_w_branch, v_mix_w_out, v_ffn2_norm_pre, v_ffn2_norm_post, v_ffn2_w_in, v_ffn2_w_out)))
    n_layers = ffn1_norm_pre.shape[0]
    t_dim, d = x.shape[1], x.shape[2]
    dims = dict(d=d, gmw=gm_ln_g.shape[1], cw=conv_ln_g.shape[1], qr=mla_q_norm.shape[1], kvr=mla_kv_norm.shape[1])
    x0 = x.reshape(t_dim, d)
    target = loss_target.reshape(t_dim, d)
    rope_tabs = _rope_tables(positions.reshape(t_dim))

    conv_all = _unshard_conv(_comm([conv_w], False, "gather_conv")[0])
    shards_of = lambda l: [w[k][l].astype(BF16) for k in BIG]
    gathered = _comm(shards_of(0), False, "gather_layer")
    params, saved = [], []
    xc = x0
    for l in range(n_layers):
        full = {k: _unshard(k, gathered[q]) for q, k in enumerate(BIG)}
        params.append(_layer_params(full, w, conv_all[l], dims, l))
        xc, s1 = _ffn_fwd(xc, params[l]["ffn1"], "ffn1")
        xc, sm, gathered = _mixer_fwd(xc, params[l]["mix"], rope_tabs, dims,
                                      (shards_of(l + 1), False) if l + 1 < n_layers else None)
        xc, s2 = _ffn_fwd(xc, params[l]["ffn2"], "ffn2")
        saved.append((s1, sm, s2))
    dx, loss_part = _loss_fwd_bwd(xc, target, "loss")
    loss = lax.psum(loss_part[0, 0], ("x", "y", "c"))

    names = BIG + ("conv_w",)
    big_out = {k: [None] * n_layers for k in names}
    small_parts = [None] * n_layers

    def update(l, recv):
        for q, k in enumerate(names):
            wl = w[k][l]
            r2 = (lambda a: a.reshape(-1, a.shape[-1]))
            outs = _adam(r2(wl), recv[q].reshape(N_DEV, -1, wl.shape[-1]), r2(mom_m[k][l]), r2(mom_v[k][l]), f"adam_{k}")
            big_out[k][l] = [o.reshape(wl.shape) for o in outs]

    outgoing = None
    for l in reversed(range(n_layers)):
        s1, sm, s2 = saved[l]
        dx, g2 = _ffn_bwd(dx, s2, params[l]["ffn2"], "ffn2")
        dx, gm, recv = _mixer_bwd(dx, sm, params[l]["mix"], rope_tabs, dims, (outgoing, True) if outgoing else None)
        if outgoing:
            update(l + 1, recv)
        dx, g1 = _ffn_bwd(dx, s1, params[l]["ffn1"], "ffn1")
        big, small_parts[l] = _layer_grads(g1, gm, g2, dims)
        outgoing = [_to_shards(k, big[k]).astype(BF16) for k in names]
    update(0, _comm(outgoing, True, "exchange_layer"))
    grad_x = dx.reshape(x.shape)

    flat = lambda tree: _pad_rows(jnp.concatenate([tree[k].reshape(-1) for k in SMALL]), 256 * 128).reshape(-1, 128)
    g_small = flat({k: jnp.stack([small_parts[l][k] for l in range(n_layers)]) for k in SMALL})
    parts = _comm([g_small], False, "gather_small_grads")[0]
    s_outs = _adam(flat(w), parts, flat(mom_m), flat(mom_v), "adam_small")
    small_out = {k: [] for k in SMALL}
    for o in s_outs:
        o = o.reshape(-1)
        off = 0
        for k in SMALL:
            n = int(np.prod(w[k].shape))
            small_out[k].append(o[off:off + n].reshape(w[k].shape))
            off += n

    def out(which, k):
        if k in SMALL:
            return small_out[k][which]
        return jnp.stack([big_out[k][l][which] for l in range(n_layers)])

    return (loss, grad_x, *[out(0, k) for k in WEIGHTS], *[out(1, k) for k in WEIGHTS],
            *[out(2, k) for k in WEIGHTS], *[out(3, k) for k in WEIGHTS])


def _unshard_conv(g):
    n_dev, n_layers, k, c = g.shape
    return jnp.transpose(g, (1, 2, 0, 3)).reshape(n_layers, k, n_dev * c)
```

```python
import functools

import numpy as np
import jax
import jax.numpy as jnp
from jax import lax
from jax.experimental import pallas as pl
from jax.experimental.pallas import tpu as pltpu

F32 = jnp.float32
BF16 = jnp.bfloat16

N_DEV = 8
N_HEADS = 16
NOPE = 128
ROPE = 64
VDIM = 128
HQ = 256
GROUPS = 4
CHUNK = 128
CONV_K = 31
CONV_HALO = 32
EPS = 1e-6
ROPE_THETA = 10000.0
ADAM_LR = 0.001
ADAM_B1 = 0.9
ADAM_B2 = 0.999
ADAM_EPS = 1e-08
ADAM_WD = 0.01
ADAM_STEP = 10
NEG = -1e30

V7X_VMEM_LIMIT = 56 * 1024 * 1024
MM_TM, MM_TN, MM_TK = 1024, 1024, 2048
ROW_TILE = 256
ROW_TILE_WIDE = 128
ATT_TILE = 1024
ATT_FWD_HEADS = 2
CONV_TILE = 256
GM_TILE = 256
ADAM_BLOCK_ELEMS = 128 * 1024

BIG = ("ffn1_w_in", "ffn1_w_out", "mix_w_in", "mla_w_uq", "mla_w_ukv", "mix_w_branch", "mix_w_out",
       "ffn2_w_in", "ffn2_w_out")
FFN1_W = ("ffn1_w_in", "ffn1_w_out")
MIX_W = ("mix_w_in", "mla_w_uq", "mla_w_ukv", "mix_w_branch", "mix_w_out")
FFN2_W = ("ffn2_w_in", "ffn2_w_out")
MIX_G = MIX_W + ("conv_w",)
COL_SHARDED = ("ffn1_w_in", "mix_w_in", "mla_w_uq", "mla_w_ukv", "ffn2_w_in", "conv_w")
SMALL = ("ffn1_norm_pre", "ffn1_norm_post", "mix_norm_pre", "mix_norm_post", "gm_ln_g", "gm_ln_b", "gm_w_s",
         "gm_b_s", "conv_b", "conv_ln_g", "conv_ln_b", "mla_q_norm", "mla_kv_norm", "ffn2_norm_pre",
         "ffn2_norm_post")
WEIGHTS = ("ffn1_norm_pre", "ffn1_norm_post", "ffn1_w_in", "ffn1_w_out", "mix_norm_pre", "mix_norm_post",
           "mix_w_in", "gm_ln_g", "gm_ln_b", "gm_w_s", "gm_b_s", "conv_w", "conv_b", "conv_ln_g", "conv_ln_b",
           "mla_q_norm", "mla_w_uq", "mla_kv_norm", "mla_w_ukv", "mix_w_branch", "mix_w_out", "ffn2_norm_pre",
           "ffn2_norm_post", "ffn2_w_in", "ffn2_w_out")


def _cparams(sem):
    return pltpu.CompilerParams(dimension_semantics=sem, vmem_limit_bytes=V7X_VMEM_LIMIT)


def _pick(dim, pref):
    if dim <= pref:
        return dim
    t = pref
    while t >= 128:
        if dim % t == 0:
            return t
        t -= 128
    return dim


def _sigmoid(x):
    return 1.0 / (1.0 + jnp.exp(-x))


_GELU_C = 0.7978845608028654


def _gelu(x):
    t = jnp.tanh(_GELU_C * (x + 0.044715 * x * x * x))
    return 0.5 * x * (1.0 + t)


def _gelu_grad(x):
    t = jnp.tanh(_GELU_C * (x + 0.044715 * x * x * x))
    return 0.5 * (1.0 + t) + 0.5 * x * (1.0 - t * t) * _GELU_C * (1.0 + 3.0 * 0.044715 * x * x)


def _rms(x):
    r = lax.rsqrt(jnp.mean(x * x, axis=-1, keepdims=True) + EPS)
    return x * r, r


def _rms_bwd(xn, r, t):
    return r * (t - xn * jnp.mean(t * xn, axis=-1, keepdims=True))


def _colsum(x):
    return jnp.sum(x, axis=0, keepdims=True)


_DIMS = {"nn": (((1,), (0,)), ((), ())), "nt": (((1,), (1,)), ((), ())), "tn": (((0,), (0,)), ((), ()))}


def _mm(a, b, mode, out_dtype, name, carry=None):
    if mode == "tn":
        k_dim, m_dim = a.shape
    else:
        m_dim, k_dim = a.shape
    n_dim = b.shape[0] if mode == "nt" else b.shape[1]
    tm, tn, tk = _pick(m_dim, MM_TM), _pick(n_dim, MM_TN), _pick(k_dim, MM_TK)
    grid = (m_dim // tm, n_dim // tn, k_dim // tk)
    nk = grid[2]
    dims = _DIMS[mode]
    c_arrays, c_exchange = carry if carry else ((), False)
    nc = len(c_arrays)

    def at_step(first):
        conds = [pl.program_id(q) == (0 if first else grid[q] - 1) for q in range(3)]
        return jnp.logical_and(jnp.logical_and(conds[0], conds[1]), conds[2])

    def body(a_ref, b_ref, *rest):
        c_ins, o_ref, c_outs = rest[:nc], rest[nc], rest[nc + 1:2 * nc + 1]
        scratch = rest[2 * nc + 1:]
        sems = scratch[1:] if nk > 1 else scratch
        if nc:
            @pl.when(at_step(True))
            def _():
                _comm_start(c_ins, c_outs, *sems, c_exchange)

        prod = lax.dot_general(a_ref[...], b_ref[...], dims, preferred_element_type=F32)
        if nk == 1:
            o_ref[...] = prod.astype(o_ref.dtype)
        else:
            acc_ref, k = scratch[0], pl.program_id(2)

            @pl.when(k == 0)
            def _():
                acc_ref[...] = prod

            @pl.when(jnp.logical_and(k > 0, k < nk - 1))
            def _():
                acc_ref[...] += prod

            @pl.when(k == nk - 1)
            def _():
                o_ref[...] = (acc_ref[...] + prod).astype(o_ref.dtype)

        if nc:
            @pl.when(at_step(False))
            def _():
                _comm_wait(c_ins, c_outs, *sems, c_exchange)

    if mode == "tn":
        a_spec = pl.BlockSpec((tk, tm), lambda i, j, k: (k, i))
    else:
        a_spec = pl.BlockSpec((tm, tk), lambda i, j, k: (i, k))
    if mode == "nt":
        b_spec = pl.BlockSpec((tn, tk), lambda i, j, k: (j, k))
    else:
        b_spec = pl.BlockSpec((tk, tn), lambda i, j, k: (k, j))
    any_spec = pl.BlockSpec(memory_space=pl.ANY)
    outs = pl.pallas_call(
        body, out_shape=[jax.ShapeDtypeStruct((m_dim, n_dim), out_dtype)] + _comm_out_shape(c_arrays, c_exchange),
        grid=grid, in_specs=[a_spec, b_spec] + [any_spec] * nc,
        out_specs=[pl.BlockSpec((tm, tn), lambda i, j, k: (i, j))] + [any_spec] * nc,
        scratch_shapes=([pltpu.VMEM((tm, tn), F32)] if nk > 1 else []) + _comm_sems(nc),
        compiler_params=_cparams(("arbitrary",) * 3 if nc else ("parallel", "parallel", "arbitrary")),
        name=name)(a, b, *c_arrays)
    return (outs[0], outs[1:]) if nc else outs[0]


def _rowwise(fn, name, n_rows, tr, row_ins, full_ins, row_outs, acc_outs=()):
    tr = min(tr, n_rows)
    n_ri, n_fi, n_ro = len(row_ins), len(full_ins), len(row_outs)

    def body(*refs):
        i = pl.program_id(0)
        ri, fi = refs[:n_ri], refs[n_ri:n_ri + n_fi]
        ro, ao = refs[n_ri + n_fi:n_ri + n_fi + n_ro], refs[n_ri + n_fi + n_ro:]

        @pl.when(i == 0)
        def _():
            for r in ao:
                r[...] = jnp.zeros_like(r)

        fn(i, ri, fi, ro, ao)

    in_specs = [pl.BlockSpec((tr, w), functools.partial(lambda c, i: (i, c), cb)) for _, w, cb in row_ins]
    in_specs += [pl.BlockSpec(a.shape, functools.partial(lambda nd, i: (0,) * nd, a.ndim)) for a in full_ins]
    out_specs = [pl.BlockSpec((tr, w), lambda i: (i, 0)) for w, _ in row_outs]
    out_specs += [pl.BlockSpec(s, functools.partial(lambda nd, i: (0,) * nd, len(s))) for s, _ in acc_outs]
    out_shape = [jax.ShapeDtypeStruct((n_rows, w), d) for w, d in row_outs]
    out_shape += [jax.ShapeDtypeStruct(s, d) for s, d in acc_outs]
    return pl.pallas_call(
        body, out_shape=out_shape, grid=(n_rows // tr,), in_specs=in_specs, out_specs=out_specs,
        compiler_params=_cparams(("arbitrary",)), name=name)(*[a for a, _, _ in row_ins], *full_ins)


def _norm_fwd(x, g, name):
    t_dim, d = x.shape

    def fn(i, ri, fi, ro, ao):
        xn, _ = _rms(ri[0][...])
        ro[0][...] = (xn * fi[0][...]).astype(BF16)

    return _rowwise(fn, name, t_dim, ROW_TILE, [(x, d, 0)], [g], [(d, BF16)])[0]


def _resid_fwd(x, y, g, coef, name):
    t_dim, d = x.shape

    def fn(i, ri, fi, ro, ao):
        yn, _ = _rms(ri[1][...])
        ro[0][...] = ri[0][...] + coef * (yn * fi[0][...])

    return _rowwise(fn, name, t_dim, ROW_TILE, [(x, d, 0), (y, d, 0)], [g], [(d, F32)])[0]


def _resid_bwd(y, dxo, g, coef, name):
    t_dim, d = y.shape

    def fn(i, ri, fi, ro, ao):
        yn, r = _rms(ri[0][...])
        dyn = coef * ri[1][...]
        ao[0][...] += _colsum(dyn * yn)
        ro[0][...] = _rms_bwd(yn, r, dyn * fi[0][...]).astype(BF16)

    return _rowwise(fn, name, t_dim, ROW_TILE, [(y, d, 0), (dxo, d, 0)], [g], [(d, BF16)], [((1, d), F32)])


def _norm_bwd(x, dhs, dxo, g, name):
    t_dim, d = x.shape
    n = len(dhs)

    def fn(i, ri, fi, ro, ao):
        xn, r = _rms(ri[0][...])
        dh = ri[2][...]
        for q in range(1, n):
            dh = dh + ri[2 + q][...]
        ao[0][...] += _colsum(dh * xn)
        ro[0][...] = ri[1][...] + _rms_bwd(xn, r, dh * fi[0][...])

    return _rowwise(fn, name, t_dim, ROW_TILE, [(x, d, 0), (dxo, d, 0)] + [(a, d, 0) for a in dhs], [g],
                    [(d, F32)], [((1, d), F32)])


def _swiglu_fwd(gu, name):
    t_dim, f2 = gu.shape
    f = f2 // 2

    def fn(i, ri, fi, ro, ao):
        gate, up = ri[0][:, :f].astype(F32), ri[0][:, f:].astype(F32)
        ro[0][...] = (gate * _sigmoid(gate) * up).astype(BF16)

    return _rowwise(fn, name, t_dim, ROW_TILE_WIDE, [(gu, f2, 0)], [], [(f, BF16)])[0]


def _swiglu_bwd(gu, dact, name):
    t_dim, f2 = gu.shape
    f = f2 // 2

    def fn(i, ri, fi, ro, ao):
        gate, up = ri[0][:, :f].astype(F32), ri[0][:, f:].astype(F32)
        da = ri[1][...].astype(F32)
        s = _sigmoid(gate)
        ro[0][:, :f] = (da * up * (s * (1.0 + gate * (1.0 - s)))).astype(BF16)
        ro[0][:, f:] = (da * (gate * s)).astype(BF16)

    return _rowwise(fn, name, t_dim, ROW_TILE_WIDE, [(gu, f2, 0), (dact, f, 0)], [], [(f2, BF16)])[0]


def _loss_fwd_bwd(y, target, name):
    t_dim, d = y.shape

    def fn(i, ri, fi, ro, ao):
        err = ri[0][...] - ri[1][...]
        ao[0][...] += _colsum(jnp.sum(err * err, axis=1, keepdims=True)) * (0.5 / d)
        ro[0][...] = err * (1.0 / d)

    return _rowwise(fn, name, t_dim, ROW_TILE, [(y, d, 0), (target, d, 0)], [], [(d, F32)], [((1, 1), F32)])


def _ln_stats(v):
    mu = jnp.mean(v, axis=-1, keepdims=True)
    xc = v - mu
    rstd = lax.rsqrt(jnp.mean(xc * xc, axis=-1, keepdims=True) + EPS)
    return xc * rstd, rstd


def _tril_mask(upper=False):
    row = lax.broadcasted_iota(jnp.int32, (CHUNK, CHUNK), 0)
    col = lax.broadcasted_iota(jnp.int32, (CHUNK, CHUNK), 1)
    return row <= col if upper else row >= col


def _gm_fwd(z_a, ln_g, ln_b, w_s, b_s, gmw, name):
    t_dim = z_a.shape[0]
    gw = gmw // GROUPS
    tr = min(GM_TILE, t_dim)

    def fn(i, ri, fi, ro, ao):
        lng, lnb, ws_ref, bs_ref = fi
        u = _gelu(ri[0][...].astype(F32))
        vhat, _ = _ln_stats(_gelu(ri[1][...].astype(F32)))
        vn = (vhat * lng[...] + lnb[...]).astype(BF16)
        mask = _tril_mask()
        for g in range(GROUPS):
            wg = jnp.where(mask, ws_ref[g], 0.0).astype(BF16)
            for c in range(tr // CHUNK):
                rows, cols = slice(c * CHUNK, (c + 1) * CHUNK), slice(g * gw, (g + 1) * gw)
                s = jnp.dot(wg, vn[rows, cols], preferred_element_type=F32) + bs_ref[g]
                ro[0][rows, cols] = (u[rows, cols] * s).astype(BF16)

    return _rowwise(fn, name, t_dim, tr, [(z_a, gmw, 0), (z_a, gmw, 1)], [ln_g, ln_b, w_s, b_s], [(gmw, BF16)])[0]


def _gm_bwd(z_a, do_a, ln_g, ln_b, w_s, w_s_t, b_s, gmw, name):
    t_dim = z_a.shape[0]
    gw = gmw // GROUPS
    tr = min(GM_TILE, t_dim)

    def fn(i, ri, fi, ro, ao):
        lng, lnb, ws_ref, wst_ref, bs_ref = fi
        d_lng, d_lnb, d_ws, d_bs = ao
        a_u, a_v, do = ri[0][...].astype(F32), ri[1][...].astype(F32), ri[2][...]
        u = _gelu(a_u)
        vhat, rstd = _ln_stats(_gelu(a_v))
        vn = (vhat * lng[...] + lnb[...]).astype(BF16)
        mask = _tril_mask()
        wgs = [jnp.where(mask, ws_ref[g], 0.0).astype(BF16) for g in range(GROUPS)]
        wgts = [jnp.where(_tril_mask(upper=True), wst_ref[g], 0.0).astype(BF16) for g in range(GROUPS)]
        for c in range(tr // CHUNK):
            rows = slice(c * CHUNK, (c + 1) * CHUNK)
            dvn_parts = []
            for g in range(GROUPS):
                cols = slice(g * gw, (g + 1) * gw)
                vn_blk = vn[rows, cols]
                s = jnp.dot(wgs[g], vn_blk, preferred_element_type=F32) + bs_ref[g]
                ro[0][rows, cols] = (do[rows, cols] * s * _gelu_grad(a_u[rows, cols])).astype(BF16)
                ds = do[rows, cols] * u[rows, cols]
                d_bs[g] += jnp.sum(ds, axis=1, keepdims=True)
                dsb = ds.astype(BF16)
                dw = lax.dot_general(dsb, vn_blk, _DIMS["nt"], preferred_element_type=F32)
                d_ws[g] += jnp.where(mask, dw, 0.0)
                dvn_parts.append(jnp.dot(wgts[g], dsb, preferred_element_type=F32))
            dvn = jnp.concatenate(dvn_parts, axis=1)
            vh, rs = vhat[rows], rstd[rows]
            d_lng[...] += _colsum(dvn * vh)
            d_lnb[...] += _colsum(dvn)
            dvh = dvn * lng[...]
            dv = rs * (dvh - jnp.mean(dvh, axis=-1, keepdims=True) - vh * jnp.mean(dvh * vh, axis=-1, keepdims=True))
            ro[1][rows, :] = (dv * _gelu_grad(a_v[rows])).astype(BF16)

    return _rowwise(fn, name, t_dim, tr, [(z_a, gmw, 0), (z_a, gmw, 1), (do_a, gmw, 0)],
                    [ln_g, ln_b, w_s, w_s_t, b_s], [(gmw, BF16), (gmw, BF16)],
                    [((1, gmw), F32), ((1, gmw), F32), ((GROUPS, CHUNK, CHUNK), F32), ((GROUPS, CHUNK, 1), F32)])


def _glu_fwd(z_a, gmw, cw, name):
    t_dim = z_a.shape[0]
    cb = (2 * gmw) // cw

    def fn(i, ri, fi, ro, ao):
        ro[0][...] = ri[0][...].astype(F32) * _sigmoid(ri[1][...].astype(F32))

    return _rowwise(fn, name, t_dim, ROW_TILE, [(z_a, cw, cb), (z_a, cw, cb + 1)], [], [(cw, F32)])[0]


def _conv_fwd(a, w, b, ln_g, ln_b, name):
    t_dim, c_dim = a.shape
    tr = min(CONV_TILE, t_dim)
    hb = tr // CONV_HALO

    def body(cur_ref, prev_ref, w_ref, b_ref, g_ref, be_ref, y_ref, o_ref, buf):
        i = pl.program_id(0)
        buf[0:CONV_HALO, :] = jnp.where(i > 0, prev_ref[...], 0.0)
        buf[CONV_HALO:, :] = cur_ref[...]
        for cs in range(c_dim // 128):
            lanes = pl.ds(cs * 128, 128)
            acc = jnp.zeros((tr, 128), F32)
            for k in range(CONV_K):
                acc = acc + w_ref[k:k + 1, lanes] * buf[pl.ds(k + 2, tr), lanes]
            y_ref[:, lanes] = acc + b_ref[:, lanes]
        n_hat, _ = _ln_stats(y_ref[...])
        n = n_hat * g_ref[...] + be_ref[...]
        o_ref[...] = (n * _sigmoid(n)).astype(BF16)

    full = lambda arr: pl.BlockSpec(arr.shape, lambda i: (0, 0))
    return pl.pallas_call(
        body, out_shape=[jax.ShapeDtypeStruct((t_dim, c_dim), F32), jax.ShapeDtypeStruct((t_dim, c_dim), BF16)],
        grid=(t_dim // tr,),
        in_specs=[pl.BlockSpec((tr, c_dim), lambda i: (i, 0)),
                  pl.BlockSpec((CONV_HALO, c_dim), lambda i: (jnp.maximum(i * hb - 1, 0), 0)),
                  full(w), full(b), full(ln_g), full(ln_b)],
        out_specs=[pl.BlockSpec((tr, c_dim), lambda i: (i, 0)), pl.BlockSpec((tr, c_dim), lambda i: (i, 0))],
        scratch_shapes=[pltpu.VMEM((tr + CONV_HALO, c_dim), F32)],
        compiler_params=_cparams(("arbitrary",)), name=name)(a, a, w, b, ln_g, ln_b)


def _conv_ln_bwd(y, do_b, ln_g, ln_b, name):
    t_dim, c_dim = y.shape

    def fn(i, ri, fi, ro, ao):
        n_hat, rstd = _ln_stats(ri[0][...])
        n = n_hat * fi[0][...] + fi[1][...]
        s = _sigmoid(n)
        dn = ri[1][...] * (s * (1.0 + n * (1.0 - s)))
        ao[0][...] += _colsum(dn * n_hat)
        ao[1][...] += _colsum(dn)
        dnh = dn * fi[0][...]
        ro[0][...] = rstd * (dnh - jnp.mean(dnh, axis=-1, keepdims=True)
                             - n_hat * jnp.mean(dnh * n_hat, axis=-1, keepdims=True))

    return _rowwise(fn, name, t_dim, ROW_TILE, [(y, c_dim, 0), (do_b, c_dim, 0)], [ln_g, ln_b], [(c_dim, F32)],
                    [((1, c_dim), F32), ((1, c_dim), F32)])


def _conv_bwd(dy, a, z_a, w, gmw, name):
    t_dim, c_dim = a.shape
    tr = min(CONV_TILE, t_dim)
    hb = tr // CONV_HALO
    n_halo = t_dim // CONV_HALO
    nb = t_dim // tr
    cb = (2 * gmw) // c_dim

    def body(dy_ref, dyn_ref, a_ref, ap_ref, val_ref, gate_ref, w_ref, dval_ref, dgate_ref, dw_ref, db_ref,
             dbuf, abuf, da_buf):
        i = pl.program_id(0)

        @pl.when(i == 0)
        def _():
            dw_ref[...] = jnp.zeros_like(dw_ref)
            db_ref[...] = jnp.zeros_like(db_ref)

        dbuf[0:tr, :] = dy_ref[...]
        dbuf[tr:, :] = jnp.where(i < nb - 1, dyn_ref[...], 0.0)
        abuf[0:CONV_HALO, :] = jnp.where(i > 0, ap_ref[...], 0.0)
        abuf[CONV_HALO:, :] = a_ref[...]
        db_ref[...] += _colsum(dy_ref[...])
        for cs in range(c_dim // 128):
            lanes = pl.ds(cs * 128, 128)
            dyc = dy_ref[:, lanes]
            acc = jnp.zeros((tr, 128), F32)
            for k in range(CONV_K):
                acc = acc + w_ref[k:k + 1, lanes] * dbuf[pl.ds(CONV_K - 1 - k, tr), lanes]
                dw_ref[k:k + 1, lanes] += _colsum(dyc * abuf[pl.ds(k + 2, tr), lanes])
            da_buf[:, lanes] = acc
        da = da_buf[...]
        s = _sigmoid(gate_ref[...].astype(F32))
        dval_ref[...] = (da * s).astype(BF16)
        dgate_ref[...] = (da * val_ref[...].astype(F32) * s * (1.0 - s)).astype(BF16)

    row = lambda cblk: pl.BlockSpec((tr, c_dim), functools.partial(lambda c, i: (i, c), cblk))
    return pl.pallas_call(
        body,
        out_shape=[jax.ShapeDtypeStruct((t_dim, c_dim), BF16), jax.ShapeDtypeStruct((t_dim, c_dim), BF16),
                   jax.ShapeDtypeStruct((CONV_HALO, c_dim), F32), jax.ShapeDtypeStruct((1, c_dim), F32)],
        grid=(nb,),
        in_specs=[row(0), pl.BlockSpec((CONV_HALO, c_dim), lambda i: (jnp.minimum((i + 1) * hb, n_halo - 1), 0)),
                  row(0), pl.BlockSpec((CONV_HALO, c_dim), lambda i: (jnp.maximum(i * hb - 1, 0), 0)),
                  row(cb), row(cb + 1), pl.BlockSpec(w.shape, lambda i: (0, 0))],
        out_specs=[row(0), row(0), pl.BlockSpec((CONV_HALO, c_dim), lambda i: (0, 0)),
                   pl.BlockSpec((1, c_dim), lambda i: (0, 0))],
        scratch_shapes=[pltpu.VMEM((tr + CONV_HALO, c_dim), F32), pltpu.VMEM((tr + CONV_HALO, c_dim), F32),
                        pltpu.VMEM((tr, c_dim), F32)],
        compiler_params=_cparams(("arbitrary",)), name=name)(dy, dy, a, a, z_a, z_a, w)


def _mla_norm_fwd(z_c, q_g, kv_g, qr, kvr, name):
    t_dim, cwid = z_c.shape

    def fn(i, ri, fi, ro, ao):
        cq, _ = _rms(ri[0][:, :qr])
        ckv, _ = _rms(ri[0][:, qr:qr + kvr])
        ro[0][...] = (cq * fi[0][...]).astype(BF16)
        ro[1][...] = (ckv * fi[1][...]).astype(BF16)

    return _rowwise(fn, name, t_dim, ROW_TILE, [(z_c, cwid, 0)], [q_g, kv_g], [(qr, BF16), (kvr, BF16)])


def _rope(t, cf, s1, s2):
    return t * cf + pltpu.roll(t, 96, 1) * s1 + pltpu.roll(t, 32, 1) * s2


def _rope_t(g, cf, s1, s2):
    return g * cf + pltpu.roll(g * s1, 32, 1) + pltpu.roll(g * s2, 96, 1)


def _rope_fwd(q_pre, k_nope, z_c, cf, s1, s2, rope_blk, scale, name):
    t_dim = q_pre.shape[0]

    def fn(i, ri, fi, ro, ao):
        c, a, b = ri[3][...], ri[4][...], ri[5][...]
        kt = _rope(ri[2][...], c, a, b).astype(BF16)
        for h in range(N_HEADS):
            ro[0][:, h * HQ:h * HQ + NOPE] = (ri[0][:, h * HQ:h * HQ + NOPE].astype(F32) * scale).astype(BF16)
            ro[0][:, h * HQ + NOPE:(h + 1) * HQ] = (
                _rope(ri[0][:, h * HQ + NOPE:(h + 1) * HQ].astype(F32), c, a, b) * scale).astype(BF16)
            ro[1][:, h * HQ:h * HQ + NOPE] = ri[1][:, h * NOPE:(h + 1) * NOPE].astype(BF16)
            ro[1][:, h * HQ + NOPE:(h + 1) * HQ] = kt

    return _rowwise(fn, name, t_dim, ROW_TILE_WIDE,
                    [(q_pre, N_HEADS * HQ, 0), (k_nope, N_HEADS * NOPE, 0), (z_c, 128, rope_blk),
                     (cf, 128, 0), (s1, 128, 0), (s2, 128, 0)], [],
                    [(N_HEADS * HQ, BF16), (N_HEADS * HQ, BF16)])


def _rope_bwd(dq_cat, dk_cat, cf, s1, s2, scale, name):
    t_dim = dq_cat.shape[0]

    def fn(i, ri, fi, ro, ao):
        c, a, b = ri[2][...], ri[3][...], ri[4][...]
        dkt = jnp.zeros((ri[0].shape[0], 128), F32)
        for h in range(N_HEADS):
            ro[0][:, h * HQ:h * HQ + NOPE] = (ri[0][:, h * HQ:h * HQ + NOPE] * scale).astype(BF16)
            ro[0][:, h * HQ + NOPE:(h + 1) * HQ] = _rope_t(ri[0][:, h * HQ + NOPE:(h + 1) * HQ] * scale, c, a, b).astype(BF16)
            ro[1][:, h * NOPE:(h + 1) * NOPE] = ri[1][:, h * HQ:h * HQ + NOPE].astype(BF16)
            dkt = dkt + ri[1][:, h * HQ + NOPE:(h + 1) * HQ].astype(F32)
        ro[2][...] = _rope_t(dkt, c, a, b)

    return _rowwise(fn, name, t_dim, ROW_TILE_WIDE,
                    [(dq_cat, N_HEADS * HQ, 0), (dk_cat, N_HEADS * HQ, 0), (cf, 128, 0), (s1, 128, 0), (s2, 128, 0)],
                    [], [(N_HEADS * HQ, BF16), (N_HEADS * NOPE, BF16), (128, F32)])


def _mla_norm_bwd(z_c, dcq, dckv_k, dckv_v, dkr, q_g, kv_g, qr, kvr, name):
    t_dim, cwid = z_c.shape

    def fn(i, ri, fi, ro, ao):
        cq, rq = _rms(ri[0][:, :qr])
        ckv, rkv = _rms(ri[0][:, qr:qr + kvr])
        dq = ri[1][...]
        dkv = ri[2][...] + ri[3][...]
        ao[0][...] += _colsum(dq * cq)
        ao[1][...] += _colsum(dkv * ckv)
        ro[0][:, :qr] = _rms_bwd(cq, rq, dq * fi[0][...]).astype(BF16)
        ro[0][:, qr:qr + kvr] = _rms_bwd(ckv, rkv, dkv * fi[1][...]).astype(BF16)
        ro[0][:, qr + kvr:] = ri[4][...].astype(BF16)

    return _rowwise(fn, name, t_dim, ROW_TILE,
                    [(z_c, cwid, 0), (dcq, qr, 0), (dckv_k, kvr, 0), (dckv_v, kvr, 0), (dkr, 128, 0)], [q_g, kv_g],
                    [(cwid, BF16)], [((1, qr), F32), ((1, kvr), F32)])


def _causal_pairs(n, by_key):
    if by_key:
        pairs = [(i, j) for j in range(n) for i in range(j, n)]
    else:
        pairs = [(i, j) for i in range(n) for j in range(i + 1)]
    return (np.array([p[0] for p in pairs], np.int32), np.array([p[1] for p in pairs], np.int32))


def _attn_fwd(q, k, v, name, carry=None):
    t_dim = q.shape[0]
    tq = min(ATT_TILE, t_dim)
    nh = ATT_FWD_HEADS
    qi, kj = _causal_pairs(t_dim // tq, by_key=False)
    n_steps = len(qi)
    c_arrays, c_exchange = carry if carry else ((), False)
    nc = len(c_arrays)

    def body(qi_ref, kj_ref, q_ref, k_ref, v_ref, *rest):
        c_ins, (o_ref, lse_ref), c_outs = rest[:nc], rest[nc:nc + 2], rest[nc + 2:2 * nc + 2]
        m_sc, l_sc, acc_sc = rest[2 * nc + 2:2 * nc + 5]
        sems = rest[2 * nc + 5:]
        s_id = pl.program_id(1)
        i, j = qi_ref[s_id], kj_ref[s_id]
        if nc:
            @pl.when(jnp.logical_and(pl.program_id(0) == 0, s_id == 0))
            def _():
                _comm_start(c_ins, c_outs, *sems, c_exchange)

        @pl.when(j == 0)
        def _():
            m_sc[...] = jnp.full_like(m_sc, NEG)
            l_sc[...] = jnp.zeros_like(l_sc)
            acc_sc[...] = jnp.zeros_like(acc_sc)

        def step(masked):
            for hh in range(nh):
                s = lax.dot_general(q_ref[:, hh * HQ:(hh + 1) * HQ], k_ref[:, hh * HQ:(hh + 1) * HQ], _DIMS["nt"],
                                    preferred_element_type=F32)
                if masked:
                    row = lax.broadcasted_iota(jnp.int32, (tq, tq), 0)
                    col = lax.broadcasted_iota(jnp.int32, (tq, tq), 1)
                    s = jnp.where(col <= row, s, NEG)
                m_prev = m_sc[hh]
                m_new = jnp.maximum(m_prev, jnp.max(s, axis=1, keepdims=True))
                alpha = jnp.exp(m_prev - m_new)
                p = jnp.exp(s - m_new)
                l_sc[hh] = alpha * l_sc[hh] + jnp.sum(p, axis=1, keepdims=True)
                acc_sc[hh] = alpha * acc_sc[hh] + jnp.dot(p.astype(BF16), v_ref[:, hh * VDIM:(hh + 1) * VDIM],
                                                          preferred_element_type=F32)
                m_sc[hh] = m_new

        @pl.when(j < i)
        def _():
            step(False)

        @pl.when(j == i)
        def _():
            step(True)
            for hh in range(nh):
                o_ref[:, hh * VDIM:(hh + 1) * VDIM] = (acc_sc[hh] / l_sc[hh]).astype(BF16)
                lse_ref[hh] = m_sc[hh] + jnp.log(l_sc[hh])

        if nc:
            @pl.when(jnp.logical_and(pl.program_id(0) == N_HEADS // nh - 1, s_id == n_steps - 1))
            def _():
                _comm_wait(c_ins, c_outs, *sems, c_exchange)

    any_spec = pl.BlockSpec(memory_space=pl.ANY)
    grid_spec = pltpu.PrefetchScalarGridSpec(
        num_scalar_prefetch=2, grid=(N_HEADS // nh, n_steps),
        in_specs=[pl.BlockSpec((tq, nh * HQ), lambda h, s, qi, kj: (qi[s], h)),
                  pl.BlockSpec((tq, nh * HQ), lambda h, s, qi, kj: (kj[s], h)),
                  pl.BlockSpec((tq, nh * VDIM), lambda h, s, qi, kj: (kj[s], h))] + [any_spec] * nc,
        out_specs=[pl.BlockSpec((tq, nh * VDIM), lambda h, s, qi, kj: (qi[s], h)),
                   pl.BlockSpec((nh, tq, 1), lambda h, s, qi, kj: (h, qi[s], 0))] + [any_spec] * nc,
        scratch_shapes=[pltpu.VMEM((nh, tq, 1), F32), pltpu.VMEM((nh, tq, 1), F32), pltpu.VMEM((nh, tq, VDIM), F32)]
        + _comm_sems(nc))
    return pl.pallas_call(
        body, grid_spec=grid_spec,
        out_shape=[jax.ShapeDtypeStruct((t_dim, N_HEADS * VDIM), BF16), jax.ShapeDtypeStruct((N_HEADS, t_dim, 1), F32)]
        + _comm_out_shape(c_arrays, c_exchange),
        compiler_params=_cparams(("arbitrary", "arbitrary")), name=name)(
            jnp.asarray(qi), jnp.asarray(kj), q, k, v, *c_arrays)


def _attn_delta(do, o, name):
    t_dim = do.shape[0]
    tr = min(ATT_TILE, t_dim)

    def body(do_ref, o_ref, d_ref):
        d_ref[0] = jnp.sum(do_ref[...].astype(F32) * o_ref[...].astype(F32), axis=1, keepdims=True)

    return pl.pallas_call(
        body, out_shape=jax.ShapeDtypeStruct((N_HEADS, t_dim, 1), F32), grid=(N_HEADS, t_dim // tr),
        in_specs=[pl.BlockSpec((tr, VDIM), lambda h, i: (i, h)), pl.BlockSpec((tr, VDIM), lambda h, i: (i, h))],
        out_specs=pl.BlockSpec((1, tr, 1), lambda h, i: (h, i, 0)),
        compiler_params=_cparams(("parallel", "parallel")), name=name)(do, o)


def _attn_bwd(q, k, v, do, lse_row, delta_row, name, carry=None):
    t_dim = q.shape[0]
    tq = min(ATT_TILE, t_dim)
    nq = t_dim // tq
    qi, kj = _causal_pairs(nq, by_key=True)
    n_steps = len(qi)
    c_arrays, c_exchange = carry if carry else ((), False)
    nc = len(c_arrays)

    def body(qi_ref, kj_ref, q_ref, k_ref, v_ref, do_ref, lse_ref, dl_ref, *rest):
        c_ins, (dq_ref, dk_ref, dv_ref), c_outs = rest[:nc], rest[nc:nc + 3], rest[nc + 3:2 * nc + 3]
        dk_sc, dv_sc = rest[2 * nc + 3:2 * nc + 5]
        sems = rest[2 * nc + 5:]
        s_id = pl.program_id(1)
        i, j = qi_ref[s_id], kj_ref[s_id]
        if nc:
            @pl.when(jnp.logical_and(pl.program_id(0) == 0, s_id == 0))
            def _():
                _comm_start(c_ins, c_outs, *sems, c_exchange)

        @pl.when(s_id == 0)
        def _():
            dq_ref[...] = jnp.zeros_like(dq_ref)

        rows = pl.ds(pl.multiple_of(i * tq, tq), tq)

        def step(masked):
            s_t = lax.dot_general(k_ref[...], q_ref[...], _DIMS["nt"], preferred_element_type=F32)
            p_t = jnp.exp(s_t - lse_ref[0])
            if masked:
                row = lax.broadcasted_iota(jnp.int32, (tq, tq), 0)
                col = lax.broadcasted_iota(jnp.int32, (tq, tq), 1)
                p_t = jnp.where(row <= col, p_t, 0.0)
            do = do_ref[...]
            dv = jnp.dot(p_t.astype(BF16), do, preferred_element_type=F32)
            dp_t = lax.dot_general(v_ref[...], do, _DIMS["nt"], preferred_element_type=F32)
            ds_t = (p_t * (dp_t - dl_ref[0])).astype(BF16)
            dk = jnp.dot(ds_t, q_ref[...], preferred_element_type=F32)
            if masked:
                dv_sc[...] = dv
                dk_sc[...] = dk
            else:
                dv_sc[...] += dv
                dk_sc[...] += dk
            dq_ref[rows, :] += lax.dot_general(ds_t, k_ref[...], _DIMS["tn"], preferred_element_type=F32)

        @pl.when(i == j)
        def _():
            step(True)

        @pl.when(i != j)
        def _():
            step(False)

        @pl.when(i == nq - 1)
        def _():
            dk_ref[...] = dk_sc[...].astype(BF16)
            dv_ref[...] = dv_sc[...].astype(BF16)

        if nc:
            @pl.when(jnp.logical_and(pl.program_id(0) == N_HEADS - 1, s_id == n_steps - 1))
            def _():
                _comm_wait(c_ins, c_outs, *sems, c_exchange)

    any_spec = pl.BlockSpec(memory_space=pl.ANY)
    grid_spec = pltpu.PrefetchScalarGridSpec(
        num_scalar_prefetch=2, grid=(N_HEADS, n_steps),
        in_specs=[pl.BlockSpec((tq, HQ), lambda h, s, qi, kj: (qi[s], h)),
                  pl.BlockSpec((tq, HQ), lambda h, s, qi, kj: (kj[s], h)),
                  pl.BlockSpec((tq, VDIM), lambda h, s, qi, kj: (kj[s], h)),
                  pl.BlockSpec((tq, VDIM), lambda h, s, qi, kj: (qi[s], h)),
                  pl.BlockSpec((1, 1, tq), lambda h, s, qi, kj: (h, 0, qi[s])),
                  pl.BlockSpec((1, 1, tq), lambda h, s, qi, kj: (h, 0, qi[s]))] + [any_spec] * nc,
        out_specs=[pl.BlockSpec((t_dim, HQ), lambda h, s, qi, kj: (0, h)),
                   pl.BlockSpec((tq, HQ), lambda h, s, qi, kj: (kj[s], h)),
                   pl.BlockSpec((tq, VDIM), lambda h, s, qi, kj: (kj[s], h))] + [any_spec] * nc,
        scratch_shapes=[pltpu.VMEM((tq, HQ), F32), pltpu.VMEM((tq, VDIM), F32)] + _comm_sems(nc))
    return pl.pallas_call(
        body, grid_spec=grid_spec,
        out_shape=[jax.ShapeDtypeStruct((t_dim, N_HEADS * HQ), F32), jax.ShapeDtypeStruct((t_dim, N_HEADS * HQ), BF16),
                   jax.ShapeDtypeStruct((t_dim, N_HEADS * VDIM), BF16)] + _comm_out_shape(c_arrays, c_exchange),
        compiler_params=_cparams(("arbitrary", "arbitrary")), name=name)(
            jnp.asarray(qi), jnp.asarray(kj), q, k, v, do, lse_row, delta_row, *c_arrays)


def _merge_fwd(z_g, y_a, y_b, y_c, name):
    t_dim, d = y_a.shape

    def fn(i, ri, fi, ro, ao):
        acc = _sigmoid(ri[0][:, :d].astype(F32)) * ri[1][...].astype(F32)
        acc = acc + _sigmoid(ri[0][:, d:2 * d].astype(F32)) * ri[2][...].astype(F32)
        acc = acc + _sigmoid(ri[0][:, 2 * d:].astype(F32)) * ri[3][...].astype(F32)
        ro[0][...] = acc.astype(BF16)

    return _rowwise(fn, name, t_dim, ROW_TILE_WIDE, [(z_g, 3 * d, 0), (y_a, d, 0), (y_b, d, 0), (y_c, d, 0)], [],
                    [(d, BF16)])[0]


def _merge_bwd(z_g, y_a, y_b, y_c, dmerged, name):
    t_dim, d = y_a.shape

    def fn(i, ri, fi, ro, ao):
        dm = ri[4][...]
        for q in range(3):
            s = _sigmoid(ri[0][:, q * d:(q + 1) * d].astype(F32))
            ro[q][...] = (s * dm).astype(BF16)
            ro[3][:, q * d:(q + 1) * d] = (dm * ri[1 + q][...].astype(F32) * s * (1.0 - s)).astype(BF16)

    return _rowwise(fn, name, t_dim, ROW_TILE_WIDE,
                    [(z_g, 3 * d, 0), (y_a, d, 0), (y_b, d, 0), (y_c, d, 0), (dmerged, d, 0)], [],
                    [(d, BF16), (d, BF16), (d, BF16), (3 * d, BF16)])


def _adam(w, parts, m, v, name):
    r_dim, c_dim = w.shape
    limit = max(16, ADAM_BLOCK_ELEMS // c_dim // 16 * 16)
    tr = r_dim
    if r_dim > limit:
        tr = next((t for t in range(limit, 15, -16) if r_dim % t == 0), r_dim)

    def body(w_ref, p_ref, m_ref, v_ref, g_out, d_out, m_out, v_out):
        g = p_ref[0].astype(F32)
        for s in range(1, N_DEV):
            g = g + p_ref[s].astype(F32)
        m_new = ADAM_B1 * m_ref[...] + (1.0 - ADAM_B1) * g
        v_new = ADAM_B2 * v_ref[...] + (1.0 - ADAM_B2) * (g * g)
        m_hat = m_new / (1.0 - ADAM_B1 ** ADAM_STEP)
        v_hat = v_new / (1.0 - ADAM_B2 ** ADAM_STEP)
        g_out[...] = g
        d_out[...] = -ADAM_LR * (m_hat / (jnp.sqrt(v_hat) + ADAM_EPS) + ADAM_WD * w_ref[...])
        m_out[...] = m_new
        v_out[...] = v_new

    blk = pl.BlockSpec((tr, c_dim), lambda i: (i, 0))
    return pl.pallas_call(
        body, out_shape=[jax.ShapeDtypeStruct((r_dim, c_dim), F32)] * 4, grid=(r_dim // tr,),
        in_specs=[blk, pl.BlockSpec((N_DEV, tr, c_dim), lambda i: (0, i, 0)), blk, blk], out_specs=[blk] * 4,
        compiler_params=_cparams(("parallel",)), name=name)(w, parts, m, v)


def _me_and_peers():
    x, y, c = lax.axis_index("x"), lax.axis_index("y"), lax.axis_index("c")
    me = 4 * x + 2 * y + c
    peers = []
    for k in range(1, N_DEV):
        px, py, pc = x ^ (k >> 2), y ^ ((k >> 1) & 1), c ^ (k & 1)
        peers.append(((px, py, pc), 4 * px + 2 * py + pc))
    return me, peers


def _comm_copies(ins, outs, send_sems, recv_sems, local_sems, exchange, with_arrivals):
    me, peers = _me_and_peers()

    def src(w, dest_idx):
        return ins[w].at[dest_idx] if exchange else ins[w]

    def remote(w, k, dev, src_ref, dst_ref):
        return pltpu.make_async_remote_copy(
            src_ref=src_ref, dst_ref=dst_ref, send_sem=send_sems.at[w * (N_DEV - 1) + k],
            recv_sem=recv_sems.at[w * (N_DEV - 1) + k], device_id=dev, device_id_type=pl.DeviceIdType.MESH)

    local = [pltpu.make_async_copy(src(w, me), outs[w].at[me], local_sems.at[w]) for w in range(len(ins))]
    sends, arrivals = [], []
    for w in range(len(ins)):
        for k, (dev, idx) in enumerate(peers):
            sends.append(remote(w, k, dev, src(w, idx), outs[w].at[me]))
            if with_arrivals:
                arrivals.append(remote(w, k, dev, src(w, idx), outs[w].at[idx]))
    return local, sends, arrivals


def _comm_start(ins, outs, send_sems, recv_sems, local_sems, exchange):
    local, sends, _ = _comm_copies(ins, outs, send_sems, recv_sems, local_sems, exchange, False)
    for cp in local + sends:
        cp.start()


def _comm_wait(ins, outs, send_sems, recv_sems, local_sems, exchange):
    local, sends, arrivals = _comm_copies(ins, outs, send_sems, recv_sems, local_sems, exchange, True)
    for cp in arrivals:
        cp.wait_recv()
    for cp in sends:
        cp.wait_send()
    for cp in local:
        cp.wait()


def _comm_sems(n):
    if not n:
        return []
    return [pltpu.SemaphoreType.DMA((n * (N_DEV - 1),)), pltpu.SemaphoreType.DMA((n * (N_DEV - 1),)),
            pltpu.SemaphoreType.DMA((n,))]


def _comm_out_shape(arrays, exchange):
    return [jax.ShapeDtypeStruct(a.shape if exchange else (N_DEV,) + a.shape, a.dtype) for a in arrays]


def _comm(arrays, exchange, name):
    n = len(arrays)
    hbm = pl.BlockSpec(memory_space=pltpu.HBM)

    def body(*refs):
        ins, outs, sems = refs[:n], refs[n:2 * n], refs[2 * n:]
        _comm_start(ins, outs, *sems, exchange)
        _comm_wait(ins, outs, *sems, exchange)

    return pl.pallas_call(
        body, out_shape=_comm_out_shape(arrays, exchange), in_specs=[hbm] * n, out_specs=[hbm] * n,
        scratch_shapes=_comm_sems(n), name=name)(*arrays)


def _unshard(name, g):
    if name in COL_SHARDED:
        return jnp.transpose(g, (1, 0, 2)).reshape(g.shape[1], N_DEV * g.shape[2])
    return g.reshape(N_DEV * g.shape[1], g.shape[2])


def _to_shards(name, full):
    if name in COL_SHARDED:
        r, c = full.shape
        return jnp.transpose(full.reshape(r, N_DEV, c // N_DEV), (1, 0, 2))
    return full.reshape(N_DEV, full.shape[0] // N_DEV, full.shape[1])


def _mmc(a, b, mode, out_dtype, name, carry):
    if carry:
        return _mm(a, b, mode, out_dtype, name + "_comm", carry)
    return _mm(a, b, mode, out_dtype, name), []


def _ffn_fwd(x, p, tag, carry_in=None, carry_out=None):
    h = _norm_fwd(x, p["norm_pre"], f"{tag}_norm")
    gu, got_in = _mmc(h, p["w_in"], "nn", BF16, f"{tag}_in", carry_in)
    act = _swiglu_fwd(gu, f"{tag}_act")
    y, got_out = _mmc(act, p["w_out"], "nn", F32, f"{tag}_out", carry_out)
    x_new = _resid_fwd(x, y, p["norm_post"], 0.5, f"{tag}_resid")
    return x_new, dict(x=x, h=h, gu=gu, act=act, y=y), got_in, got_out


def _ffn_bwd(dxo, s, p, tag, carry_out=None, carry_in=None):
    dy, dg_post = _resid_bwd(s["y"], dxo, p["norm_post"], 0.5, f"{tag}_resid_bwd")
    dact, got_out = _mmc(dy, p["w_out"], "nt", BF16, f"{tag}_out_dx", carry_out)
    dw_out = _mm(s["act"], dy, "tn", F32, f"{tag}_out_dw")
    dgu = _swiglu_bwd(s["gu"], dact, f"{tag}_act_bwd")
    dh, got_in = _mmc(dgu, p["w_in"], "nt", F32, f"{tag}_in_dx", carry_in)
    dw_in = _mm(s["h"], dgu, "tn", F32, f"{tag}_in_dw")
    dx, dg_pre = _norm_bwd(s["x"], [dh], dxo, p["norm_pre"], f"{tag}_norm_bwd")
    return dx, dict(norm_pre=dg_pre, norm_post=dg_post, w_in=dw_in, w_out=dw_out), got_out, got_in


def _mixer_fwd(x, p, rope_tabs, dims, carry):
    gmw, cw, qr, kvr = dims["gmw"], dims["cw"], dims["qr"], dims["kvr"]
    cf, s1, s2 = rope_tabs
    scale = (NOPE + ROPE) ** -0.5
    h = _norm_fwd(x, p["norm_pre"], "mix_norm")
    z_a = _mm(h, p["w_a"], "nn", BF16, "mix_in_a")
    z_c = _mm(h, p["w_c"], "nn", F32, "mix_in_c")
    z_g = _mm(h, p["w_g"], "nn", BF16, "mix_in_g")
    o_a = _gm_fwd(z_a, p["gm_ln_g"], p["gm_ln_b"], p["gm_w_s"], p["gm_b_s"], gmw, "gm_fwd")
    a = _glu_fwd(z_a, gmw, cw, "glu_fwd")
    y_conv, o_b = _conv_fwd(a, p["conv_w"], p["conv_b"], p["conv_ln_g"], p["conv_ln_b"], "conv_fwd")
    cqn, ckvn = _mla_norm_fwd(z_c, p["q_norm"], p["kv_norm"], qr, kvr, "mla_norm")
    q_pre = _mm(cqn, p["w_uq"], "nn", BF16, "mla_uq")
    k_nope = _mm(ckvn, p["w_uk"], "nn", BF16, "mla_uk")
    v = _mm(ckvn, p["w_uv"], "nn", BF16, "mla_uv")
    q_cat, k_cat = _rope_fwd(q_pre, k_nope, z_c, cf, s1, s2, (qr + kvr) // 128, scale, "rope_fwd")
    o_c, lse, *carried = _attn_fwd(q_cat, k_cat, v, "attn_fwd_comm" if carry else "attn_fwd", carry)
    y_a = _mm(o_a, p["wb_a"], "nn", BF16, "branch_a")
    y_b = _mm(o_b, p["wb_b"], "nn", BF16, "branch_b")
    y_c = _mm(o_c, p["wb_c"], "nn", BF16, "branch_c")
    merged = _merge_fwd(z_g, y_a, y_b, y_c, "merge_fwd")
    m = _mm(merged, p["w_out"], "nn", F32, "mix_out")
    x_new = _resid_fwd(x, m, p["norm_post"], 1.0, "mix_resid")
    saved = dict(x=x, h=h, z_a=z_a, z_c=z_c, z_g=z_g, o_a=o_a, a=a, y_conv=y_conv, o_b=o_b, cqn=cqn, ckvn=ckvn,
                 v=v, q_cat=q_cat, k_cat=k_cat, o_c=o_c, lse=lse, y_a=y_a, y_b=y_b, y_c=y_c, merged=merged, m=m)
    return x_new, saved, carried


def _mixer_bwd(dxo, s, p, rope_tabs, dims, carry):
    gmw, cw, qr, kvr = dims["gmw"], dims["cw"], dims["qr"], dims["kvr"]
    cf, s1, s2 = rope_tabs
    scale = (NOPE + ROPE) ** -0.5
    t_dim = dxo.shape[0]
    g = {}
    dm, g["norm_post"] = _resid_bwd(s["m"], dxo, p["norm_post"], 1.0, "mix_resid_bwd")
    dmerged = _mm(dm, p["w_out"], "nt", F32, "mix_out_dx")
    g["w_out"] = _mm(s["merged"], dm, "tn", F32, "mix_out_dw")
    dy_a, dy_b, dy_c, dz_g = _merge_bwd(s["z_g"], s["y_a"], s["y_b"], s["y_c"], dmerged, "merge_bwd")
    do_a = _mm(dy_a, p["wb_a"], "nt", F32, "branch_a_dx")
    do_b = _mm(dy_b, p["wb_b"], "nt", F32, "branch_b_dx")
    do_c = _mm(dy_c, p["wb_c"], "nt", BF16, "branch_c_dx")
    g["w_branch"] = jnp.concatenate([_mm(s["o_a"], dy_a, "tn", F32, "branch_a_dw"),
                                     _mm(s["o_b"], dy_b, "tn", F32, "branch_b_dw"),
                                     _mm(s["o_c"], dy_c, "tn", F32, "branch_c_dw")], axis=0)
    delta = _attn_delta(do_c, s["o_c"], "attn_delta")
    dq_cat, dk_cat, dv, *carried = _attn_bwd(s["q_cat"], s["k_cat"], s["v"], do_c, s["lse"].reshape(N_HEADS, 1, t_dim),
                                             delta.reshape(N_HEADS, 1, t_dim),
                                             "attn_bwd_comm" if carry else "attn_bwd", carry)
    dq_pre, dk_nope, dkr = _rope_bwd(dq_cat, dk_cat, cf, s1, s2, scale, "rope_bwd")
    dcq = _mm(dq_pre, p["w_uq"], "nt", F32, "mla_uq_dx")
    g["w_uq"] = _mm(s["cqn"], dq_pre, "tn", F32, "mla_uq_dw")
    dckv_k = _mm(dk_nope, p["w_uk"], "nt", F32, "mla_uk_dx")
    dckv_v = _mm(dv, p["w_uv"], "nt", F32, "mla_uv_dx")
    g["w_uk"] = _mm(s["ckvn"], dk_nope, "tn", F32, "mla_uk_dw")
    g["w_uv"] = _mm(s["ckvn"], dv, "tn", F32, "mla_uv_dw")
    dz_c, g["q_norm"], g["kv_norm"] = _mla_norm_bwd(s["z_c"], dcq, dckv_k, dckv_v, dkr, p["q_norm"], p["kv_norm"],
                                                    qr, kvr, "mla_norm_bwd")
    dy_conv, g["conv_ln_g"], g["conv_ln_b"] = _conv_ln_bwd(s["y_conv"], do_b, p["conv_ln_g"], p["conv_ln_b"], "conv_ln_bwd")
    dval, dgate, g["conv_w"], g["conv_b"] = _conv_bwd(dy_conv, s["a"], s["z_a"], p["conv_w"], gmw, "conv_bwd")
    da_u, da_v, g["gm_ln_g"], g["gm_ln_b"], g["gm_w_s"], g["gm_b_s"] = _gm_bwd(
        s["z_a"], do_a, p["gm_ln_g"], p["gm_ln_b"], p["gm_w_s"], p["gm_w_s_t"], p["gm_b_s"], gmw, "gm_bwd")
    dz_a = jnp.concatenate([da_u, da_v, dval, dgate], axis=1)
    dh_a = _mm(dz_a, p["w_a"], "nt", F32, "mix_in_a_dx")
    dh_c = _mm(dz_c, p["w_c"], "nt", F32, "mix_in_c_dx")
    dh_g = _mm(dz_g, p["w_g"], "nt", F32, "mix_in_g_dx")
    g["w_a"] = _mm(s["h"], dz_a, "tn", F32, "mix_in_a_dw")
    g["w_c"] = _mm(s["h"], dz_c, "tn", F32, "mix_in_c_dw")
    g["w_g"] = _mm(s["h"], dz_g, "tn", F32, "mix_in_g_dw")
    dx, g["norm_pre"] = _norm_bwd(s["x"], [dh_a, dh_c, dh_g], dxo, p["norm_pre"], "mix_norm_bwd")
    return dx, g, carried


def _ffn_params(full, small, k, l):
    row = lambda a: a[l][None, :]
    return dict(norm_pre=row(small[f"{k}_norm_pre"]), norm_post=row(small[f"{k}_norm_post"]),
                w_in=full[f"{k}_w_in"], w_out=full[f"{k}_w_out"])


def _mix_params(full, small, conv_w_full, dims, l):
    gmw, cw, qr, kvr, d = dims["gmw"], dims["cw"], dims["qr"], dims["kvr"], dims["d"]
    row = lambda a: a[l][None, :]
    w_in = full["mix_w_in"]
    a_end = 2 * gmw + 2 * cw
    c_end = a_end + qr + kvr + ROPE
    w_c = jnp.concatenate([w_in[:, a_end:c_end], jnp.zeros((d, 128 - ROPE), w_in.dtype)], axis=1)
    w_uq = full["mla_w_uq"].reshape(qr, N_HEADS, NOPE + ROPE)
    w_uq = jnp.concatenate([w_uq, jnp.zeros((qr, N_HEADS, HQ - NOPE - ROPE), w_uq.dtype)], axis=2).reshape(qr, N_HEADS * HQ)
    w_ukv = full["mla_w_ukv"].reshape(kvr, N_HEADS, NOPE + VDIM)
    w_b = full["mix_w_branch"]
    w_s = small["gm_w_s"][l]
    return dict(norm_pre=row(small["mix_norm_pre"]), norm_post=row(small["mix_norm_post"]),
                w_a=w_in[:, :a_end], w_c=w_c, w_g=w_in[:, c_end:],
                gm_ln_g=row(small["gm_ln_g"]), gm_ln_b=row(small["gm_ln_b"]), gm_w_s=w_s,
                gm_w_s_t=jnp.transpose(w_s, (0, 2, 1)), gm_b_s=small["gm_b_s"][l][:, :, None],
                conv_w=jnp.concatenate([conv_w_full, jnp.zeros((CONV_HALO - CONV_K, cw), F32)], axis=0),
                conv_b=row(small["conv_b"]), conv_ln_g=row(small["conv_ln_g"]), conv_ln_b=row(small["conv_ln_b"]),
                q_norm=row(small["mla_q_norm"]), kv_norm=row(small["mla_kv_norm"]),
                w_uq=w_uq, w_uk=w_ukv[:, :, :NOPE].reshape(kvr, N_HEADS * NOPE),
                w_uv=w_ukv[:, :, NOPE:].reshape(kvr, N_HEADS * VDIM),
                wb_a=w_b[:gmw], wb_b=w_b[gmw:gmw + cw], wb_c=w_b[gmw + cw:], w_out=full["mix_w_out"])


def _mix_big_grads(gm, dims):
    qr, kvr = dims["qr"], dims["kvr"]
    return {
        "mix_w_in": jnp.concatenate([gm["w_a"], gm["w_c"][:, :qr + kvr + ROPE], gm["w_g"]], axis=1),
        "mla_w_uq": gm["w_uq"].reshape(qr, N_HEADS, HQ)[:, :, :NOPE + ROPE].reshape(qr, N_HEADS * (NOPE + ROPE)),
        "mla_w_ukv": jnp.concatenate([gm["w_uk"].reshape(kvr, N_HEADS, NOPE), gm["w_uv"].reshape(kvr, N_HEADS, VDIM)],
                                     axis=2).reshape(kvr, N_HEADS * (NOPE + VDIM)),
        "mix_w_branch": gm["w_branch"], "mix_w_out": gm["w_out"], "conv_w": gm["conv_w"][:CONV_K],
    }


def _small_grads(g1, gm, g2):
    return {
        "ffn1_norm_pre": g1["norm_pre"][0], "ffn1_norm_post": g1["norm_post"][0],
        "ffn2_norm_pre": g2["norm_pre"][0], "ffn2_norm_post": g2["norm_post"][0],
        "mix_norm_pre": gm["norm_pre"][0], "mix_norm_post": gm["norm_post"][0],
        "gm_ln_g": gm["gm_ln_g"][0], "gm_ln_b": gm["gm_ln_b"][0], "gm_w_s": gm["gm_w_s"], "gm_b_s": gm["gm_b_s"][:, :, 0],
        "conv_b": gm["conv_b"][0], "conv_ln_g": gm["conv_ln_g"][0], "conv_ln_b": gm["conv_ln_b"][0],
        "mla_q_norm": gm["q_norm"][0], "mla_kv_norm": gm["kv_norm"][0],
    }


def _rope_tables(positions):
    inv_freq = ROPE_THETA ** (-jnp.arange(0, ROPE, 2, dtype=F32) / ROPE)
    ang = positions.astype(F32)[:, None] * inv_freq
    cos, sin = jnp.cos(ang), jnp.sin(ang)
    z = lambda w: jnp.zeros((positions.shape[0], w), F32)
    return (jnp.concatenate([cos, cos, z(64)], axis=1), jnp.concatenate([-sin, z(96)], axis=1),
            jnp.concatenate([z(32), sin, z(64)], axis=1))


def _pad_rows(flat, mult):
    n = flat.shape[0]
    pad = (-n) % mult
    return jnp.concatenate([flat, jnp.zeros((pad,), flat.dtype)]) if pad else flat


def kernel(x, positions, ffn1_norm_pre, ffn1_norm_post, ffn1_w_in, ffn1_w_out, mix_norm_pre, mix_norm_post, mix_w_in, gm_ln_g, gm_ln_b, gm_w_s, gm_b_s, conv_w, conv_b, conv_ln_g, conv_ln_b, mla_q_norm, mla_w_uq, mla_kv_norm, mla_w_ukv, mix_w_branch, mix_w_out, ffn2_norm_pre, ffn2_norm_post, ffn2_w_in, ffn2_w_out, loss_target, m_ffn1_norm_pre, m_ffn1_norm_post, m_ffn1_w_in, m_ffn1_w_out, m_mix_norm_pre, m_mix_norm_post, m_mix_w_in, m_gm_ln_g, m_gm_ln_b, m_gm_w_s, m_gm_b_s, m_conv_w, m_conv_b, m_conv_ln_g, m_conv_ln_b, m_mla_q_norm, m_mla_w_uq, m_mla_kv_norm, m_mla_w_ukv, m_mix_w_branch, m_mix_w_out, m_ffn2_norm_pre, m_ffn2_norm_post, m_ffn2_w_in, m_ffn2_w_out, v_ffn1_norm_pre, v_ffn1_norm_post, v_ffn1_w_in, v_ffn1_w_out, v_mix_norm_pre, v_mix_norm_post, v_mix_w_in, v_gm_ln_g, v_gm_ln_b, v_gm_w_s, v_gm_b_s, v_conv_w, v_conv_b, v_conv_ln_g, v_conv_ln_b, v_mla_q_norm, v_mla_w_uq, v_mla_kv_norm, v_mla_w_ukv, v_mix_w_branch, v_mix_w_out, v_ffn2_norm_pre, v_ffn2_norm_post, v_ffn2_w_in, v_ffn2_w_out):
    w = dict(zip(WEIGHTS, (ffn1_norm_pre, ffn1_norm_post, ffn1_w_in, ffn1_w_out, mix_norm_pre, mix_norm_post, mix_w_in, gm_ln_g, gm_ln_b, gm_w_s, gm_b_s, conv_w, conv_b, conv_ln_g, conv_ln_b, mla_q_norm, mla_w_uq, mla_kv_norm, mla_w_ukv, mix_w_branch, mix_w_out, ffn2_norm_pre, ffn2_norm_post, ffn2_w_in, ffn2_w_out)))
    mom_m = dict(zip(WEIGHTS, (m_ffn1_norm_pre, m_ffn1_norm_post, m_ffn1_w_in, m_ffn1_w_out, m_mix_norm_pre, m_mix_norm_post, m_mix_w_in, m_gm_ln_g, m_gm_ln_b, m_gm_w_s, m_gm_b_s, m_conv_w, m_conv_b, m_conv_ln_g, m_conv_ln_b, m_mla_q_norm, m_mla_w_uq, m_mla_kv_norm, m_mla_w_ukv, m_mix_w_branch, m_mix_w_out, m_ffn2_norm_pre, m_ffn2_norm_post, m_ffn2_w_in, m_ffn2_w_out)))
    mom_v = dict(zip(WEIGHTS, (v_ffn1_norm_pre, v_ffn1_norm_post, v_ffn1_w_in, v_ffn1_w_out, v_mix_norm_pre, v_mix_norm_post, v_mix_w_in, v_gm_ln_g, v_gm_ln_b, v_gm_w_s, v_gm_b_s, v_conv_w, v_conv_b, v_conv_ln_g, v_conv_ln_b, v_mla_q_norm, v_mla_w_uq, v_mla_kv_norm, v_mla_w_ukv, v_mix_w_branch, v_mix_w_out, v_ffn2_norm_pre, v_ffn2_norm_post, v_ffn2_w_in, v_ffn2_w_out)))
    n_layers = ffn1_norm_pre.shape[0]
    t_dim, d = x.shape[1], x.shape[2]
    dims = dict(d=d, gmw=gm_ln_g.shape[1], cw=conv_ln_g.shape[1], qr=mla_q_norm.shape[1], kvr=mla_kv_norm.shape[1])
    x0 = x.reshape(t_dim, d)
    target = loss_target.reshape(t_dim, d)
    rope_tabs = _rope_tables(positions.reshape(t_dim))

    conv_all = _unshard_conv(_comm([conv_w], False, "gather_conv")[0])
    gather = lambda names, l: ([w[k][l].astype(BF16) for k in names], False)
    full_of = lambda names, gathered: {k: _unshard(k, g) for k, g in zip(names, gathered)}
    last = n_layers - 1
    got_ffn1 = _comm(gather(FFN1_W, 0)[0], False, "gather_ffn1")
    got_mix = None
    params, saved = [], []
    xc = x0
    for l in range(n_layers):
        p1 = _ffn_params(full_of(FFN1_W, got_ffn1), w, "ffn1", l)
        xc, s1, got_a, got_b = _ffn_fwd(xc, p1, "ffn1", gather(MIX_W[:1], 0) if l == 0 else None,
                                        gather(MIX_W[1:], 0) if l == 0 else None)
        if l == 0:
            got_mix = got_a + got_b
        pm = _mix_params(full_of(MIX_W, got_mix), w, conv_all[l], dims, l)
        xc, sm, got = _mixer_fwd(xc, pm, rope_tabs, dims, gather(FFN2_W, l) if l == last else
                                 (gather(FFN2_W, l)[0] + gather(MIX_W, l + 1)[0], False))
        got_ffn2, got_mix = got[:len(FFN2_W)], got[len(FFN2_W):]
        p2 = _ffn_params(full_of(FFN2_W, got_ffn2), w, "ffn2", l)
        xc, s2, got_a, got_b = _ffn_fwd(xc, p2, "ffn2", gather(FFN1_W[:1], l + 1) if l < last else None,
                                        gather(FFN1_W[1:], l + 1) if l < last else None)
        got_ffn1 = got_a + got_b
        params.append((p1, pm, p2))
        saved.append((s1, sm, s2))
    dx, loss_part = _loss_fwd_bwd(xc, target, "loss")
    loss = lax.psum(loss_part[0, 0], ("x", "y", "c"))

    big_out = {k: [None] * n_layers for k in BIG + ("conv_w",)}
    small_parts = [None] * n_layers
    send = lambda names, grads: [_to_shards(k, grads[k]).astype(BF16) for k in names]

    def update(l, names, recv):
        for k, r in zip(names, recv):
            wl = w[k][l]
            r2 = (lambda a: a.reshape(-1, a.shape[-1]))
            outs = _adam(r2(wl), r.reshape(N_DEV, -1, wl.shape[-1]), r2(mom_m[k][l]), r2(mom_v[k][l]), f"adam_{k}")
            big_out[k][l] = [o.reshape(wl.shape) for o in outs]

    above = []
    for l in reversed(range(n_layers)):
        p1, pm, p2 = params[l]
        s1, sm, s2 = saved[l]
        dx, g2, _, _ = _ffn_bwd(dx, s2, p2, "ffn2")
        mine = send(FFN2_W, {"ffn2_w_in": g2["w_in"], "ffn2_w_out": g2["w_out"]})
        dx, gm, recv = _mixer_bwd(dx, sm, pm, rope_tabs, dims, (mine + above, True))
        update(l, FFN2_W, recv[:len(FFN2_W)])
        if above:
            update(l + 1, MIX_G + FFN1_W, recv[len(FFN2_W):])
        mix_out = send(MIX_G, _mix_big_grads(gm, dims))
        if l > 0:
            dx, g1, _, _ = _ffn_bwd(dx, s1, p1, "ffn1")
        else:
            dx, g1, got_a, got_b = _ffn_bwd(dx, s1, p1, "ffn1", (mix_out[1:], True), (mix_out[:1], True))
            update(0, MIX_G, got_b + got_a)
        ffn1_out = send(FFN1_W, {"ffn1_w_in": g1["w_in"], "ffn1_w_out": g1["w_out"]})
        above = mix_out + ffn1_out
        small_parts[l] = _small_grads(g1, gm, g2)
    update(0, FFN1_W, _comm(ffn1_out, True, "exchange_ffn1"))
    grad_x = dx.reshape(x.shape)

    flat = lambda tree: _pad_rows(jnp.concatenate([tree[k].reshape(-1) for k in SMALL]), 256 * 128).reshape(-1, 128)
    g_small = flat({k: jnp.stack([small_parts[l][k] for l in range(n_layers)]) for k in SMALL})
    parts = _comm([g_small], False, "gather_small_grads")[0]
    s_outs = _adam(flat(w), parts, flat(mom_m), flat(mom_v), "adam_small")
    small_out = {k: [] for k in SMALL}
    for o in s_outs:
        o = o.reshape(-1)
        off = 0
        for k in SMALL:
            n = int(np.prod(w[k].shape))
            small_out[k].append(o[off:off + n].reshape(w[k].shape))
            off += n

    def out(which, k):
        if k in SMALL:
            return small_out[k][which]
        return jnp.stack([big_out[k][l][which] for l in range(n_layers)])

    return (loss, grad_x, *[out(0, k) for k in WEIGHTS], *[out(1, k) for k in WEIGHTS],
            *[out(2, k) for k in WEIGHTS], *[out(3, k) for k in WEIGHTS])


def _unshard_conv(g):
    n_dev, n_layers, k, c = g.shape
    return jnp.transpose(g, (1, 2, 0, 3)).reshape(n_layers, k, n_dev * c)
```

```python
import functools

import numpy as np
import jax
import jax.numpy as jnp
from jax import lax
from jax.experimental import pallas as pl
from jax.experimental.pallas import tpu as pltpu

F32 = jnp.float32
BF16 = jnp.bfloat16

N_DEV = 8
N_HEADS = 16
NOPE = 128
ROPE = 64
VDIM = 128
HQ = 256
GROUPS = 4
CHUNK = 128
CONV_K = 31
CONV_HALO = 32
EPS = 1e-6
ROPE_THETA = 10000.0
ADAM_LR = 0.001
ADAM_B1 = 0.9
ADAM_B2 = 0.999
ADAM_EPS = 1e-08
ADAM_WD = 0.01
ADAM_STEP = 10
NEG = -1e30

V7X_VMEM_LIMIT = 56 * 1024 * 1024
MM_TM, MM_TN, MM_TK = 1024, 1024, 2048
ROW_TILE = 256
ROW_TILE_WIDE = 128
ATT_TILE = 1024
ATT_FWD_HEADS = 2
ATT_BWD_HEADS = 2
CONV_TILE = 256
GM_TILE = 256
ADAM_BLOCK_ELEMS = 128 * 1024

BIG = ("ffn1_w_in", "ffn1_w_out", "mix_w_in", "mla_w_uq", "mla_w_ukv", "mix_w_branch", "mix_w_out",
       "ffn2_w_in", "ffn2_w_out")
FFN1_W = ("ffn1_w_in", "ffn1_w_out")
MIX_W = ("mix_w_in", "mla_w_uq", "mla_w_ukv", "mix_w_branch", "mix_w_out")
FFN2_W = ("ffn2_w_in", "ffn2_w_out")
MIX_G = MIX_W + ("conv_w",)
COL_SHARDED = ("ffn1_w_in", "mix_w_in", "mla_w_uq", "mla_w_ukv", "ffn2_w_in", "conv_w")
SMALL = ("ffn1_norm_pre", "ffn1_norm_post", "mix_norm_pre", "mix_norm_post", "gm_ln_g", "gm_ln_b", "gm_w_s",
         "gm_b_s", "conv_b", "conv_ln_g", "conv_ln_b", "mla_q_norm", "mla_kv_norm", "ffn2_norm_pre",
         "ffn2_norm_post")
WEIGHTS = ("ffn1_norm_pre", "ffn1_norm_post", "ffn1_w_in", "ffn1_w_out", "mix_norm_pre", "mix_norm_post",
           "mix_w_in", "gm_ln_g", "gm_ln_b", "gm_w_s", "gm_b_s", "conv_w", "conv_b", "conv_ln_g", "conv_ln_b",
           "mla_q_norm", "mla_w_uq", "mla_kv_norm", "mla_w_ukv", "mix_w_branch", "mix_w_out", "ffn2_norm_pre",
           "ffn2_norm_post", "ffn2_w_in", "ffn2_w_out")


def _cparams(sem):
    return pltpu.CompilerParams(dimension_semantics=sem, vmem_limit_bytes=V7X_VMEM_LIMIT)


def _pick(dim, pref):
    if dim <= pref:
        return dim
    t = pref
    while t >= 128:
        if dim % t == 0:
            return t
        t -= 128
    return dim


def _sigmoid(x):
    return 1.0 / (1.0 + jnp.exp(-x))


_GELU_C = 0.7978845608028654


def _gelu(x):
    t = jnp.tanh(_GELU_C * (x + 0.044715 * x * x * x))
    return 0.5 * x * (1.0 + t)


def _gelu_grad(x):
    t = jnp.tanh(_GELU_C * (x + 0.044715 * x * x * x))
    return 0.5 * (1.0 + t) + 0.5 * x * (1.0 - t * t) * _GELU_C * (1.0 + 3.0 * 0.044715 * x * x)


def _rms(x):
    r = lax.rsqrt(jnp.mean(x * x, axis=-1, keepdims=True) + EPS)
    return x * r, r


def _rms_bwd(xn, r, t):
    return r * (t - xn * jnp.mean(t * xn, axis=-1, keepdims=True))


def _colsum(x):
    return jnp.sum(x, axis=0, keepdims=True)


_DIMS = {"nn": (((1,), (0,)), ((), ())), "nt": (((1,), (1,)), ((), ())), "tn": (((0,), (0,)), ((), ()))}


def _mm(a, b, mode, out_dtype, name, carry=None, shard_out=None):
    if mode == "tn":
        k_dim, m_dim = a.shape
    else:
        m_dim, k_dim = a.shape
    n_dim = b.shape[0] if mode == "nt" else b.shape[1]
    tm = _pick(m_dim // N_DEV if shard_out == "row" else m_dim, MM_TM)
    tn = _pick(n_dim // N_DEV if shard_out == "col" else n_dim, MM_TN)
    tk = _pick(k_dim, MM_TK)
    grid = (m_dim // tm, n_dim // tn, k_dim // tk)
    nk = grid[2]
    dims = _DIMS[mode]
    c_arrays, c_exchange = carry if carry else ((), False)
    nc = len(c_arrays)

    def at_step(first):
        conds = [pl.program_id(q) == (0 if first else grid[q] - 1) for q in range(3)]
        return jnp.logical_and(jnp.logical_and(conds[0], conds[1]), conds[2])

    def body(a_ref, b_ref, *rest):
        c_ins, o_ref, c_outs = rest[:nc], rest[nc], rest[nc + 1:2 * nc + 1]
        scratch = rest[2 * nc + 1:]
        sems = scratch[1:] if nk > 1 else scratch
        if nc:
            @pl.when(at_step(True))
            def _():
                _comm_start(c_ins, c_outs, *sems, c_exchange)

        prod = lax.dot_general(a_ref[...], b_ref[...], dims, preferred_element_type=F32)
        if nk == 1:
            o_ref[...] = prod.astype(o_ref.dtype)
        else:
            acc_ref, k = scratch[0], pl.program_id(2)

            @pl.when(k == 0)
            def _():
                acc_ref[...] = prod

            @pl.when(jnp.logical_and(k > 0, k < nk - 1))
            def _():
                acc_ref[...] += prod

            @pl.when(k == nk - 1)
            def _():
                o_ref[...] = (acc_ref[...] + prod).astype(o_ref.dtype)

        if nc:
            @pl.when(at_step(False))
            def _():
                _comm_wait(c_ins, c_outs, *sems, c_exchange)

    if mode == "tn":
        a_spec = pl.BlockSpec((tk, tm), lambda i, j, k: (k, i))
    else:
        a_spec = pl.BlockSpec((tm, tk), lambda i, j, k: (i, k))
    if mode == "nt":
        b_spec = pl.BlockSpec((tn, tk), lambda i, j, k: (j, k))
    else:
        b_spec = pl.BlockSpec((tk, tn), lambda i, j, k: (k, j))
    any_spec = pl.BlockSpec(memory_space=pl.ANY)
    if shard_out == "col":
        per = n_dim // N_DEV // tn
        o_shape, o_spec = (N_DEV, m_dim, n_dim // N_DEV), pl.BlockSpec((None, tm, tn), lambda i, j, k: (j // per, i, j % per))
    elif shard_out == "row":
        per = m_dim // N_DEV // tm
        o_shape, o_spec = (N_DEV, m_dim // N_DEV, n_dim), pl.BlockSpec((None, tm, tn), lambda i, j, k: (i // per, i % per, j))
    else:
        o_shape, o_spec = (m_dim, n_dim), pl.BlockSpec((tm, tn), lambda i, j, k: (i, j))
    outs = pl.pallas_call(
        body, out_shape=[jax.ShapeDtypeStruct(o_shape, out_dtype)] + _comm_out_shape(c_arrays, c_exchange),
        grid=grid, in_specs=[a_spec, b_spec] + [any_spec] * nc,
        out_specs=[o_spec] + [any_spec] * nc,
        scratch_shapes=([pltpu.VMEM((tm, tn), F32)] if nk > 1 else []) + _comm_sems(nc),
        compiler_params=_cparams(("arbitrary",) * 3 if nc else ("parallel", "parallel", "arbitrary")),
        name=name)(a, b, *c_arrays)
    return (outs[0], outs[1:]) if nc else outs[0]


def _rowwise(fn, name, n_rows, tr, row_ins, full_ins, row_outs, acc_outs=()):
    tr = min(tr, n_rows)
    n_ri, n_fi, n_ro = len(row_ins), len(full_ins), len(row_outs)

    def body(*refs):
        i = pl.program_id(0)
        ri, fi = refs[:n_ri], refs[n_ri:n_ri + n_fi]
        ro, ao = refs[n_ri + n_fi:n_ri + n_fi + n_ro], refs[n_ri + n_fi + n_ro:]

        @pl.when(i == 0)
        def _():
            for r in ao:
                r[...] = jnp.zeros_like(r)

        fn(i, ri, fi, ro, ao)

    in_specs = [pl.BlockSpec((tr, w), functools.partial(lambda c, i: (i, c), cb)) for _, w, cb in row_ins]
    in_specs += [pl.BlockSpec(a.shape, functools.partial(lambda nd, i: (0,) * nd, a.ndim)) for a in full_ins]
    out_specs = [pl.BlockSpec((tr, w), lambda i: (i, 0)) for w, _ in row_outs]
    out_specs += [pl.BlockSpec(s, functools.partial(lambda nd, i: (0,) * nd, len(s))) for s, _ in acc_outs]
    out_shape = [jax.ShapeDtypeStruct((n_rows, w), d) for w, d in row_outs]
    out_shape += [jax.ShapeDtypeStruct(s, d) for s, d in acc_outs]
    return pl.pallas_call(
        body, out_shape=out_shape, grid=(n_rows // tr,), in_specs=in_specs, out_specs=out_specs,
        compiler_params=_cparams(("arbitrary",)), name=name)(*[a for a, _, _ in row_ins], *full_ins)


def _norm_fwd(x, g, name):
    t_dim, d = x.shape

    def fn(i, ri, fi, ro, ao):
        xn, _ = _rms(ri[0][...])
        ro[0][...] = (xn * fi[0][...]).astype(BF16)

    return _rowwise(fn, name, t_dim, ROW_TILE, [(x, d, 0)], [g], [(d, BF16)])[0]


def _resid_fwd(x, y, g, coef, name):
    t_dim, d = x.shape

    def fn(i, ri, fi, ro, ao):
        yn, _ = _rms(ri[1][...])
        ro[0][...] = ri[0][...] + coef * (yn * fi[0][...])

    return _rowwise(fn, name, t_dim, ROW_TILE, [(x, d, 0), (y, d, 0)], [g], [(d, F32)])[0]


def _resid_bwd(y, dxo, g, coef, name):
    t_dim, d = y.shape

    def fn(i, ri, fi, ro, ao):
        yn, r = _rms(ri[0][...])
        dyn = coef * ri[1][...]
        ao[0][...] += _colsum(dyn * yn)
        ro[0][...] = _rms_bwd(yn, r, dyn * fi[0][...]).astype(BF16)

    return _rowwise(fn, name, t_dim, ROW_TILE, [(y, d, 0), (dxo, d, 0)], [g], [(d, BF16)], [((1, d), F32)])


def _norm_bwd(x, dhs, dxo, g, name):
    t_dim, d = x.shape
    n = len(dhs)

    def fn(i, ri, fi, ro, ao):
        xn, r = _rms(ri[0][...])
        dh = ri[2][...]
        for q in range(1, n):
            dh = dh + ri[2 + q][...]
        ao[0][...] += _colsum(dh * xn)
        ro[0][...] = ri[1][...] + _rms_bwd(xn, r, dh * fi[0][...])

    return _rowwise(fn, name, t_dim, ROW_TILE, [(x, d, 0), (dxo, d, 0)] + [(a, d, 0) for a in dhs], [g],
                    [(d, F32)], [((1, d), F32)])


def _swiglu_fwd(gu, name):
    t_dim, f2 = gu.shape
    f = f2 // 2

    def fn(i, ri, fi, ro, ao):
        gate, up = ri[0][:, :f].astype(F32), ri[0][:, f:].astype(F32)
        ro[0][...] = (gate * _sigmoid(gate) * up).astype(BF16)

    return _rowwise(fn, name, t_dim, ROW_TILE_WIDE, [(gu, f2, 0)], [], [(f, BF16)])[0]


def _swiglu_bwd(gu, dact, name):
    t_dim, f2 = gu.shape
    f = f2 // 2

    def fn(i, ri, fi, ro, ao):
        gate, up = ri[0][:, :f].astype(F32), ri[0][:, f:].astype(F32)
        da = ri[1][...].astype(F32)
        s = _sigmoid(gate)
        ro[0][:, :f] = (da * up * (s * (1.0 + gate * (1.0 - s)))).astype(BF16)
        ro[0][:, f:] = (da * (gate * s)).astype(BF16)

    return _rowwise(fn, name, t_dim, ROW_TILE_WIDE, [(gu, f2, 0), (dact, f, 0)], [], [(f2, BF16)])[0]


def _loss_fwd_bwd(y, target, name):
    t_dim, d = y.shape

    def fn(i, ri, fi, ro, ao):
        err = ri[0][...] - ri[1][...]
        ao[0][...] += _colsum(jnp.sum(err * err, axis=1, keepdims=True)) * (0.5 / d)
        ro[0][...] = err * (1.0 / d)

    return _rowwise(fn, name, t_dim, ROW_TILE, [(y, d, 0), (target, d, 0)], [], [(d, F32)], [((1, 1), F32)])


def _ln_stats(v):
    mu = jnp.mean(v, axis=-1, keepdims=True)
    xc = v - mu
    rstd = lax.rsqrt(jnp.mean(xc * xc, axis=-1, keepdims=True) + EPS)
    return xc * rstd, rstd


def _tril_mask(upper=False):
    row = lax.broadcasted_iota(jnp.int32, (CHUNK, CHUNK), 0)
    col = lax.broadcasted_iota(jnp.int32, (CHUNK, CHUNK), 1)
    return row <= col if upper else row >= col


def _gm_fwd(z_a, ln_g, ln_b, w_s, b_s, gmw, name):
    t_dim = z_a.shape[0]
    gw = gmw // GROUPS
    tr = min(GM_TILE, t_dim)

    def fn(i, ri, fi, ro, ao):
        lng, lnb, ws_ref, bs_ref = fi
        u = _gelu(ri[0][...].astype(F32))
        vhat, _ = _ln_stats(_gelu(ri[1][...].astype(F32)))
        vn = (vhat * lng[...] + lnb[...]).astype(BF16)
        mask = _tril_mask()
        for g in range(GROUPS):
            wg = jnp.where(mask, ws_ref[g], 0.0).astype(BF16)
            for c in range(tr // CHUNK):
                rows, cols = slice(c * CHUNK, (c + 1) * CHUNK), slice(g * gw, (g + 1) * gw)
                s = jnp.dot(wg, vn[rows, cols], preferred_element_type=F32) + bs_ref[g]
                ro[0][rows, cols] = (u[rows, cols] * s).astype(BF16)

    return _rowwise(fn, name, t_dim, tr, [(z_a, gmw, 0), (z_a, gmw, 1)], [ln_g, ln_b, w_s, b_s], [(gmw, BF16)])[0]


def _gm_bwd(z_a, do_a, ln_g, ln_b, w_s, w_s_t, b_s, gmw, name):
    t_dim = z_a.shape[0]
    gw = gmw // GROUPS
    tr = min(GM_TILE, t_dim)

    def fn(i, ri, fi, ro, ao):
        lng, lnb, ws_ref, wst_ref, bs_ref = fi
        d_lng, d_lnb, d_ws, d_bs = ao
        a_u, a_v, do = ri[0][...].astype(F32), ri[1][...].astype(F32), ri[2][...]
        u = _gelu(a_u)
        vhat, rstd = _ln_stats(_gelu(a_v))
        vn = (vhat * lng[...] + lnb[...]).astype(BF16)
        mask = _tril_mask()
        wgs = [jnp.where(mask, ws_ref[g], 0.0).astype(BF16) for g in range(GROUPS)]
        wgts = [jnp.where(_tril_mask(upper=True), wst_ref[g], 0.0).astype(BF16) for g in range(GROUPS)]
        for c in range(tr // CHUNK):
            rows = slice(c * CHUNK, (c + 1) * CHUNK)
            dvn_parts = []
            for g in range(GROUPS):
                cols = slice(g * gw, (g + 1) * gw)
                vn_blk = vn[rows, cols]
                s = jnp.dot(wgs[g], vn_blk, preferred_element_type=F32) + bs_ref[g]
                ro[0][rows, cols] = (do[rows, cols] * s * _gelu_grad(a_u[rows, cols])).astype(BF16)
                ds = do[rows, cols] * u[rows, cols]
                d_bs[g] += jnp.sum(ds, axis=1, keepdims=True)
                dsb = ds.astype(BF16)
                dw = lax.dot_general(dsb, vn_blk, _DIMS["nt"], preferred_element_type=F32)
                d_ws[g] += jnp.where(mask, dw, 0.0)
                dvn_parts.append(jnp.dot(wgts[g], dsb, preferred_element_type=F32))
            dvn = jnp.concatenate(dvn_parts, axis=1)
            vh, rs = vhat[rows], rstd[rows]
            d_lng[...] += _colsum(dvn * vh)
            d_lnb[...] += _colsum(dvn)
            dvh = dvn * lng[...]
            dv = rs * (dvh - jnp.mean(dvh, axis=-1, keepdims=True) - vh * jnp.mean(dvh * vh, axis=-1, keepdims=True))
            ro[0][rows, gmw:] = (dv * _gelu_grad(a_v[rows])).astype(BF16)

    return _rowwise(fn, name, t_dim, tr, [(z_a, gmw, 0), (z_a, gmw, 1), (do_a, gmw, 0)],
                    [ln_g, ln_b, w_s, w_s_t, b_s], [(2 * gmw, BF16)],
                    [((1, gmw), F32), ((1, gmw), F32), ((GROUPS, CHUNK, CHUNK), F32), ((GROUPS, CHUNK, 1), F32)])


def _glu_fwd(z_a, gmw, cw, name):
    t_dim = z_a.shape[0]
    cb = (2 * gmw) // cw

    def fn(i, ri, fi, ro, ao):
        ro[0][...] = ri[0][...].astype(F32) * _sigmoid(ri[1][...].astype(F32))

    return _rowwise(fn, name, t_dim, ROW_TILE, [(z_a, cw, cb), (z_a, cw, cb + 1)], [], [(cw, F32)])[0]


def _by_sublane_shift(offsets):
    groups = {}
    for o in offsets:
        groups.setdefault(o % 8, []).append(o)
    return sorted(groups.items())


def _conv_fwd(a, w, b, ln_g, ln_b, name):
    t_dim, c_dim = a.shape
    tr = min(CONV_TILE, t_dim)
    hb = tr // CONV_HALO

    def body(cur_ref, prev_ref, w_ref, b_ref, g_ref, be_ref, y_ref, o_ref, buf):
        i = pl.program_id(0)
        buf[0:CONV_HALO, :] = jnp.where(i > 0, prev_ref[...], 0.0)
        buf[CONV_HALO:, :] = cur_ref[...]
        for cs in range(c_dim // 128):
            lanes = pl.ds(cs * 128, 128)
            acc = jnp.zeros((tr, 128), F32)
            for k in range(CONV_K):
                acc = acc + w_ref[k:k + 1, lanes] * buf[pl.ds(k + 2, tr), lanes]
            y_ref[:, lanes] = acc + b_ref[:, lanes]
        n_hat, _ = _ln_stats(y_ref[...])
        n = n_hat * g_ref[...] + be_ref[...]
        o_ref[...] = (n * _sigmoid(n)).astype(BF16)

    full = lambda arr: pl.BlockSpec(arr.shape, lambda i: (0, 0))
    return pl.pallas_call(
        body, out_shape=[jax.ShapeDtypeStruct((t_dim, c_dim), F32), jax.ShapeDtypeStruct((t_dim, c_dim), BF16)],
        grid=(t_dim // tr,),
        in_specs=[pl.BlockSpec((tr, c_dim), lambda i: (i, 0)),
                  pl.BlockSpec((CONV_HALO, c_dim), lambda i: (jnp.maximum(i * hb - 1, 0), 0)),
                  full(w), full(b), full(ln_g), full(ln_b)],
        out_specs=[pl.BlockSpec((tr, c_dim), lambda i: (i, 0)), pl.BlockSpec((tr, c_dim), lambda i: (i, 0))],
        scratch_shapes=[pltpu.VMEM((tr + CONV_HALO, c_dim), F32)],
        compiler_params=_cparams(("arbitrary",)), name=name)(a, a, w, b, ln_g, ln_b)


def _conv_ln_bwd(y, do_b, ln_g, ln_b, name):
    t_dim, c_dim = y.shape

    def fn(i, ri, fi, ro, ao):
        n_hat, rstd = _ln_stats(ri[0][...])
        n = n_hat * fi[0][...] + fi[1][...]
        s = _sigmoid(n)
        dn = ri[1][...] * (s * (1.0 + n * (1.0 - s)))
        ao[0][...] += _colsum(dn * n_hat)
        ao[1][...] += _colsum(dn)
        dnh = dn * fi[0][...]
        ro[0][...] = rstd * (dnh - jnp.mean(dnh, axis=-1, keepdims=True)
                             - n_hat * jnp.mean(dnh * n_hat, axis=-1, keepdims=True))

    return _rowwise(fn, name, t_dim, ROW_TILE, [(y, c_dim, 0), (do_b, c_dim, 0)], [ln_g, ln_b], [(c_dim, F32)],
                    [((1, c_dim), F32), ((1, c_dim), F32)])


def _conv_bwd(dy, a, z_a, w, gmw, name):
    t_dim, c_dim = a.shape
    tr = min(CONV_TILE, t_dim)
    hb = tr // CONV_HALO
    n_halo = t_dim // CONV_HALO
    nb = t_dim // tr
    cb = (2 * gmw) // c_dim

    def body(dy_ref, dyn_ref, a_ref, ap_ref, val_ref, gate_ref, w_ref, dvg_ref, dw_ref, db_ref, dbuf, abuf, da_buf):
        i = pl.program_id(0)

        @pl.when(i == 0)
        def _():
            dw_ref[...] = jnp.zeros_like(dw_ref)
            db_ref[...] = jnp.zeros_like(db_ref)

        dbuf[0:tr, :] = dy_ref[...]
        dbuf[tr:, :] = jnp.where(i < nb - 1, dyn_ref[...], 0.0)
        abuf[0:CONV_HALO, :] = jnp.where(i > 0, ap_ref[...], 0.0)
        abuf[CONV_HALO:, :] = a_ref[...]
        db_ref[...] += _colsum(dy_ref[...])
        for cs in range(c_dim // 128):
            lanes = pl.ds(cs * 128, 128)
            dyc = dy_ref[:, lanes]
            acc = jnp.zeros((tr, 128), F32)
            for k in range(CONV_K):
                acc = acc + w_ref[k:k + 1, lanes] * dbuf[pl.ds(CONV_K - 1 - k, tr), lanes]
            for r, offs in _by_sublane_shift([k + 2 for k in range(CONV_K)]):
                shifted = abuf[pl.ds(r, tr + CONV_HALO - (8 if r else 0)), lanes]
                for o in offs:
                    dw_ref[o - 2:o - 1, lanes] += _colsum(dyc * shifted[o - r:o - r + tr])
            da_buf[:, lanes] = acc
        da = da_buf[...]
        s = _sigmoid(gate_ref[...].astype(F32))
        dvg_ref[:, :c_dim] = (da * s).astype(BF16)
        dvg_ref[:, c_dim:] = (da * val_ref[...].astype(F32) * s * (1.0 - s)).astype(BF16)

    row = lambda cblk: pl.BlockSpec((tr, c_dim), functools.partial(lambda c, i: (i, c), cblk))
    return pl.pallas_call(
        body,
        out_shape=[jax.ShapeDtypeStruct((t_dim, 2 * c_dim), BF16),
                   jax.ShapeDtypeStruct((CONV_HALO, c_dim), F32), jax.ShapeDtypeStruct((1, c_dim), F32)],
        grid=(nb,),
        in_specs=[row(0), pl.BlockSpec((CONV_HALO, c_dim), lambda i: (jnp.minimum((i + 1) * hb, n_halo - 1), 0)),
                  row(0), pl.BlockSpec((CONV_HALO, c_dim), lambda i: (jnp.maximum(i * hb - 1, 0), 0)),
                  row(cb), row(cb + 1), pl.BlockSpec(w.shape, lambda i: (0, 0))],
        out_specs=[pl.BlockSpec((tr, 2 * c_dim), lambda i: (i, 0)), pl.BlockSpec((CONV_HALO, c_dim), lambda i: (0, 0)),
                   pl.BlockSpec((1, c_dim), lambda i: (0, 0))],
        scratch_shapes=[pltpu.VMEM((tr + CONV_HALO, c_dim), F32), pltpu.VMEM((tr + CONV_HALO, c_dim), F32),
                        pltpu.VMEM((tr, c_dim), F32)],
        compiler_params=_cparams(("arbitrary",)), name=name)(dy, dy, a, a, z_a, z_a, w)


def _mla_norm_fwd(z_c, q_g, kv_g, qr, kvr, name):
    t_dim, cwid = z_c.shape

    def fn(i, ri, fi, ro, ao):
        cq, _ = _rms(ri[0][:, :qr])
        ckv, _ = _rms(ri[0][:, qr:qr + kvr])
        ro[0][...] = (cq * fi[0][...]).astype(BF16)
        ro[1][...] = (ckv * fi[1][...]).astype(BF16)

    return _rowwise(fn, name, t_dim, ROW_TILE, [(z_c, cwid, 0)], [q_g, kv_g], [(qr, BF16), (kvr, BF16)])


def _rope(t, cf, s1, s2):
    return t * cf + pltpu.roll(t, 96, 1) * s1 + pltpu.roll(t, 32, 1) * s2


def _rope_t(g, cf, s1, s2):
    return g * cf + pltpu.roll(g * s1, 32, 1) + pltpu.roll(g * s2, 96, 1)


def _rope_fwd(q_pre, k_nope, z_c, cf, s1, s2, rope_blk, scale, name):
    t_dim = q_pre.shape[0]

    def fn(i, ri, fi, ro, ao):
        c, a, b = ri[3][...], ri[4][...], ri[5][...]
        kt = _rope(ri[2][...], c, a, b).astype(BF16)
        for h in range(N_HEADS):
            ro[0][:, h * HQ:h * HQ + NOPE] = (ri[0][:, h * HQ:h * HQ + NOPE].astype(F32) * scale).astype(BF16)
            ro[0][:, h * HQ + NOPE:(h + 1) * HQ] = (
                _rope(ri[0][:, h * HQ + NOPE:(h + 1) * HQ].astype(F32), c, a, b) * scale).astype(BF16)
            ro[1][:, h * HQ:h * HQ + NOPE] = ri[1][:, h * NOPE:(h + 1) * NOPE].astype(BF16)
            ro[1][:, h * HQ + NOPE:(h + 1) * HQ] = kt

    return _rowwise(fn, name, t_dim, ROW_TILE_WIDE,
                    [(q_pre, N_HEADS * HQ, 0), (k_nope, N_HEADS * NOPE, 0), (z_c, 128, rope_blk),
                     (cf, 128, 0), (s1, 128, 0), (s2, 128, 0)], [],
                    [(N_HEADS * HQ, BF16), (N_HEADS * HQ, BF16)])


def _rope_bwd(dq_cat, dk_cat, cf, s1, s2, scale, name):
    t_dim = dq_cat.shape[0]

    def fn(i, ri, fi, ro, ao):
        c, a, b = ri[2][...], ri[3][...], ri[4][...]
        dkt = jnp.zeros((ri[0].shape[0], 128), F32)
        for h in range(N_HEADS):
            ro[0][:, h * HQ:h * HQ + NOPE] = (ri[0][:, h * HQ:h * HQ + NOPE] * scale).astype(BF16)
            ro[0][:, h * HQ + NOPE:(h + 1) * HQ] = _rope_t(ri[0][:, h * HQ + NOPE:(h + 1) * HQ] * scale, c, a, b).astype(BF16)
            ro[1][:, h * NOPE:(h + 1) * NOPE] = ri[1][:, h * HQ:h * HQ + NOPE].astype(BF16)
            dkt = dkt + ri[1][:, h * HQ + NOPE:(h + 1) * HQ].astype(F32)
        ro[2][...] = _rope_t(dkt, c, a, b)

    return _rowwise(fn, name, t_dim, ROW_TILE_WIDE,
                    [(dq_cat, N_HEADS * HQ, 0), (dk_cat, N_HEADS * HQ, 0), (cf, 128, 0), (s1, 128, 0), (s2, 128, 0)],
                    [], [(N_HEADS * HQ, BF16), (N_HEADS * NOPE, BF16), (128, F32)])


def _mla_norm_bwd(z_c, dcq, dckv_k, dckv_v, dkr, q_g, kv_g, qr, kvr, name):
    t_dim, cwid = z_c.shape

    def fn(i, ri, fi, ro, ao):
        cq, rq = _rms(ri[0][:, :qr])
        ckv, rkv = _rms(ri[0][:, qr:qr + kvr])
        dq = ri[1][...]
        dkv = ri[2][...] + ri[3][...]
        ao[0][...] += _colsum(dq * cq)
        ao[1][...] += _colsum(dkv * ckv)
        ro[0][:, :qr] = _rms_bwd(cq, rq, dq * fi[0][...]).astype(BF16)
        ro[0][:, qr:qr + kvr] = _rms_bwd(ckv, rkv, dkv * fi[1][...]).astype(BF16)
        ro[0][:, qr + kvr:] = ri[4][...].astype(BF16)

    return _rowwise(fn, name, t_dim, ROW_TILE,
                    [(z_c, cwid, 0), (dcq, qr, 0), (dckv_k, kvr, 0), (dckv_v, kvr, 0), (dkr, 128, 0)], [q_g, kv_g],
                    [(cwid, BF16)], [((1, qr), F32), ((1, kvr), F32)])


def _causal_pairs(n, by_key):
    if by_key:
        pairs = [(i, j) for j in range(n) for i in range(j, n)]
    else:
        pairs = [(i, j) for i in range(n) for j in range(i + 1)]
    return (np.array([p[0] for p in pairs], np.int32), np.array([p[1] for p in pairs], np.int32))


def _attn_fwd(q, k, v, name, carry=None):
    t_dim = q.shape[0]
    tq = min(ATT_TILE, t_dim)
    nh = ATT_FWD_HEADS
    qi, kj = _causal_pairs(t_dim // tq, by_key=False)
    n_steps = len(qi)
    c_arrays, c_exchange = carry if carry else ((), False)
    nc = len(c_arrays)

    def body(qi_ref, kj_ref, q_ref, k_ref, v_ref, *rest):
        c_ins, (o_ref, lse_ref), c_outs = rest[:nc], rest[nc:nc + 2], rest[nc + 2:2 * nc + 2]
        m_sc, l_sc, acc_sc = rest[2 * nc + 2:2 * nc + 5]
        sems = rest[2 * nc + 5:]
        s_id = pl.program_id(1)
        i, j = qi_ref[s_id], kj_ref[s_id]
        if nc:
            @pl.when(jnp.logical_and(pl.program_id(0) == 0, s_id == 0))
            def _():
                _comm_start(c_ins, c_outs, *sems, c_exchange)

        @pl.when(j == 0)
        def _():
            m_sc[...] = jnp.full_like(m_sc, NEG)
            l_sc[...] = jnp.zeros_like(l_sc)
            acc_sc[...] = jnp.zeros_like(acc_sc)

        def step(masked):
            for hh in range(nh):
                s = lax.dot_general(q_ref[:, hh * HQ:(hh + 1) * HQ], k_ref[:, hh * HQ:(hh + 1) * HQ], _DIMS["nt"],
                                    preferred_element_type=F32)
                if masked:
                    row = lax.broadcasted_iota(jnp.int32, (tq, tq), 0)
                    col = lax.broadcasted_iota(jnp.int32, (tq, tq), 1)
                    s = jnp.where(col <= row, s, NEG)
                m_prev = m_sc[hh]
                m_new = jnp.maximum(m_prev, jnp.max(s, axis=1, keepdims=True))
                alpha = jnp.exp(m_prev - m_new)
                p = jnp.exp(s - m_new)
                l_sc[hh] = alpha * l_sc[hh] + jnp.sum(p, axis=1, keepdims=True)
                acc_sc[hh] = alpha * acc_sc[hh] + jnp.dot(p.astype(BF16), v_ref[:, hh * VDIM:(hh + 1) * VDIM],
                                                          preferred_element_type=F32)
                m_sc[hh] = m_new

        @pl.when(j < i)
        def _():
            step(False)

        @pl.when(j == i)
        def _():
            step(True)
            for hh in range(nh):
                o_ref[:, hh * VDIM:(hh + 1) * VDIM] = (acc_sc[hh] / l_sc[hh]).astype(BF16)
                lse_ref[hh] = m_sc[hh] + jnp.log(l_sc[hh])

        if nc:
            @pl.when(jnp.logical_and(pl.program_id(0) == N_HEADS // nh - 1, s_id == n_steps - 1))
            def _():
                _comm_wait(c_ins, c_outs, *sems, c_exchange)

    any_spec = pl.BlockSpec(memory_space=pl.ANY)
    grid_spec = pltpu.PrefetchScalarGridSpec(
        num_scalar_prefetch=2, grid=(N_HEADS // nh, n_steps),
        in_specs=[pl.BlockSpec((tq, nh * HQ), lambda h, s, qi, kj: (qi[s], h)),
                  pl.BlockSpec((tq, nh * HQ), lambda h, s, qi, kj: (kj[s], h)),
                  pl.BlockSpec((tq, nh * VDIM), lambda h, s, qi, kj: (kj[s], h))] + [any_spec] * nc,
        out_specs=[pl.BlockSpec((tq, nh * VDIM), lambda h, s, qi, kj: (qi[s], h)),
                   pl.BlockSpec((nh, tq, 1), lambda h, s, qi, kj: (h, qi[s], 0))] + [any_spec] * nc,
        scratch_shapes=[pltpu.VMEM((nh, tq, 1), F32), pltpu.VMEM((nh, tq, 1), F32), pltpu.VMEM((nh, tq, VDIM), F32)]
        + _comm_sems(nc))
    return pl.pallas_call(
        body, grid_spec=grid_spec,
        out_shape=[jax.ShapeDtypeStruct((t_dim, N_HEADS * VDIM), BF16), jax.ShapeDtypeStruct((N_HEADS, t_dim, 1), F32)]
        + _comm_out_shape(c_arrays, c_exchange),
        compiler_params=_cparams(("arbitrary", "arbitrary")), name=name)(
            jnp.asarray(qi), jnp.asarray(kj), q, k, v, *c_arrays)


def _attn_delta(do, o, name):
    t_dim = do.shape[0]
    tr = min(ATT_TILE, t_dim)

    def body(do_ref, o_ref, d_ref):
        d_ref[0] = jnp.sum(do_ref[...].astype(F32) * o_ref[...].astype(F32), axis=1, keepdims=True)

    return pl.pallas_call(
        body, out_shape=jax.ShapeDtypeStruct((N_HEADS, t_dim, 1), F32), grid=(N_HEADS, t_dim // tr),
        in_specs=[pl.BlockSpec((tr, VDIM), lambda h, i: (i, h)), pl.BlockSpec((tr, VDIM), lambda h, i: (i, h))],
        out_specs=pl.BlockSpec((1, tr, 1), lambda h, i: (h, i, 0)),
        compiler_params=_cparams(("parallel", "parallel")), name=name)(do, o)


def _attn_bwd(q, k, v, do, lse_row, delta_row, name, carry=None):
    t_dim = q.shape[0]
    tq = min(ATT_TILE, t_dim)
    nq = t_dim // tq
    nh = ATT_BWD_HEADS
    qi, kj = _causal_pairs(nq, by_key=True)
    n_steps = len(qi)
    c_arrays, c_exchange = carry if carry else ((), False)
    nc = len(c_arrays)

    def body(qi_ref, kj_ref, q_ref, k_ref, v_ref, do_ref, lse_ref, dl_ref, *rest):
        c_ins, (dq_ref, dk_ref, dv_ref), c_outs = rest[:nc], rest[nc:nc + 3], rest[nc + 3:2 * nc + 3]
        dk_sc, dv_sc = rest[2 * nc + 3:2 * nc + 5]
        sems = rest[2 * nc + 5:]
        s_id = pl.program_id(1)
        i, j = qi_ref[s_id], kj_ref[s_id]
        if nc:
            @pl.when(jnp.logical_and(pl.program_id(0) == 0, s_id == 0))
            def _():
                _comm_start(c_ins, c_outs, *sems, c_exchange)

        @pl.when(s_id == 0)
        def _():
            dq_ref[...] = jnp.zeros_like(dq_ref)

        rows = pl.ds(pl.multiple_of(i * tq, tq), tq)

        def step(masked):
            for hh in range(nh):
                qh, kh = q_ref[:, hh * HQ:(hh + 1) * HQ], k_ref[:, hh * HQ:(hh + 1) * HQ]
                vh, doh = v_ref[:, hh * VDIM:(hh + 1) * VDIM], do_ref[:, hh * VDIM:(hh + 1) * VDIM]
                s_t = lax.dot_general(kh, qh, _DIMS["nt"], preferred_element_type=F32)
                p_t = jnp.exp(s_t - lse_ref[hh])
                if masked:
                    row = lax.broadcasted_iota(jnp.int32, (tq, tq), 0)
                    col = lax.broadcasted_iota(jnp.int32, (tq, tq), 1)
                    p_t = jnp.where(row <= col, p_t, 0.0)
                dv = jnp.dot(p_t.astype(BF16), doh, preferred_element_type=F32)
                dp_t = lax.dot_general(vh, doh, _DIMS["nt"], preferred_element_type=F32)
                ds_t = (p_t * (dp_t - dl_ref[hh])).astype(BF16)
                dk = jnp.dot(ds_t, qh, preferred_element_type=F32)
                if masked:
                    dv_sc[hh] = dv
                    dk_sc[hh] = dk
                else:
                    dv_sc[hh] += dv
                    dk_sc[hh] += dk
                dq_ref[rows, hh * HQ:(hh + 1) * HQ] += lax.dot_general(ds_t, kh, _DIMS["tn"], preferred_element_type=F32)

        @pl.when(i == j)
        def _():
            step(True)

        @pl.when(i != j)
        def _():
            step(False)

        @pl.when(i == nq - 1)
        def _():
            for hh in range(nh):
                dk_ref[:, hh * HQ:(hh + 1) * HQ] = dk_sc[hh].astype(BF16)
                dv_ref[:, hh * VDIM:(hh + 1) * VDIM] = dv_sc[hh].astype(BF16)

        if nc:
            @pl.when(jnp.logical_and(pl.program_id(0) == N_HEADS // nh - 1, s_id == n_steps - 1))
            def _():
                _comm_wait(c_ins, c_outs, *sems, c_exchange)

    any_spec = pl.BlockSpec(memory_space=pl.ANY)
    grid_spec = pltpu.PrefetchScalarGridSpec(
        num_scalar_prefetch=2, grid=(N_HEADS // nh, n_steps),
        in_specs=[pl.BlockSpec((tq, nh * HQ), lambda h, s, qi, kj: (qi[s], h)),
                  pl.BlockSpec((tq, nh * HQ), lambda h, s, qi, kj: (kj[s], h)),
                  pl.BlockSpec((tq, nh * VDIM), lambda h, s, qi, kj: (kj[s], h)),
                  pl.BlockSpec((tq, nh * VDIM), lambda h, s, qi, kj: (qi[s], h)),
                  pl.BlockSpec((nh, 1, tq), lambda h, s, qi, kj: (h, 0, qi[s])),
                  pl.BlockSpec((nh, 1, tq), lambda h, s, qi, kj: (h, 0, qi[s]))] + [any_spec] * nc,
        out_specs=[pl.BlockSpec((t_dim, nh * HQ), lambda h, s, qi, kj: (0, h)),
                   pl.BlockSpec((tq, nh * HQ), lambda h, s, qi, kj: (kj[s], h)),
                   pl.BlockSpec((tq, nh * VDIM), lambda h, s, qi, kj: (kj[s], h))] + [any_spec] * nc,
        scratch_shapes=[pltpu.VMEM((nh, tq, HQ), F32), pltpu.VMEM((nh, tq, VDIM), F32)] + _comm_sems(nc))
    return pl.pallas_call(
        body, grid_spec=grid_spec,
        out_shape=[jax.ShapeDtypeStruct((t_dim, N_HEADS * HQ), F32), jax.ShapeDtypeStruct((t_dim, N_HEADS * HQ), BF16),
                   jax.ShapeDtypeStruct((t_dim, N_HEADS * VDIM), BF16)] + _comm_out_shape(c_arrays, c_exchange),
        compiler_params=_cparams(("arbitrary", "arbitrary")), name=name)(
            jnp.asarray(qi), jnp.asarray(kj), q, k, v, do, lse_row, delta_row, *c_arrays)


def _merge_fwd(z_g, y_a, y_b, y_c, name):
    t_dim, d = y_a.shape

    def fn(i, ri, fi, ro, ao):
        acc = _sigmoid(ri[0][:, :d].astype(F32)) * ri[1][...].astype(F32)
        acc = acc + _sigmoid(ri[0][:, d:2 * d].astype(F32)) * ri[2][...].astype(F32)
        acc = acc + _sigmoid(ri[0][:, 2 * d:].astype(F32)) * ri[3][...].astype(F32)
        ro[0][...] = acc.astype(BF16)

    return _rowwise(fn, name, t_dim, ROW_TILE_WIDE, [(z_g, 3 * d, 0), (y_a, d, 0), (y_b, d, 0), (y_c, d, 0)], [],
                    [(d, BF16)])[0]


def _merge_bwd(z_g, y_a, y_b, y_c, dmerged, name):
    t_dim, d = y_a.shape

    def fn(i, ri, fi, ro, ao):
        dm = ri[4][...]
        for q in range(3):
            s = _sigmoid(ri[0][:, q * d:(q + 1) * d].astype(F32))
            ro[q][...] = (s * dm).astype(BF16)
            ro[3][:, q * d:(q + 1) * d] = (dm * ri[1 + q][...].astype(F32) * s * (1.0 - s)).astype(BF16)

    return _rowwise(fn, name, t_dim, ROW_TILE_WIDE,
                    [(z_g, 3 * d, 0), (y_a, d, 0), (y_b, d, 0), (y_c, d, 0), (dmerged, d, 0)], [],
                    [(d, BF16), (d, BF16), (d, BF16), (3 * d, BF16)])


def _adam(w, parts, m, v, name):
    r_dim, c_dim = w.shape
    limit = max(16, ADAM_BLOCK_ELEMS // c_dim // 16 * 16)
    tr = r_dim
    if r_dim > limit:
        tr = next((t for t in range(limit, 15, -16) if r_dim % t == 0), r_dim)

    def body(w_ref, p_ref, m_ref, v_ref, g_out, d_out, m_out, v_out):
        g = p_ref[0].astype(F32)
        for s in range(1, N_DEV):
            g = g + p_ref[s].astype(F32)
        m_new = ADAM_B1 * m_ref[...] + (1.0 - ADAM_B1) * g
        v_new = ADAM_B2 * v_ref[...] + (1.0 - ADAM_B2) * (g * g)
        m_hat = m_new / (1.0 - ADAM_B1 ** ADAM_STEP)
        v_hat = v_new / (1.0 - ADAM_B2 ** ADAM_STEP)
        g_out[...] = g
        d_out[...] = -ADAM_LR * (m_hat / (jnp.sqrt(v_hat) + ADAM_EPS) + ADAM_WD * w_ref[...])
        m_out[...] = m_new
        v_out[...] = v_new

    blk = pl.BlockSpec((tr, c_dim), lambda i: (i, 0))
    return pl.pallas_call(
        body, out_shape=[jax.ShapeDtypeStruct((r_dim, c_dim), F32)] * 4, grid=(r_dim // tr,),
        in_specs=[blk, pl.BlockSpec((N_DEV, tr, c_dim), lambda i: (0, i, 0)), blk, blk], out_specs=[blk] * 4,
        compiler_params=_cparams(("parallel",)), name=name)(w, parts, m, v)


def _me_and_peers():
    x, y, c = lax.axis_index("x"), lax.axis_index("y"), lax.axis_index("c")
    me = 4 * x + 2 * y + c
    peers = []
    for k in range(1, N_DEV):
        px, py, pc = x ^ (k >> 2), y ^ ((k >> 1) & 1), c ^ (k & 1)
        peers.append(((px, py, pc), 4 * px + 2 * py + pc))
    return me, peers


def _comm_copies(ins, outs, send_sems, recv_sems, local_sems, exchange, with_arrivals):
    me, peers = _me_and_peers()

    def src(w, dest_idx):
        return ins[w].at[dest_idx] if exchange else ins[w]

    def remote(w, k, dev, src_ref, dst_ref):
        return pltpu.make_async_remote_copy(
            src_ref=src_ref, dst_ref=dst_ref, send_sem=send_sems.at[w * (N_DEV - 1) + k],
            recv_sem=recv_sems.at[w * (N_DEV - 1) + k], device_id=dev, device_id_type=pl.DeviceIdType.MESH)

    local = [pltpu.make_async_copy(src(w, me), outs[w].at[me], local_sems.at[w]) for w in range(len(ins))]
    sends, arrivals = [], []
    for w in range(len(ins)):
        for k, (dev, idx) in enumerate(peers):
            sends.append(remote(w, k, dev, src(w, idx), outs[w].at[me]))
            if with_arrivals:
                arrivals.append(remote(w, k, dev, src(w, idx), outs[w].at[idx]))
    return local, sends, arrivals


def _comm_start(ins, outs, send_sems, recv_sems, local_sems, exchange):
    local, sends, _ = _comm_copies(ins, outs, send_sems, recv_sems, local_sems, exchange, False)
    for cp in local + sends:
        cp.start()


def _comm_wait(ins, outs, send_sems, recv_sems, local_sems, exchange):
    local, sends, arrivals = _comm_copies(ins, outs, send_sems, recv_sems, local_sems, exchange, True)
    for cp in arrivals:
        cp.wait_recv()
    for cp in sends:
        cp.wait_send()
    for cp in local:
        cp.wait()


def _comm_sems(n):
    if not n:
        return []
    return [pltpu.SemaphoreType.DMA((n * (N_DEV - 1),)), pltpu.SemaphoreType.DMA((n * (N_DEV - 1),)),
            pltpu.SemaphoreType.DMA((n,))]


def _comm_out_shape(arrays, exchange):
    return [jax.ShapeDtypeStruct(a.shape if exchange else (N_DEV,) + a.shape, a.dtype) for a in arrays]


def _comm(arrays, exchange, name):
    n = len(arrays)
    hbm = pl.BlockSpec(memory_space=pltpu.HBM)

    def body(*refs):
        ins, outs, sems = refs[:n], refs[n:2 * n], refs[2 * n:]
        _comm_start(ins, outs, *sems, exchange)
        _comm_wait(ins, outs, *sems, exchange)

    return pl.pallas_call(
        body, out_shape=_comm_out_shape(arrays, exchange), in_specs=[hbm] * n, out_specs=[hbm] * n,
        scratch_shapes=_comm_sems(n), name=name)(*arrays)


def _unshard(name, g):
    if name in COL_SHARDED:
        return jnp.transpose(g, (1, 0, 2)).reshape(g.shape[1], N_DEV * g.shape[2])
    return g.reshape(N_DEV * g.shape[1], g.shape[2])


def _to_shards(name, full):
    if name in COL_SHARDED:
        r, c = full.shape
        return jnp.transpose(full.reshape(r, N_DEV, c // N_DEV), (1, 0, 2))
    return full.reshape(N_DEV, full.shape[0] // N_DEV, full.shape[1])


def _mmc(a, b, mode, out_dtype, name, carry):
    if carry:
        return _mm(a, b, mode, out_dtype, name + "_comm", carry)
    return _mm(a, b, mode, out_dtype, name), []


def _ffn_fwd(x, p, tag, carry_in=None, carry_out=None):
    h = _norm_fwd(x, p["norm_pre"], f"{tag}_norm")
    gu, got_in = _mmc(h, p["w_in"], "nn", BF16, f"{tag}_in", carry_in)
    act = _swiglu_fwd(gu, f"{tag}_act")
    y, got_out = _mmc(act, p["w_out"], "nn", F32, f"{tag}_out", carry_out)
    x_new = _resid_fwd(x, y, p["norm_post"], 0.5, f"{tag}_resid")
    return x_new, dict(x=x, h=h, gu=gu, act=act, y=y), got_in, got_out


def _ffn_bwd(dxo, s, p, tag, carry_out=None, carry_in=None):
    dy, dg_post = _resid_bwd(s["y"], dxo, p["norm_post"], 0.5, f"{tag}_resid_bwd")
    dact, got_out = _mmc(dy, p["w_out"], "nt", BF16, f"{tag}_out_dx", carry_out)
    dw_out = _mm(s["act"], dy, "tn", BF16, f"{tag}_out_dw", shard_out="row")
    dgu = _swiglu_bwd(s["gu"], dact, f"{tag}_act_bwd")
    dh, got_in = _mmc(dgu, p["w_in"], "nt", F32, f"{tag}_in_dx", carry_in)
    dw_in = _mm(s["h"], dgu, "tn", BF16, f"{tag}_in_dw", shard_out="col")
    dx, dg_pre = _norm_bwd(s["x"], [dh], dxo, p["norm_pre"], f"{tag}_norm_bwd")
    return dx, dict(norm_pre=dg_pre, norm_post=dg_post, w_in=dw_in, w_out=dw_out), got_out, got_in


def _mixer_fwd(x, p, rope_tabs, dims, carry):
    gmw, cw, qr, kvr = dims["gmw"], dims["cw"], dims["qr"], dims["kvr"]
    cf, s1, s2 = rope_tabs
    scale = (NOPE + ROPE) ** -0.5
    h = _norm_fwd(x, p["norm_pre"], "mix_norm")
    z_a = _mm(h, p["w_a"], "nn", BF16, "mix_in_a")
    z_c = _mm(h, p["w_c"], "nn", F32, "mix_in_c")
    z_g = _mm(h, p["w_g"], "nn", BF16, "mix_in_g")
    o_a = _gm_fwd(z_a, p["gm_ln_g"], p["gm_ln_b"], p["gm_w_s"], p["gm_b_s"], gmw, "gm_fwd")
    a = _glu_fwd(z_a, gmw, cw, "glu_fwd")
    y_conv, o_b = _conv_fwd(a, p["conv_w"], p["conv_b"], p["conv_ln_g"], p["conv_ln_b"], "conv_fwd")
    cqn, ckvn = _mla_norm_fwd(z_c, p["q_norm"], p["kv_norm"], qr, kvr, "mla_norm")
    q_pre = _mm(cqn, p["w_uq"], "nn", BF16, "mla_uq")
    k_nope = _mm(ckvn, p["w_uk"], "nn", BF16, "mla_uk")
    v = _mm(ckvn, p["w_uv"], "nn", BF16, "mla_uv")
    q_cat, k_cat = _rope_fwd(q_pre, k_nope, z_c, cf, s1, s2, (qr + kvr) // 128, scale, "rope_fwd")
    o_c, lse, *carried = _attn_fwd(q_cat, k_cat, v, "attn_fwd_comm" if carry else "attn_fwd", carry)
    y_a = _mm(o_a, p["wb_a"], "nn", BF16, "branch_a")
    y_b = _mm(o_b, p["wb_b"], "nn", BF16, "branch_b")
    y_c = _mm(o_c, p["wb_c"], "nn", BF16, "branch_c")
    merged = _merge_fwd(z_g, y_a, y_b, y_c, "merge_fwd")
    m = _mm(merged, p["w_out"], "nn", F32, "mix_out")
    x_new = _resid_fwd(x, m, p["norm_post"], 1.0, "mix_resid")
    saved = dict(x=x, h=h, z_a=z_a, z_c=z_c, z_g=z_g, o_a=o_a, a=a, y_conv=y_conv, o_b=o_b, cqn=cqn, ckvn=ckvn,
                 v=v, q_cat=q_cat, k_cat=k_cat, o_c=o_c, lse=lse, y_a=y_a, y_b=y_b, y_c=y_c, merged=merged, m=m)
    return x_new, saved, carried


def _mixer_bwd(dxo, s, p, rope_tabs, dims, carry):
    gmw, cw, qr, kvr = dims["gmw"], dims["cw"], dims["qr"], dims["kvr"]
    cf, s1, s2 = rope_tabs
    scale = (NOPE + ROPE) ** -0.5
    t_dim = dxo.shape[0]
    g = {}
    dm, g["norm_post"] = _resid_bwd(s["m"], dxo, p["norm_post"], 1.0, "mix_resid_bwd")
    dmerged = _mm(dm, p["w_out"], "nt", F32, "mix_out_dx")
    g["w_out"] = _mm(s["merged"], dm, "tn", F32, "mix_out_dw")
    dy_a, dy_b, dy_c, dz_g = _merge_bwd(s["z_g"], s["y_a"], s["y_b"], s["y_c"], dmerged, "merge_bwd")
    do_a = _mm(dy_a, p["wb_a"], "nt", F32, "branch_a_dx")
    do_b = _mm(dy_b, p["wb_b"], "nt", F32, "branch_b_dx")
    do_c = _mm(dy_c, p["wb_c"], "nt", BF16, "branch_c_dx")
    g["w_branch"] = jnp.concatenate([_mm(s["o_a"], dy_a, "tn", F32, "branch_a_dw"),
                                     _mm(s["o_b"], dy_b, "tn", F32, "branch_b_dw"),
                                     _mm(s["o_c"], dy_c, "tn", F32, "branch_c_dw")], axis=0)
    delta = _attn_delta(do_c, s["o_c"], "attn_delta")
    dq_cat, dk_cat, dv, *carried = _attn_bwd(s["q_cat"], s["k_cat"], s["v"], do_c, s["lse"].reshape(N_HEADS, 1, t_dim),
                                             delta.reshape(N_HEADS, 1, t_dim),
                                             "attn_bwd_comm" if carry else "attn_bwd", carry)
    dq_pre, dk_nope, dkr = _rope_bwd(dq_cat, dk_cat, cf, s1, s2, scale, "rope_bwd")
    dcq = _mm(dq_pre, p["w_uq"], "nt", F32, "mla_uq_dx")
    g["w_uq"] = _mm(s["cqn"], dq_pre, "tn", F32, "mla_uq_dw")
    dckv_k = _mm(dk_nope, p["w_uk"], "nt", F32, "mla_uk_dx")
    dckv_v = _mm(dv, p["w_uv"], "nt", F32, "mla_uv_dx")
    g["w_uk"] = _mm(s["ckvn"], dk_nope, "tn", F32, "mla_uk_dw")
    g["w_uv"] = _mm(s["ckvn"], dv, "tn", F32, "mla_uv_dw")
    dz_c, g["q_norm"], g["kv_norm"] = _mla_norm_bwd(s["z_c"], dcq, dckv_k, dckv_v, dkr, p["q_norm"], p["kv_norm"],
                                                    qr, kvr, "mla_norm_bwd")
    dy_conv, g["conv_ln_g"], g["conv_ln_b"] = _conv_ln_bwd(s["y_conv"], do_b, p["conv_ln_g"], p["conv_ln_b"], "conv_ln_bwd")
    dz_cv, g["conv_w"], g["conv_b"] = _conv_bwd(dy_conv, s["a"], s["z_a"], p["conv_w"], gmw, "conv_bwd")
    dz_gm, g["gm_ln_g"], g["gm_ln_b"], g["gm_w_s"], g["gm_b_s"] = _gm_bwd(
        s["z_a"], do_a, p["gm_ln_g"], p["gm_ln_b"], p["gm_w_s"], p["gm_w_s_t"], p["gm_b_s"], gmw, "gm_bwd")
    dh_gm = _mm(dz_gm, p["w_gm"], "nt", F32, "mix_in_gm_dx")
    dh_cv = _mm(dz_cv, p["w_cv"], "nt", F32, "mix_in_cv_dx")
    dh_c = _mm(dz_c, p["w_c"], "nt", F32, "mix_in_c_dx")
    dh_g = _mm(dz_g, p["w_g"], "nt", F32, "mix_in_g_dx")
    g["w_gm"] = _mm(s["h"], dz_gm, "tn", F32, "mix_in_gm_dw")
    g["w_cv"] = _mm(s["h"], dz_cv, "tn", F32, "mix_in_cv_dw")
    g["w_c"] = _mm(s["h"], dz_c, "tn", F32, "mix_in_c_dw")
    g["w_g"] = _mm(s["h"], dz_g, "tn", F32, "mix_in_g_dw")
    dx, g["norm_pre"] = _norm_bwd(s["x"], [dh_gm, dh_cv, dh_c, dh_g], dxo, p["norm_pre"], "mix_norm_bwd")
    return dx, g, carried


def _ffn_params(full, small, k, l):
    row = lambda a: a[l][None, :]
    return dict(norm_pre=row(small[f"{k}_norm_pre"]), norm_post=row(small[f"{k}_norm_post"]),
                w_in=full[f"{k}_w_in"], w_out=full[f"{k}_w_out"])


def _mix_params(full, small, conv_w_full, dims, l):
    gmw, cw, qr, kvr, d = dims["gmw"], dims["cw"], dims["qr"], dims["kvr"], dims["d"]
    row = lambda a: a[l][None, :]
    w_in = full["mix_w_in"]
    a_end = 2 * gmw + 2 * cw
    c_end = a_end + qr + kvr + ROPE
    w_c = jnp.concatenate([w_in[:, a_end:c_end], jnp.zeros((d, 128 - ROPE), w_in.dtype)], axis=1)
    w_uq = full["mla_w_uq"].reshape(qr, N_HEADS, NOPE + ROPE)
    w_uq = jnp.concatenate([w_uq, jnp.zeros((qr, N_HEADS, HQ - NOPE - ROPE), w_uq.dtype)], axis=2).reshape(qr, N_HEADS * HQ)
    w_ukv = full["mla_w_ukv"].reshape(kvr, N_HEADS, NOPE + VDIM)
    w_b = full["mix_w_branch"]
    w_s = small["gm_w_s"][l]
    return dict(norm_pre=row(small["mix_norm_pre"]), norm_post=row(small["mix_norm_post"]),
                w_a=w_in[:, :a_end], w_gm=w_in[:, :2 * gmw], w_cv=w_in[:, 2 * gmw:a_end], w_c=w_c, w_g=w_in[:, c_end:],
                gm_ln_g=row(small["gm_ln_g"]), gm_ln_b=row(small["gm_ln_b"]), gm_w_s=w_s,
                gm_w_s_t=jnp.transpose(w_s, (0, 2, 1)), gm_b_s=small["gm_b_s"][l][:, :, None],
                conv_w=jnp.concatenate([conv_w_full, jnp.zeros((CONV_HALO - CONV_K, cw), F32)], axis=0),
                conv_b=row(small["conv_b"]), conv_ln_g=row(small["conv_ln_g"]), conv_ln_b=row(small["conv_ln_b"]),
                q_norm=row(small["mla_q_norm"]), kv_norm=row(small["mla_kv_norm"]),
                w_uq=w_uq, w_uk=w_ukv[:, :, :NOPE].reshape(kvr, N_HEADS * NOPE),
                w_uv=w_ukv[:, :, NOPE:].reshape(kvr, N_HEADS * VDIM),
                wb_a=w_b[:gmw], wb_b=w_b[gmw:gmw + cw], wb_c=w_b[gmw + cw:], w_out=full["mix_w_out"])


def _mix_big_grads(gm, dims):
    qr, kvr = dims["qr"], dims["kvr"]
    return {
        "mix_w_in": jnp.concatenate([gm["w_gm"], gm["w_cv"], gm["w_c"][:, :qr + kvr + ROPE], gm["w_g"]], axis=1),
        "mla_w_uq": gm["w_uq"].reshape(qr, N_HEADS, HQ)[:, :, :NOPE + ROPE].reshape(qr, N_HEADS * (NOPE + ROPE)),
        "mla_w_ukv": jnp.concatenate([gm["w_uk"].reshape(kvr, N_HEADS, NOPE), gm["w_uv"].reshape(kvr, N_HEADS, VDIM)],
                                     axis=2).reshape(kvr, N_HEADS * (NOPE + VDIM)),
        "mix_w_branch": gm["w_branch"], "mix_w_out": gm["w_out"], "conv_w": gm["conv_w"][:CONV_K],
    }


def _small_grads(g1, gm, g2):
    return {
        "ffn1_norm_pre": g1["norm_pre"][0], "ffn1_norm_post": g1["norm_post"][0],
        "ffn2_norm_pre": g2["norm_pre"][0], "ffn2_norm_post": g2["norm_post"][0],
        "mix_norm_pre": gm["norm_pre"][0], "mix_norm_post": gm["norm_post"][0],
        "gm_ln_g": gm["gm_ln_g"][0], "gm_ln_b": gm["gm_ln_b"][0], "gm_w_s": gm["gm_w_s"], "gm_b_s": gm["gm_b_s"][:, :, 0],
        "conv_b": gm["conv_b"][0], "conv_ln_g": gm["conv_ln_g"][0], "conv_ln_b": gm["conv_ln_b"][0],
        "mla_q_norm": gm["q_norm"][0], "mla_kv_norm": gm["kv_norm"][0],
    }


def _rope_tables(positions):
    inv_freq = ROPE_THETA ** (-jnp.arange(0, ROPE, 2, dtype=F32) / ROPE)
    ang = positions.astype(F32)[:, None] * inv_freq
    cos, sin = jnp.cos(ang), jnp.sin(ang)
    z = lambda w: jnp.zeros((positions.shape[0], w), F32)
    return (jnp.concatenate([cos, cos, z(64)], axis=1), jnp.concatenate([-sin, z(96)], axis=1),
            jnp.concatenate([z(32), sin, z(64)], axis=1))


def _pad_rows(flat, mult):
    n = flat.shape[0]
    pad = (-n) % mult
    return jnp.concatenate([flat, jnp.zeros((pad,), flat.dtype)]) if pad else flat


def kernel(x, positions, ffn1_norm_pre, ffn1_norm_post, ffn1_w_in, ffn1_w_out, mix_norm_pre, mix_norm_post, mix_w_in, gm_ln_g, gm_ln_b, gm_w_s, gm_b_s, conv_w, conv_b, conv_ln_g, conv_ln_b, mla_q_norm, mla_w_uq, mla_kv_norm, mla_w_ukv, mix_w_branch, mix_w_out, ffn2_norm_pre, ffn2_norm_post, ffn2_w_in, ffn2_w_out, loss_target, m_ffn1_norm_pre, m_ffn1_norm_post, m_ffn1_w_in, m_ffn1_w_out, m_mix_norm_pre, m_mix_norm_post, m_mix_w_in, m_gm_ln_g, m_gm_ln_b, m_gm_w_s, m_gm_b_s, m_conv_w, m_conv_b, m_conv_ln_g, m_conv_ln_b, m_mla_q_norm, m_mla_w_uq, m_mla_kv_norm, m_mla_w_ukv, m_mix_w_branch, m_mix_w_out, m_ffn2_norm_pre, m_ffn2_norm_post, m_ffn2_w_in, m_ffn2_w_out, v_ffn1_norm_pre, v_ffn1_norm_post, v_ffn1_w_in, v_ffn1_w_out, v_mix_norm_pre, v_mix_norm_post, v_mix_w_in, v_gm_ln_g, v_gm_ln_b, v_gm_w_s, v_gm_b_s, v_conv_w, v_conv_b, v_conv_ln_g, v_conv_ln_b, v_mla_q_norm, v_mla_w_uq, v_mla_kv_norm, v_mla_w_ukv, v_mix_w_branch, v_mix_w_out, v_ffn2_norm_pre, v_ffn2_norm_post, v_ffn2_w_in, v_ffn2_w_out):
    w = dict(zip(WEIGHTS, (ffn1_norm_pre, ffn1_norm_post, ffn1_w_in, ffn1_w_out, mix_norm_pre, mix_norm_post, mix_w_in, gm_ln_g, gm_ln_b, gm_w_s, gm_b_s, conv_w, conv_b, conv_ln_g, conv_ln_b, mla_q_norm, mla_w_uq, mla_kv_norm, mla_w_ukv, mix_w_branch, mix_w_out, ffn2_norm_pre, ffn2_norm_post, ffn2_w_in, ffn2_w_out)))
    mom_m = dict(zip(WEIGHTS, (m_ffn1_norm_pre, m_ffn1_norm_post, m_ffn1_w_in, m_ffn1_w_out, m_mix_norm_pre, m_mix_norm_post, m_mix_w_in, m_gm_ln_g, m_gm_ln_b, m_gm_w_s, m_gm_b_s, m_conv_w, m_conv_b, m_conv_ln_g, m_conv_ln_b, m_mla_q_norm, m_mla_w_uq, m_mla_kv_norm, m_mla_w_ukv, m_mix_w_branch, m_mix_w_out, m_ffn2_norm_pre, m_ffn2_norm_post, m_ffn2_w_in, m_ffn2_w_out)))
    mom_v = dict(zip(WEIGHTS, (v_ffn1_norm_pre, v_ffn1_norm_post, v_ffn1_w_in, v_ffn1_w_out, v_mix_norm_pre, v_mix_norm_post, v_mix_w_in, v_gm_ln_g, v_gm_ln_b, v_gm_w_s, v_gm_b_s, v_conv_w, v_conv_b, v_conv_ln_g, v_conv_ln_b, v_mla_q_norm, v_mla_w_uq, v_mla_kv_norm, v_mla_w_ukv, v_mix_w_branch, v_mix_w_out, v_ffn2_norm_pre, v_ffn2_norm_post, v_ffn2_w_in, v_ffn2_w_out)))
    n_layers = ffn1_norm_pre.shape[0]
    t_dim, d = x.shape[1], x.shape[2]
    dims = dict(d=d, gmw=gm_ln_g.shape[1], cw=conv_ln_g.shape[1], qr=mla_q_norm.shape[1], kvr=mla_kv_norm.shape[1])
    x0 = x.reshape(t_dim, d)
    target = loss_target.reshape(t_dim, d)
    rope_tabs = _rope_tables(positions.reshape(t_dim))

    conv_all = _unshard_conv(_comm([conv_w], False, "gather_conv")[0])
    gather = lambda names, l: ([w[k][l].astype(BF16) for k in names], False)
    full_of = lambda names, gathered: {k: _unshard(k, g) for k, g in zip(names, gathered)}
    last = n_layers - 1
    got_ffn1 = _comm(gather(FFN1_W, 0)[0], False, "gather_ffn1")
    got_mix = None
    params, saved = [], []
    xc = x0
    for l in range(n_layers):
        p1 = _ffn_params(full_of(FFN1_W, got_ffn1), w, "ffn1", l)
        xc, s1, got_a, got_b = _ffn_fwd(xc, p1, "ffn1", gather(MIX_W[:1], 0) if l == 0 else None,
                                        gather(MIX_W[1:], 0) if l == 0 else None)
        if l == 0:
            got_mix = got_a + got_b
        pm = _mix_params(full_of(MIX_W, got_mix), w, conv_all[l], dims, l)
        xc, sm, got = _mixer_fwd(xc, pm, rope_tabs, dims, gather(FFN2_W, l) if l == last else
                                 (gather(FFN2_W, l)[0] + gather(MIX_W, l + 1)[0], False))
        got_ffn2, got_mix = got[:len(FFN2_W)], got[len(FFN2_W):]
        p2 = _ffn_params(full_of(FFN2_W, got_ffn2), w, "ffn2", l)
        xc, s2, got_a, got_b = _ffn_fwd(xc, p2, "ffn2", gather(FFN1_W[:1], l + 1) if l < last else None,
                                        gather(FFN1_W[1:], l + 1) if l < last else None)
        got_ffn1 = got_a + got_b
        params.append((p1, pm, p2))
        saved.append((s1, sm, s2))
    dx, loss_part = _loss_fwd_bwd(xc, target, "loss")
    loss = lax.psum(loss_part[0, 0], ("x", "y", "c"))

    big_out = {k: [None] * n_layers for k in BIG + ("conv_w",)}
    small_parts = [None] * n_layers
    send = lambda names, grads: [_to_shards(k, grads[k]).astype(BF16) for k in names]

    def update(l, names, recv):
        for k, r in zip(names, recv):
            wl = w[k][l]
            r2 = (lambda a: a.reshape(-1, a.shape[-1]))
            outs = _adam(r2(wl), r.reshape(N_DEV, -1, wl.shape[-1]), r2(mom_m[k][l]), r2(mom_v[k][l]), f"adam_{k}")
            big_out[k][l] = [o.reshape(wl.shape) for o in outs]

    above = []
    for l in reversed(range(n_layers)):
        p1, pm, p2 = params[l]
        s1, sm, s2 = saved[l]
        dx, g2, _, _ = _ffn_bwd(dx, s2, p2, "ffn2")
        mine = [g2["w_in"], g2["w_out"]]
        dx, gm, recv = _mixer_bwd(dx, sm, pm, rope_tabs, dims, (mine + above, True))
        update(l, FFN2_W, recv[:len(FFN2_W)])
        if above:
            update(l + 1, MIX_G + FFN1_W, recv[len(FFN2_W):])
        mix_out = send(MIX_G, _mix_big_grads(gm, dims))
        if l > 0:
            dx, g1, _, _ = _ffn_bwd(dx, s1, p1, "ffn1")
        else:
            dx, g1, got_a, got_b = _ffn_bwd(dx, s1, p1, "ffn1", (mix_out[1:], True), (mix_out[:1], True))
            update(0, MIX_G, got_b + got_a)
        ffn1_out = [g1["w_in"], g1["w_out"]]
        above = mix_out + ffn1_out
        small_parts[l] = _small_grads(g1, gm, g2)
    update(0, FFN1_W, _comm(ffn1_out, True, "exchange_ffn1"))
    grad_x = dx.reshape(x.shape)

    flat = lambda tree: _pad_rows(jnp.concatenate([tree[k].reshape(-1) for k in SMALL]), 256 * 128).reshape(-1, 128)
    g_small = flat({k: jnp.stack([small_parts[l][k] for l in range(n_layers)]) for k in SMALL})
    parts = _comm([g_small], False, "gather_small_grads")[0]
    s_outs = _adam(flat(w), parts, flat(mom_m), flat(mom_v), "adam_small")
    small_out = {k: [] for k in SMALL}
    for o in s_outs:
        o = o.reshape(-1)
        off = 0
        for k in SMALL:
            n = int(np.prod(w[k].shape))
            small_out[k].append(o[off:off + n].reshape(w[k].shape))
            off += n

    def out(which, k):
        if k in SMALL:
            return small_out[k][which]
        return jnp.stack([big_out[k][l][which] for l in range(n_layers)])

    return (loss, grad_x, *[out(0, k) for k in WEIGHTS], *[out(1, k) for k in WEIGHTS],
            *[out(2, k) for k in WEIGHTS], *[out(3, k) for k in WEIGHTS])


def _unshard_conv(g):
    n_dev, n_layers, k, c = g.shape
    return jnp.transpose(g, (1, 2, 0, 3)).reshape(n_layers, k, n_dev * c)
```

```python
import functools

import numpy as np
import jax
import jax.numpy as jnp
from jax import lax
from jax.experimental import pallas as pl
from jax.experimental.pallas import tpu as pltpu

F32 = jnp.float32
BF16 = jnp.bfloat16

N_DEV = 8
N_HEADS = 16
NOPE = 128
ROPE = 64
VDIM = 128
HQ = 256
GROUPS = 4
CHUNK = 128
CONV_K = 31
CONV_HALO = 32
EPS = 1e-6
ROPE_THETA = 10000.0
ADAM_LR = 0.001
ADAM_B1 = 0.9
ADAM_B2 = 0.999
ADAM_EPS = 1e-08
ADAM_WD = 0.01
ADAM_STEP = 10
NEG = -1e30

V7X_VMEM_LIMIT = 56 * 1024 * 1024
MM_TM, MM_TN, MM_TK = 1024, 1024, 4096
ROW_TILE = 256
ROW_TILE_WIDE = 256
ATT_TILE = 1024
ATT_FWD_HEADS = 2
ATT_BWD_HEADS = 2
CONV_TILE = 256
GM_TILE = 256
ADAM_BLOCK_ELEMS = 128 * 1024

BIG = ("ffn1_w_in", "ffn1_w_out", "mix_w_in", "mla_w_uq", "mla_w_ukv", "mix_w_branch", "mix_w_out",
       "ffn2_w_in", "ffn2_w_out")
FFN1_W = ("ffn1_w_in", "ffn1_w_out")
MIX_W = ("mix_w_in", "mla_w_uq", "mla_w_ukv", "mix_w_branch", "mix_w_out")
FFN2_W = ("ffn2_w_in", "ffn2_w_out")
MIX_G = MIX_W + ("conv_w",)
COL_SHARDED = ("ffn1_w_in", "mix_w_in", "mla_w_uq", "mla_w_ukv", "ffn2_w_in", "conv_w")
SMALL = ("ffn1_norm_pre", "ffn1_norm_post", "mix_norm_pre", "mix_norm_post", "gm_ln_g", "gm_ln_b", "gm_w_s",
         "gm_b_s", "conv_b", "conv_ln_g", "conv_ln_b", "mla_q_norm", "mla_kv_norm", "ffn2_norm_pre",
         "ffn2_norm_post")
WEIGHTS = ("ffn1_norm_pre", "ffn1_norm_post", "ffn1_w_in", "ffn1_w_out", "mix_norm_pre", "mix_norm_post",
           "mix_w_in", "gm_ln_g", "gm_ln_b", "gm_w_s", "gm_b_s", "conv_w", "conv_b", "conv_ln_g", "conv_ln_b",
           "mla_q_norm", "mla_w_uq", "mla_kv_norm", "mla_w_ukv", "mix_w_branch", "mix_w_out", "ffn2_norm_pre",
           "ffn2_norm_post", "ffn2_w_in", "ffn2_w_out")


def _cparams(sem):
    return pltpu.CompilerParams(dimension_semantics=sem, vmem_limit_bytes=V7X_VMEM_LIMIT)


def _pick(dim, pref):
    if dim <= pref:
        return dim
    t = pref
    while t >= 128:
        if dim % t == 0:
            return t
        t -= 128
    return dim


def _sigmoid(x):
    return 1.0 / (1.0 + jnp.exp(-x))


_GELU_C = 0.7978845608028654


def _gelu(x):
    t = jnp.tanh(_GELU_C * (x + 0.044715 * x * x * x))
    return 0.5 * x * (1.0 + t)


def _gelu_grad(x):
    t = jnp.tanh(_GELU_C * (x + 0.044715 * x * x * x))
    return 0.5 * (1.0 + t) + 0.5 * x * (1.0 - t * t) * _GELU_C * (1.0 + 3.0 * 0.044715 * x * x)


def _rms(x):
    r = lax.rsqrt(jnp.mean(x * x, axis=-1, keepdims=True) + EPS)
    return x * r, r


def _rms_bwd(xn, r, t):
    return r * (t - xn * jnp.mean(t * xn, axis=-1, keepdims=True))


def _colsum(x):
    return jnp.sum(x, axis=0, keepdims=True)


_DIMS = {"nn": (((1,), (0,)), ((), ())), "nt": (((1,), (1,)), ((), ())), "tn": (((0,), (0,)), ((), ()))}


def _mm(a, b, mode, out_dtype, name, carry=None, shard_out=None):
    if mode == "tn":
        k_dim, m_dim = a.shape
    else:
        m_dim, k_dim = a.shape
    n_dim = b.shape[0] if mode == "nt" else b.shape[1]
    tm = _pick(m_dim // N_DEV if shard_out == "row" else m_dim, MM_TM)
    tn = _pick(n_dim // N_DEV if shard_out == "col" else n_dim, MM_TN)
    tk = _pick(k_dim, MM_TK)
    grid = (m_dim // tm, n_dim // tn, k_dim // tk)
    nk = grid[2]
    dims = _DIMS[mode]
    c_arrays, c_exchange = carry if carry else ((), False)
    nc = len(c_arrays)

    def at_step(first):
        conds = [pl.program_id(q) == (0 if first else grid[q] - 1) for q in range(3)]
        return jnp.logical_and(jnp.logical_and(conds[0], conds[1]), conds[2])

    def body(a_ref, b_ref, *rest):
        c_ins, o_ref, c_outs = rest[:nc], rest[nc], rest[nc + 1:2 * nc + 1]
        scratch = rest[2 * nc + 1:]
        sems = scratch[1:] if nk > 1 else scratch
        if nc:
            @pl.when(at_step(True))
            def _():
                _comm_start(c_ins, c_outs, *sems, c_exchange)

        prod = lax.dot_general(a_ref[...], b_ref[...], dims, preferred_element_type=F32)
        if nk == 1:
            o_ref[...] = prod.astype(o_ref.dtype)
        else:
            acc_ref, k = scratch[0], pl.program_id(2)

            @pl.when(k == 0)
            def _():
                acc_ref[...] = prod

            @pl.when(jnp.logical_and(k > 0, k < nk - 1))
            def _():
                acc_ref[...] += prod

            @pl.when(k == nk - 1)
            def _():
                o_ref[...] = (acc_ref[...] + prod).astype(o_ref.dtype)

        if nc:
            @pl.when(at_step(False))
            def _():
                _comm_wait(c_ins, c_outs, *sems, c_exchange)

    if mode == "tn":
        a_spec = pl.BlockSpec((tk, tm), lambda i, j, k: (k, i))
    else:
        a_spec = pl.BlockSpec((tm, tk), lambda i, j, k: (i, k))
    if mode == "nt":
        b_spec = pl.BlockSpec((tn, tk), lambda i, j, k: (j, k))
    else:
        b_spec = pl.BlockSpec((tk, tn), lambda i, j, k: (k, j))
    any_spec = pl.BlockSpec(memory_space=pl.ANY)
    if shard_out == "col":
        per = n_dim // N_DEV // tn
        o_shape, o_spec = (N_DEV, m_dim, n_dim // N_DEV), pl.BlockSpec((None, tm, tn), lambda i, j, k: (j // per, i, j % per))
    elif shard_out == "row":
        per = m_dim // N_DEV // tm
        o_shape, o_spec = (N_DEV, m_dim // N_DEV, n_dim), pl.BlockSpec((None, tm, tn), lambda i, j, k: (i // per, i % per, j))
    else:
        o_shape, o_spec = (m_dim, n_dim), pl.BlockSpec((tm, tn), lambda i, j, k: (i, j))
    outs = pl.pallas_call(
        body, out_shape=[jax.ShapeDtypeStruct(o_shape, out_dtype)] + _comm_out_shape(c_arrays, c_exchange),
        grid=grid, in_specs=[a_spec, b_spec] + [any_spec] * nc,
        out_specs=[o_spec] + [any_spec] * nc,
        scratch_shapes=([pltpu.VMEM((tm, tn), F32)] if nk > 1 else []) + _comm_sems(nc),
        compiler_params=_cparams(("arbitrary",) * 3 if nc else ("parallel", "parallel", "arbitrary")),
        name=name)(a, b, *c_arrays)
    return (outs[0], outs[1:]) if nc else outs[0]


def _rowwise(fn, name, n_rows, tr, row_ins, full_ins, row_outs, acc_outs=()):
    tr = min(tr, n_rows)
    n_ri, n_fi, n_ro = len(row_ins), len(full_ins), len(row_outs)

    def body(*refs):
        i = pl.program_id(0)
        ri, fi = refs[:n_ri], refs[n_ri:n_ri + n_fi]
        ro, ao = refs[n_ri + n_fi:n_ri + n_fi + n_ro], refs[n_ri + n_fi + n_ro:]

        @pl.when(i == 0)
        def _():
            for r in ao:
                r[...] = jnp.zeros_like(r)

        fn(i, ri, fi, ro, ao)

    in_specs = [pl.BlockSpec((tr, w), functools.partial(lambda c, i: (i, c), cb)) for _, w, cb in row_ins]
    in_specs += [pl.BlockSpec(a.shape, functools.partial(lambda nd, i: (0,) * nd, a.ndim)) for a in full_ins]
    out_specs = [pl.BlockSpec((tr, w), lambda i: (i, 0)) for w, _ in row_outs]
    out_specs += [pl.BlockSpec(s, functools.partial(lambda nd, i: (0,) * nd, len(s))) for s, _ in acc_outs]
    out_shape = [jax.ShapeDtypeStruct((n_rows, w), d) for w, d in row_outs]
    out_shape += [jax.ShapeDtypeStruct(s, d) for s, d in acc_outs]
    return pl.pallas_call(
        body, out_shape=out_shape, grid=(n_rows // tr,), in_specs=in_specs, out_specs=out_specs,
        compiler_params=_cparams(("arbitrary",)), name=name)(*[a for a, _, _ in row_ins], *full_ins)


def _norm_fwd(x, g, name):
    t_dim, d = x.shape

    def fn(i, ri, fi, ro, ao):
        xn, _ = _rms(ri[0][...])
        ro[0][...] = (xn * fi[0][...]).astype(BF16)

    return _rowwise(fn, name, t_dim, ROW_TILE, [(x, d, 0)], [g], [(d, BF16)])[0]


def _resid_fwd(x, y, g, coef, name):
    t_dim, d = x.shape

    def fn(i, ri, fi, ro, ao):
        yn, _ = _rms(ri[1][...])
        ro[0][...] = ri[0][...] + coef * (yn * fi[0][...])

    return _rowwise(fn, name, t_dim, ROW_TILE, [(x, d, 0), (y, d, 0)], [g], [(d, F32)])[0]


def _resid_bwd(y, dxo, g, coef, name):
    t_dim, d = y.shape

    def fn(i, ri, fi, ro, ao):
        yn, r = _rms(ri[0][...])
        dyn = coef * ri[1][...]
        ao[0][...] += _colsum(dyn * yn)
        ro[0][...] = _rms_bwd(yn, r, dyn * fi[0][...]).astype(BF16)

    return _rowwise(fn, name, t_dim, ROW_TILE, [(y, d, 0), (dxo, d, 0)], [g], [(d, BF16)], [((1, d), F32)])


def _norm_bwd(x, dhs, dxo, g, name):
    t_dim, d = x.shape
    n = len(dhs)

    def fn(i, ri, fi, ro, ao):
        xn, r = _rms(ri[0][...])
        dh = ri[2][...]
        for q in range(1, n):
            dh = dh + ri[2 + q][...]
        ao[0][...] += _colsum(dh * xn)
        ro[0][...] = ri[1][...] + _rms_bwd(xn, r, dh * fi[0][...])

    return _rowwise(fn, name, t_dim, ROW_TILE, [(x, d, 0), (dxo, d, 0)] + [(a, d, 0) for a in dhs], [g],
                    [(d, F32)], [((1, d), F32)])


def _swiglu_fwd(gu, name):
    t_dim, f2 = gu.shape
    f = f2 // 2

    def fn(i, ri, fi, ro, ao):
        gate, up = ri[0][:, :f].astype(F32), ri[0][:, f:].astype(F32)
        ro[0][...] = (gate * _sigmoid(gate) * up).astype(BF16)

    return _rowwise(fn, name, t_dim, ROW_TILE_WIDE, [(gu, f2, 0)], [], [(f, BF16)])[0]


def _swiglu_bwd(gu, dact, name):
    t_dim, f2 = gu.shape
    f = f2 // 2

    def fn(i, ri, fi, ro, ao):
        gate, up = ri[0][:, :f].astype(F32), ri[0][:, f:].astype(F32)
        da = ri[1][...].astype(F32)
        s = _sigmoid(gate)
        ro[0][:, :f] = (da * up * (s * (1.0 + gate * (1.0 - s)))).astype(BF16)
        ro[0][:, f:] = (da * (gate * s)).astype(BF16)

    return _rowwise(fn, name, t_dim, ROW_TILE_WIDE, [(gu, f2, 0), (dact, f, 0)], [], [(f2, BF16)])[0]


def _loss_fwd_bwd(y, target, name):
    t_dim, d = y.shape

    def fn(i, ri, fi, ro, ao):
        err = ri[0][...] - ri[1][...]
        ao[0][...] += _colsum(jnp.sum(err * err, axis=1, keepdims=True)) * (0.5 / d)
        ro[0][...] = err * (1.0 / d)

    return _rowwise(fn, name, t_dim, ROW_TILE, [(y, d, 0), (target, d, 0)], [], [(d, F32)], [((1, 1), F32)])


def _ln_stats(v):
    mu = jnp.mean(v, axis=-1, keepdims=True)
    xc = v - mu
    rstd = lax.rsqrt(jnp.mean(xc * xc, axis=-1, keepdims=True) + EPS)
    return xc * rstd, rstd


def _tril_mask(upper=False):
    row = lax.broadcasted_iota(jnp.int32, (CHUNK, CHUNK), 0)
    col = lax.broadcasted_iota(jnp.int32, (CHUNK, CHUNK), 1)
    return row <= col if upper else row >= col


def _gm_fwd(z_a, ln_g, ln_b, w_s, b_s, gmw, name):
    t_dim = z_a.shape[0]
    gw = gmw // GROUPS
    tr = min(GM_TILE, t_dim)

    def fn(i, ri, fi, ro, ao):
        lng, lnb, ws_ref, bs_ref = fi
        u = _gelu(ri[0][...].astype(F32))
        vhat, _ = _ln_stats(_gelu(ri[1][...].astype(F32)))
        vn = (vhat * lng[...] + lnb[...]).astype(BF16)
        mask = _tril_mask()
        for g in range(GROUPS):
            wg = jnp.where(mask, ws_ref[g], 0.0).astype(BF16)
            for c in range(tr // CHUNK):
                rows, cols = slice(c * CHUNK, (c + 1) * CHUNK), slice(g * gw, (g + 1) * gw)
                s = jnp.dot(wg, vn[rows, cols], preferred_element_type=F32) + bs_ref[g]
                ro[0][rows, cols] = (u[rows, cols] * s).astype(BF16)

    return _rowwise(fn, name, t_dim, tr, [(z_a, gmw, 0), (z_a, gmw, 1)], [ln_g, ln_b, w_s, b_s], [(gmw, BF16)])[0]


def _gm_bwd(z_a, do_a, ln_g, ln_b, w_s, w_s_t, b_s, gmw, name):
    t_dim = z_a.shape[0]
    gw = gmw // GROUPS
    tr = min(GM_TILE, t_dim)

    def fn(i, ri, fi, ro, ao):
        lng, lnb, ws_ref, wst_ref, bs_ref = fi
        d_lng, d_lnb, d_ws, d_bs = ao
        a_u, a_v, do = ri[0][...].astype(F32), ri[1][...].astype(F32), ri[2][...]
        u = _gelu(a_u)
        vhat, rstd = _ln_stats(_gelu(a_v))
        vn = (vhat * lng[...] + lnb[...]).astype(BF16)
        mask = _tril_mask()
        wgs = [jnp.where(mask, ws_ref[g], 0.0).astype(BF16) for g in range(GROUPS)]
        wgts = [jnp.where(_tril_mask(upper=True), wst_ref[g], 0.0).astype(BF16) for g in range(GROUPS)]
        for c in range(tr // CHUNK):
            rows = slice(c * CHUNK, (c + 1) * CHUNK)
            dvn_parts = []
            for g in range(GROUPS):
                cols = slice(g * gw, (g + 1) * gw)
                vn_blk = vn[rows, cols]
                s = jnp.dot(wgs[g], vn_blk, preferred_element_type=F32) + bs_ref[g]
                ro[0][rows, cols] = (do[rows, cols] * s * _gelu_grad(a_u[rows, cols])).astype(BF16)
                ds = do[rows, cols] * u[rows, cols]
                d_bs[g] += jnp.sum(ds, axis=1, keepdims=True)
                dsb = ds.astype(BF16)
                dw = lax.dot_general(dsb, vn_blk, _DIMS["nt"], preferred_element_type=F32)
                d_ws[g] += jnp.where(mask, dw, 0.0)
                dvn_parts.append(jnp.dot(wgts[g], dsb, preferred_element_type=F32))
            dvn = jnp.concatenate(dvn_parts, axis=1)
            vh, rs = vhat[rows], rstd[rows]
            d_lng[...] += _colsum(dvn * vh)
            d_lnb[...] += _colsum(dvn)
            dvh = dvn * lng[...]
            dv = rs * (dvh - jnp.mean(dvh, axis=-1, keepdims=True) - vh * jnp.mean(dvh * vh, axis=-1, keepdims=True))
            ro[0][rows, gmw:] = (dv * _gelu_grad(a_v[rows])).astype(BF16)

    return _rowwise(fn, name, t_dim, tr, [(z_a, gmw, 0), (z_a, gmw, 1), (do_a, gmw, 0)],
                    [ln_g, ln_b, w_s, w_s_t, b_s], [(2 * gmw, BF16)],
                    [((1, gmw), F32), ((1, gmw), F32), ((GROUPS, CHUNK, CHUNK), F32), ((GROUPS, CHUNK, 1), F32)])


def _glu_fwd(z_a, gmw, cw, name):
    t_dim = z_a.shape[0]
    cb = (2 * gmw) // cw

    def fn(i, ri, fi, ro, ao):
        ro[0][...] = ri[0][...].astype(F32) * _sigmoid(ri[1][...].astype(F32))

    return _rowwise(fn, name, t_dim, ROW_TILE, [(z_a, cw, cb), (z_a, cw, cb + 1)], [], [(cw, F32)])[0]


def _by_sublane_shift(offsets):
    groups = {}
    for o in offsets:
        groups.setdefault(o % 8, []).append(o)
    return sorted(groups.items())


def _conv_fwd(a, w, b, ln_g, ln_b, name):
    t_dim, c_dim = a.shape
    tr = min(CONV_TILE, t_dim)
    hb = tr // CONV_HALO

    def body(cur_ref, prev_ref, w_ref, b_ref, g_ref, be_ref, y_ref, o_ref, buf):
        i = pl.program_id(0)
        buf[0:CONV_HALO, :] = jnp.where(i > 0, prev_ref[...], 0.0)
        buf[CONV_HALO:, :] = cur_ref[...]
        for cs in range(c_dim // 128):
            lanes = pl.ds(cs * 128, 128)
            acc = jnp.zeros((tr, 128), F32)
            for k in range(CONV_K):
                acc = acc + w_ref[k:k + 1, lanes] * buf[pl.ds(k + 2, tr), lanes]
            y_ref[:, lanes] = acc + b_ref[:, lanes]
        n_hat, _ = _ln_stats(y_ref[...])
        n = n_hat * g_ref[...] + be_ref[...]
        o_ref[...] = (n * _sigmoid(n)).astype(BF16)

    full = lambda arr: pl.BlockSpec(arr.shape, lambda i: (0, 0))
    return pl.pallas_call(
        body, out_shape=[jax.ShapeDtypeStruct((t_dim, c_dim), F32), jax.ShapeDtypeStruct((t_dim, c_dim), BF16)],
        grid=(t_dim // tr,),
        in_specs=[pl.BlockSpec((tr, c_dim), lambda i: (i, 0)),
                  pl.BlockSpec((CONV_HALO, c_dim), lambda i: (jnp.maximum(i * hb - 1, 0), 0)),
                  full(w), full(b), full(ln_g), full(ln_b)],
        out_specs=[pl.BlockSpec((tr, c_dim), lambda i: (i, 0)), pl.BlockSpec((tr, c_dim), lambda i: (i, 0))],
        scratch_shapes=[pltpu.VMEM((tr + CONV_HALO, c_dim), F32)],
        compiler_params=_cparams(("arbitrary",)), name=name)(a, a, w, b, ln_g, ln_b)


def _conv_ln_bwd(y, do_b, ln_g, ln_b, name):
    t_dim, c_dim = y.shape

    def fn(i, ri, fi, ro, ao):
        n_hat, rstd = _ln_stats(ri[0][...])
        n = n_hat * fi[0][...] + fi[1][...]
        s = _sigmoid(n)
        dn = ri[1][...] * (s * (1.0 + n * (1.0 - s)))
        ao[0][...] += _colsum(dn * n_hat)
        ao[1][...] += _colsum(dn)
        dnh = dn * fi[0][...]
        ro[0][...] = rstd * (dnh - jnp.mean(dnh, axis=-1, keepdims=True)
                             - n_hat * jnp.mean(dnh * n_hat, axis=-1, keepdims=True))

    return _rowwise(fn, name, t_dim, ROW_TILE, [(y, c_dim, 0), (do_b, c_dim, 0)], [ln_g, ln_b], [(c_dim, F32)],
                    [((1, c_dim), F32), ((1, c_dim), F32)])


def _conv_bwd(dy, a, z_a, w, gmw, name):
    t_dim, c_dim = a.shape
    tr = min(CONV_TILE, t_dim)
    hb = tr // CONV_HALO
    n_halo = t_dim // CONV_HALO
    nb = t_dim // tr
    cb = (2 * gmw) // c_dim

    def body(dy_ref, dyn_ref, a_ref, ap_ref, val_ref, gate_ref, w_ref, dvg_ref, dw_ref, db_ref, dbuf, abuf, da_buf):
        i = pl.program_id(0)

        @pl.when(i == 0)
        def _():
            dw_ref[...] = jnp.zeros_like(dw_ref)
            db_ref[...] = jnp.zeros_like(db_ref)

        dbuf[0:tr, :] = dy_ref[...]
        dbuf[tr:, :] = jnp.where(i < nb - 1, dyn_ref[...], 0.0)
        abuf[0:CONV_HALO, :] = jnp.where(i > 0, ap_ref[...], 0.0)
        abuf[CONV_HALO:, :] = a_ref[...]
        db_ref[...] += _colsum(dy_ref[...])
        for cs in range(c_dim // 128):
            lanes = pl.ds(cs * 128, 128)
            dyc = dy_ref[:, lanes]
            acc = jnp.zeros((tr, 128), F32)
            for k in range(CONV_K):
                acc = acc + w_ref[k:k + 1, lanes] * dbuf[pl.ds(CONV_K - 1 - k, tr), lanes]
            for r, offs in _by_sublane_shift([k + 2 for k in range(CONV_K)]):
                shifted = abuf[pl.ds(r, tr + CONV_HALO - (8 if r else 0)), lanes]
                for o in offs:
                    dw_ref[o - 2:o - 1, lanes] += _colsum(dyc * shifted[o - r:o - r + tr])
            da_buf[:, lanes] = acc
        da = da_buf[...]
        s = _sigmoid(gate_ref[...].astype(F32))
        dvg_ref[:, :c_dim] = (da * s).astype(BF16)
        dvg_ref[:, c_dim:] = (da * val_ref[...].astype(F32) * s * (1.0 - s)).astype(BF16)

    row = lambda cblk: pl.BlockSpec((tr, c_dim), functools.partial(lambda c, i: (i, c), cblk))
    return pl.pallas_call(
        body,
        out_shape=[jax.ShapeDtypeStruct((t_dim, 2 * c_dim), BF16),
                   jax.ShapeDtypeStruct((CONV_HALO, c_dim), F32), jax.ShapeDtypeStruct((1, c_dim), F32)],
        grid=(nb,),
        in_specs=[row(0), pl.BlockSpec((CONV_HALO, c_dim), lambda i: (jnp.minimum((i + 1) * hb, n_halo - 1), 0)),
                  row(0), pl.BlockSpec((CONV_HALO, c_dim), lambda i: (jnp.maximum(i * hb - 1, 0), 0)),
                  row(cb), row(cb + 1), pl.BlockSpec(w.shape, lambda i: (0, 0))],
        out_specs=[pl.BlockSpec((tr, 2 * c_dim), lambda i: (i, 0)), pl.BlockSpec((CONV_HALO, c_dim), lambda i: (0, 0)),
                   pl.BlockSpec((1, c_dim), lambda i: (0, 0))],
        scratch_shapes=[pltpu.VMEM((tr + CONV_HALO, c_dim), F32), pltpu.VMEM((tr + CONV_HALO, c_dim), F32),
                        pltpu.VMEM((tr, c_dim), F32)],
        compiler_params=_cparams(("arbitrary",)), name=name)(dy, dy, a, a, z_a, z_a, w)


def _mla_norm_fwd(z_c, q_g, kv_g, qr, kvr, name):
    t_dim, cwid = z_c.shape

    def fn(i, ri, fi, ro, ao):
        cq, _ = _rms(ri[0][:, :qr])
        ckv, _ = _rms(ri[0][:, qr:qr + kvr])
        ro[0][...] = (cq * fi[0][...]).astype(BF16)
        ro[1][...] = (ckv * fi[1][...]).astype(BF16)

    return _rowwise(fn, name, t_dim, ROW_TILE, [(z_c, cwid, 0)], [q_g, kv_g], [(qr, BF16), (kvr, BF16)])


def _rope(t, cf, s1, s2):
    return t * cf + pltpu.roll(t, 96, 1) * s1 + pltpu.roll(t, 32, 1) * s2


def _rope_t(g, cf, s1, s2):
    return g * cf + pltpu.roll(g * s1, 32, 1) + pltpu.roll(g * s2, 96, 1)


def _rope_fwd(q_pre, k_nope, z_c, cf, s1, s2, rope_blk, scale, name):
    t_dim = q_pre.shape[0]

    def fn(i, ri, fi, ro, ao):
        c, a, b = ri[3][...], ri[4][...], ri[5][...]
        kt = _rope(ri[2][...], c, a, b).astype(BF16)
        for h in range(N_HEADS):
            ro[0][:, h * HQ:h * HQ + NOPE] = (ri[0][:, h * HQ:h * HQ + NOPE].astype(F32) * scale).astype(BF16)
            ro[0][:, h * HQ + NOPE:(h + 1) * HQ] = (
                _rope(ri[0][:, h * HQ + NOPE:(h + 1) * HQ].astype(F32), c, a, b) * scale).astype(BF16)
            ro[1][:, h * HQ:h * HQ + NOPE] = ri[1][:, h * NOPE:(h + 1) * NOPE].astype(BF16)
            ro[1][:, h * HQ + NOPE:(h + 1) * HQ] = kt

    return _rowwise(fn, name, t_dim, ROW_TILE_WIDE,
                    [(q_pre, N_HEADS * HQ, 0), (k_nope, N_HEADS * NOPE, 0), (z_c, 128, rope_blk),
                     (cf, 128, 0), (s1, 128, 0), (s2, 128, 0)], [],
                    [(N_HEADS * HQ, BF16), (N_HEADS * HQ, BF16)])


def _rope_bwd(dq_cat, dk_cat, cf, s1, s2, scale, name):
    t_dim = dq_cat.shape[0]

    def fn(i, ri, fi, ro, ao):
        c, a, b = ri[2][...], ri[3][...], ri[4][...]
        dkt = jnp.zeros((ri[0].shape[0], 128), F32)
        for h in range(N_HEADS):
            ro[0][:, h * HQ:h * HQ + NOPE] = (ri[0][:, h * HQ:h * HQ + NOPE] * scale).astype(BF16)
            ro[0][:, h * HQ + NOPE:(h + 1) * HQ] = _rope_t(ri[0][:, h * HQ + NOPE:(h + 1) * HQ] * scale, c, a, b).astype(BF16)
            ro[1][:, h * NOPE:(h + 1) * NOPE] = ri[1][:, h * HQ:h * HQ + NOPE].astype(BF16)
            dkt = dkt + ri[1][:, h * HQ + NOPE:(h + 1) * HQ].astype(F32)
        ro[2][...] = _rope_t(dkt, c, a, b)

    return _rowwise(fn, name, t_dim, ROW_TILE_WIDE,
                    [(dq_cat, N_HEADS * HQ, 0), (dk_cat, N_HEADS * HQ, 0), (cf, 128, 0), (s1, 128, 0), (s2, 128, 0)],
                    [], [(N_HEADS * HQ, BF16), (N_HEADS * NOPE, BF16), (128, F32)])


def _mla_norm_bwd(z_c, dcq, dckv_k, dckv_v, dkr, q_g, kv_g, qr, kvr, name):
    t_dim, cwid = z_c.shape

    def fn(i, ri, fi, ro, ao):
        cq, rq = _rms(ri[0][:, :qr])
        ckv, rkv = _rms(ri[0][:, qr:qr + kvr])
        dq = ri[1][...]
        dkv = ri[2][...] + ri[3][...]
        ao[0][...] += _colsum(dq * cq)
        ao[1][...] += _colsum(dkv * ckv)
        ro[0][:, :qr] = _rms_bwd(cq, rq, dq * fi[0][...]).astype(BF16)
        ro[0][:, qr:qr + kvr] = _rms_bwd(ckv, rkv, dkv * fi[1][...]).astype(BF16)
        ro[0][:, qr + kvr:] = ri[4][...].astype(BF16)

    return _rowwise(fn, name, t_dim, ROW_TILE,
                    [(z_c, cwid, 0), (dcq, qr, 0), (dckv_k, kvr, 0), (dckv_v, kvr, 0), (dkr, 128, 0)], [q_g, kv_g],
                    [(cwid, BF16)], [((1, qr), F32), ((1, kvr), F32)])


def _causal_pairs(n, by_key):
    if by_key:
        pairs = [(i, j) for j in range(n) for i in range(j, n)]
    else:
        pairs = [(i, j) for i in range(n) for j in range(i + 1)]
    return (np.array([p[0] for p in pairs], np.int32), np.array([p[1] for p in pairs], np.int32))


def _attn_fwd(q, k, v, name, carry=None):
    t_dim = q.shape[0]
    tq = min(ATT_TILE, t_dim)
    nh = ATT_FWD_HEADS
    qi, kj = _causal_pairs(t_dim // tq, by_key=False)
    n_steps = len(qi)
    c_arrays, c_exchange = carry if carry else ((), False)
    nc = len(c_arrays)

    def body(qi_ref, kj_ref, q_ref, k_ref, v_ref, *rest):
        c_ins, (o_ref, lse_ref), c_outs = rest[:nc], rest[nc:nc + 2], rest[nc + 2:2 * nc + 2]
        m_sc, l_sc, acc_sc = rest[2 * nc + 2:2 * nc + 5]
        sems = rest[2 * nc + 5:]
        s_id = pl.program_id(1)
        i, j = qi_ref[s_id], kj_ref[s_id]
        if nc:
            @pl.when(jnp.logical_and(pl.program_id(0) == 0, s_id == 0))
            def _():
                _comm_start(c_ins, c_outs, *sems, c_exchange)

        @pl.when(j == 0)
        def _():
            m_sc[...] = jnp.full_like(m_sc, NEG)
            l_sc[...] = jnp.zeros_like(l_sc)
            acc_sc[...] = jnp.zeros_like(acc_sc)

        def step(masked):
            for hh in range(nh):
                s = lax.dot_general(q_ref[:, hh * HQ:(hh + 1) * HQ], k_ref[:, hh * HQ:(hh + 1) * HQ], _DIMS["nt"],
                                    preferred_element_type=F32)
                if masked:
                    row = lax.broadcasted_iota(jnp.int32, (tq, tq), 0)
                    col = lax.broadcasted_iota(jnp.int32, (tq, tq), 1)
                    s = jnp.where(col <= row, s, NEG)
                m_prev = m_sc[hh]
                m_new = jnp.maximum(m_prev, jnp.max(s, axis=1, keepdims=True))
                alpha = jnp.exp(m_prev - m_new)
                p = jnp.exp(s - m_new)
                l_sc[hh] = alpha * l_sc[hh] + jnp.sum(p, axis=1, keepdims=True)
                acc_sc[hh] = alpha * acc_sc[hh] + jnp.dot(p.astype(BF16), v_ref[:, hh * VDIM:(hh + 1) * VDIM],
                                                          preferred_element_type=F32)
                m_sc[hh] = m_new

        @pl.when(j < i)
        def _():
            step(False)

        @pl.when(j == i)
        def _():
            step(True)
            for hh in range(nh):
                o_ref[:, hh * VDIM:(hh + 1) * VDIM] = (acc_sc[hh] / l_sc[hh]).astype(BF16)
                lse_ref[hh] = m_sc[hh] + jnp.log(l_sc[hh])

        if nc:
            @pl.when(jnp.logical_and(pl.program_id(0) == N_HEADS // nh - 1, s_id == n_steps - 1))
            def _():
                _comm_wait(c_ins, c_outs, *sems, c_exchange)

    any_spec = pl.BlockSpec(memory_space=pl.ANY)
    grid_spec = pltpu.PrefetchScalarGridSpec(
        num_scalar_prefetch=2, grid=(N_HEADS // nh, n_steps),
        in_specs=[pl.BlockSpec((tq, nh * HQ), lambda h, s, qi, kj: (qi[s], h)),
                  pl.BlockSpec((tq, nh * HQ), lambda h, s, qi, kj: (kj[s], h)),
                  pl.BlockSpec((tq, nh * VDIM), lambda h, s, qi, kj: (kj[s], h))] + [any_spec] * nc,
        out_specs=[pl.BlockSpec((tq, nh * VDIM), lambda h, s, qi, kj: (qi[s], h)),
                   pl.BlockSpec((nh, tq, 1), lambda h, s, qi, kj: (h, qi[s], 0))] + [any_spec] * nc,
        scratch_shapes=[pltpu.VMEM((nh, tq, 1), F32), pltpu.VMEM((nh, tq, 1), F32), pltpu.VMEM((nh, tq, VDIM), F32)]
        + _comm_sems(nc))
    return pl.pallas_call(
        body, grid_spec=grid_spec,
        out_shape=[jax.ShapeDtypeStruct((t_dim, N_HEADS * VDIM), BF16), jax.ShapeDtypeStruct((N_HEADS, t_dim, 1), F32)]
        + _comm_out_shape(c_arrays, c_exchange),
        compiler_params=_cparams(("arbitrary", "arbitrary")), name=name)(
            jnp.asarray(qi), jnp.asarray(kj), q, k, v, *c_arrays)


def _attn_delta(do, o, name):
    t_dim = do.shape[0]
    tr = min(ATT_TILE, t_dim)

    def body(do_ref, o_ref, d_ref):
        d_ref[0] = jnp.sum(do_ref[...].astype(F32) * o_ref[...].astype(F32), axis=1, keepdims=True)

    return pl.pallas_call(
        body, out_shape=jax.ShapeDtypeStruct((N_HEADS, t_dim, 1), F32), grid=(N_HEADS, t_dim // tr),
        in_specs=[pl.BlockSpec((tr, VDIM), lambda h, i: (i, h)), pl.BlockSpec((tr, VDIM), lambda h, i: (i, h))],
        out_specs=pl.BlockSpec((1, tr, 1), lambda h, i: (h, i, 0)),
        compiler_params=_cparams(("parallel", "parallel")), name=name)(do, o)


def _attn_bwd(q, k, v, do, lse_row, delta_row, name, carry=None):
    t_dim = q.shape[0]
    tq = min(ATT_TILE, t_dim)
    nq = t_dim // tq
    nh = ATT_BWD_HEADS
    qi, kj = _causal_pairs(nq, by_key=True)
    n_steps = len(qi)
    c_arrays, c_exchange = carry if carry else ((), False)
    nc = len(c_arrays)

    def body(qi_ref, kj_ref, q_ref, k_ref, v_ref, do_ref, lse_ref, dl_ref, *rest):
        c_ins, (dq_ref, dk_ref, dv_ref), c_outs = rest[:nc], rest[nc:nc + 3], rest[nc + 3:2 * nc + 3]
        dk_sc, dv_sc = rest[2 * nc + 3:2 * nc + 5]
        sems = rest[2 * nc + 5:]
        s_id = pl.program_id(1)
        i, j = qi_ref[s_id], kj_ref[s_id]
        if nc:
            @pl.when(jnp.logical_and(pl.program_id(0) == 0, s_id == 0))
            def _():
                _comm_start(c_ins, c_outs, *sems, c_exchange)

        @pl.when(s_id == 0)
        def _():
            dq_ref[...] = jnp.zeros_like(dq_ref)

        rows = pl.ds(pl.multiple_of(i * tq, tq), tq)

        def step(masked):
            for hh in range(nh):
                qh, kh = q_ref[:, hh * HQ:(hh + 1) * HQ], k_ref[:, hh * HQ:(hh + 1) * HQ]
                vh, doh = v_ref[:, hh * VDIM:(hh + 1) * VDIM], do_ref[:, hh * VDIM:(hh + 1) * VDIM]
                s_t = lax.dot_general(kh, qh, _DIMS["nt"], preferred_element_type=F32)
                p_t = jnp.exp(s_t - lse_ref[hh])
                if masked:
                    row = lax.broadcasted_iota(jnp.int32, (tq, tq), 0)
                    col = lax.broadcasted_iota(jnp.int32, (tq, tq), 1)
                    p_t = jnp.where(row <= col, p_t, 0.0)
                dv = jnp.dot(p_t.astype(BF16), doh, preferred_element_type=F32)
                dp_t = lax.dot_general(vh, doh, _DIMS["nt"], preferred_element_type=F32)
                ds_t = (p_t * (dp_t - dl_ref[hh])).astype(BF16)
                dk = jnp.dot(ds_t, qh, preferred_element_type=F32)
                if masked:
                    dv_sc[hh] = dv
                    dk_sc[hh] = dk
                else:
                    dv_sc[hh] += dv
                    dk_sc[hh] += dk
                dq_ref[rows, hh * HQ:(hh + 1) * HQ] += lax.dot_general(ds_t, kh, _DIMS["tn"], preferred_element_type=F32)

        @pl.when(i == j)
        def _():
            step(True)

        @pl.when(i != j)
        def _():
            step(False)

        @pl.when(i == nq - 1)
        def _():
            for hh in range(nh):
                dk_ref[:, hh * HQ:(hh + 1) * HQ] = dk_sc[hh].astype(BF16)
                dv_ref[:, hh * VDIM:(hh + 1) * VDIM] = dv_sc[hh].astype(BF16)

        if nc:
            @pl.when(jnp.logical_and(pl.program_id(0) == N_HEADS // nh - 1, s_id == n_steps - 1))
            def _():
                _comm_wait(c_ins, c_outs, *sems, c_exchange)

    any_spec = pl.BlockSpec(memory_space=pl.ANY)
    grid_spec = pltpu.PrefetchScalarGridSpec(
        num_scalar_prefetch=2, grid=(N_HEADS // nh, n_steps),
        in_specs=[pl.BlockSpec((tq, nh * HQ), lambda h, s, qi, kj: (qi[s], h)),
                  pl.BlockSpec((tq, nh * HQ), lambda h, s, qi, kj: (kj[s], h)),
                  pl.BlockSpec((tq, nh * VDIM), lambda h, s, qi, kj: (kj[s], h)),
                  pl.BlockSpec((tq, nh * VDIM), lambda h, s, qi, kj: (qi[s], h)),
                  pl.BlockSpec((nh, 1, tq), lambda h, s, qi, kj: (h, 0, qi[s])),
                  pl.BlockSpec((nh, 1, tq), lambda h, s, qi, kj: (h, 0, qi[s]))] + [any_spec] * nc,
        out_specs=[pl.BlockSpec((t_dim, nh * HQ), lambda h, s, qi, kj: (0, h)),
                   pl.BlockSpec((tq, nh * HQ), lambda h, s, qi, kj: (kj[s], h)),
                   pl.BlockSpec((tq, nh * VDIM), lambda h, s, qi, kj: (kj[s], h))] + [any_spec] * nc,
        scratch_shapes=[pltpu.VMEM((nh, tq, HQ), F32), pltpu.VMEM((nh, tq, VDIM), F32)] + _comm_sems(nc))
    return pl.pallas_call(
        body, grid_spec=grid_spec,
        out_shape=[jax.ShapeDtypeStruct((t_dim, N_HEADS * HQ), F32), jax.ShapeDtypeStruct((t_dim, N_HEADS * HQ), BF16),
                   jax.ShapeDtypeStruct((t_dim, N_HEADS * VDIM), BF16)] + _comm_out_shape(c_arrays, c_exchange),
        compiler_params=_cparams(("arbitrary", "arbitrary")), name=name)(
            jnp.asarray(qi), jnp.asarray(kj), q, k, v, do, lse_row, delta_row, *c_arrays)


def _merge_fwd(z_g, y_a, y_b, y_c, name):
    t_dim, d = y_a.shape

    def fn(i, ri, fi, ro, ao):
        acc = _sigmoid(ri[0][:, :d].astype(F32)) * ri[1][...].astype(F32)
        acc = acc + _sigmoid(ri[0][:, d:2 * d].astype(F32)) * ri[2][...].astype(F32)
        acc = acc + _sigmoid(ri[0][:, 2 * d:].astype(F32)) * ri[3][...].astype(F32)
        ro[0][...] = acc.astype(BF16)

    return _rowwise(fn, name, t_dim, ROW_TILE_WIDE, [(z_g, 3 * d, 0), (y_a, d, 0), (y_b, d, 0), (y_c, d, 0)], [],
                    [(d, BF16)])[0]


def _merge_bwd(z_g, y_a, y_b, y_c, dmerged, name):
    t_dim, d = y_a.shape

    def fn(i, ri, fi, ro, ao):
        dm = ri[4][...]
        for q in range(3):
            s = _sigmoid(ri[0][:, q * d:(q + 1) * d].astype(F32))
            ro[q][...] = (s * dm).astype(BF16)
            ro[3][:, q * d:(q + 1) * d] = (dm * ri[1 + q][...].astype(F32) * s * (1.0 - s)).astype(BF16)

    return _rowwise(fn, name, t_dim, ROW_TILE_WIDE,
                    [(z_g, 3 * d, 0), (y_a, d, 0), (y_b, d, 0), (y_c, d, 0), (dmerged, d, 0)], [],
                    [(d, BF16), (d, BF16), (d, BF16), (3 * d, BF16)])


def _adam(w, parts, m, v, name):
    r_dim, c_dim = w.shape
    limit = max(16, ADAM_BLOCK_ELEMS // c_dim // 16 * 16)
    tr = r_dim
    if r_dim > limit:
        tr = next((t for t in range(limit, 15, -16) if r_dim % t == 0), r_dim)

    def body(w_ref, p_ref, m_ref, v_ref, g_out, d_out, m_out, v_out):
        g = p_ref[0].astype(F32)
        for s in range(1, N_DEV):
            g = g + p_ref[s].astype(F32)
        m_new = ADAM_B1 * m_ref[...] + (1.0 - ADAM_B1) * g
        v_new = ADAM_B2 * v_ref[...] + (1.0 - ADAM_B2) * (g * g)
        m_hat = m_new / (1.0 - ADAM_B1 ** ADAM_STEP)
        v_hat = v_new / (1.0 - ADAM_B2 ** ADAM_STEP)
        g_out[...] = g
        d_out[...] = -ADAM_LR * (m_hat / (jnp.sqrt(v_hat) + ADAM_EPS) + ADAM_WD * w_ref[...])
        m_out[...] = m_new
        v_out[...] = v_new

    blk = pl.BlockSpec((tr, c_dim), lambda i: (i, 0))
    return pl.pallas_call(
        body, out_shape=[jax.ShapeDtypeStruct((r_dim, c_dim), F32)] * 4, grid=(r_dim // tr,),
        in_specs=[blk, pl.BlockSpec((N_DEV, tr, c_dim), lambda i: (0, i, 0)), blk, blk], out_specs=[blk] * 4,
        compiler_params=_cparams(("parallel",)), name=name)(w, parts, m, v)


def _me_and_peers():
    x, y, c = lax.axis_index("x"), lax.axis_index("y"), lax.axis_index("c")
    me = 4 * x + 2 * y + c
    peers = []
    for k in range(1, N_DEV):
        px, py, pc = x ^ (k >> 2), y ^ ((k >> 1) & 1), c ^ (k & 1)
        peers.append(((px, py, pc), 4 * px + 2 * py + pc))
    return me, peers


def _comm_copies(ins, outs, send_sems, recv_sems, local_sems, exchange, with_arrivals):
    me, peers = _me_and_peers()

    def src(w, dest_idx):
        return ins[w].at[dest_idx] if exchange else ins[w]

    def remote(w, k, dev, src_ref, dst_ref):
        return pltpu.make_async_remote_copy(
            src_ref=src_ref, dst_ref=dst_ref, send_sem=send_sems.at[w * (N_DEV - 1) + k],
            recv_sem=recv_sems.at[w * (N_DEV - 1) + k], device_id=dev, device_id_type=pl.DeviceIdType.MESH)

    local = [pltpu.make_async_copy(src(w, me), outs[w].at[me], local_sems.at[w]) for w in range(len(ins))]
    sends, arrivals = [], []
    for w in range(len(ins)):
        for k, (dev, idx) in enumerate(peers):
            sends.append(remote(w, k, dev, src(w, idx), outs[w].at[me]))
            if with_arrivals:
                arrivals.append(remote(w, k, dev, src(w, idx), outs[w].at[idx]))
    return local, sends, arrivals


def _comm_start(ins, outs, send_sems, recv_sems, local_sems, exchange):
    local, sends, _ = _comm_copies(ins, outs, send_sems, recv_sems, local_sems, exchange, False)
    for cp in local + sends:
        cp.start()


def _comm_wait(ins, outs, send_sems, recv_sems, local_sems, exchange):
    local, sends, arrivals = _comm_copies(ins, outs, send_sems, recv_sems, local_sems, exchange, True)
    for cp in arrivals:
        cp.wait_recv()
    for cp in sends:
        cp.wait_send()
    for cp in local:
        cp.wait()


def _comm_sems(n):
    if not n:
        return []
    return [pltpu.SemaphoreType.DMA((n * (N_DEV - 1),)), pltpu.SemaphoreType.DMA((n * (N_DEV - 1),)),
            pltpu.SemaphoreType.DMA((n,))]


def _comm_out_shape(arrays, exchange):
    return [jax.ShapeDtypeStruct(a.shape if exchange else (N_DEV,) + a.shape, a.dtype) for a in arrays]


def _comm(arrays, exchange, name):
    n = len(arrays)
    hbm = pl.BlockSpec(memory_space=pltpu.HBM)

    def body(*refs):
        ins, outs, sems = refs[:n], refs[n:2 * n], refs[2 * n:]
        _comm_start(ins, outs, *sems, exchange)
        _comm_wait(ins, outs, *sems, exchange)

    return pl.pallas_call(
        body, out_shape=_comm_out_shape(arrays, exchange), in_specs=[hbm] * n, out_specs=[hbm] * n,
        scratch_shapes=_comm_sems(n), name=name)(*arrays)


def _unshard(name, g):
    if name in COL_SHARDED:
        return jnp.transpose(g, (1, 0, 2)).reshape(g.shape[1], N_DEV * g.shape[2])
    return g.reshape(N_DEV * g.shape[1], g.shape[2])


def _to_shards(name, full):
    if name in COL_SHARDED:
        r, c = full.shape
        return jnp.transpose(full.reshape(r, N_DEV, c // N_DEV), (1, 0, 2))
    return full.reshape(N_DEV, full.shape[0] // N_DEV, full.shape[1])


def _mmc(a, b, mode, out_dtype, name, carry):
    if carry:
        return _mm(a, b, mode, out_dtype, name + "_comm", carry)
    return _mm(a, b, mode, out_dtype, name), []


def _ffn_fwd(x, p, tag, carry_in=None, carry_out=None):
    h = _norm_fwd(x, p["norm_pre"], f"{tag}_norm")
    gu, got_in = _mmc(h, p["w_in"], "nn", BF16, f"{tag}_in", carry_in)
    act = _swiglu_fwd(gu, f"{tag}_act")
    y, got_out = _mmc(act, p["w_out"], "nn", F32, f"{tag}_out", carry_out)
    x_new = _resid_fwd(x, y, p["norm_post"], 0.5, f"{tag}_resid")
    return x_new, dict(x=x, h=h, gu=gu, act=act, y=y), got_in, got_out


def _ffn_bwd(dxo, s, p, tag, carry_out=None, carry_in=None):
    dy, dg_post = _resid_bwd(s["y"], dxo, p["norm_post"], 0.5, f"{tag}_resid_bwd")
    dact, got_out = _mmc(dy, p["w_out"], "nt", BF16, f"{tag}_out_dx", carry_out)
    dw_out = _mm(s["act"], dy, "tn", BF16, f"{tag}_out_dw", shard_out="row")
    dgu = _swiglu_bwd(s["gu"], dact, f"{tag}_act_bwd")
    dh, got_in = _mmc(dgu, p["w_in"], "nt", F32, f"{tag}_in_dx", carry_in)
    dw_in = _mm(s["h"], dgu, "tn", BF16, f"{tag}_in_dw", shard_out="col")
    dx, dg_pre = _norm_bwd(s["x"], [dh], dxo, p["norm_pre"], f"{tag}_norm_bwd")
    return dx, dict(norm_pre=dg_pre, norm_post=dg_post, w_in=dw_in, w_out=dw_out), got_out, got_in


def _mixer_fwd(x, p, rope_tabs, dims, carry):
    gmw, cw, qr, kvr = dims["gmw"], dims["cw"], dims["qr"], dims["kvr"]
    cf, s1, s2 = rope_tabs
    scale = (NOPE + ROPE) ** -0.5
    h = _norm_fwd(x, p["norm_pre"], "mix_norm")
    z_a = _mm(h, p["w_a"], "nn", BF16, "mix_in_a")
    z_c = _mm(h, p["w_c"], "nn", F32, "mix_in_c")
    z_g = _mm(h, p["w_g"], "nn", BF16, "mix_in_g")
    o_a = _gm_fwd(z_a, p["gm_ln_g"], p["gm_ln_b"], p["gm_w_s"], p["gm_b_s"], gmw, "gm_fwd")
    a = _glu_fwd(z_a, gmw, cw, "glu_fwd")
    y_conv, o_b = _conv_fwd(a, p["conv_w"], p["conv_b"], p["conv_ln_g"], p["conv_ln_b"], "conv_fwd")
    cqn, ckvn = _mla_norm_fwd(z_c, p["q_norm"], p["kv_norm"], qr, kvr, "mla_norm")
    q_pre = _mm(cqn, p["w_uq"], "nn", BF16, "mla_uq")
    k_nope = _mm(ckvn, p["w_uk"], "nn", BF16, "mla_uk")
    v = _mm(ckvn, p["w_uv"], "nn", BF16, "mla_uv")
    q_cat, k_cat = _rope_fwd(q_pre, k_nope, z_c, cf, s1, s2, (qr + kvr) // 128, scale, "rope_fwd")
    o_c, lse, *carried = _attn_fwd(q_cat, k_cat, v, "attn_fwd_comm" if carry else "attn_fwd", carry)
    y_a = _mm(o_a, p["wb_a"], "nn", BF16, "branch_a")
    y_b = _mm(o_b, p["wb_b"], "nn", BF16, "branch_b")
    y_c = _mm(o_c, p["wb_c"], "nn", BF16, "branch_c")
    merged = _merge_fwd(z_g, y_a, y_b, y_c, "merge_fwd")
    m = _mm(merged, p["w_out"], "nn", F32, "mix_out")
    x_new = _resid_fwd(x, m, p["norm_post"], 1.0, "mix_resid")
    saved = dict(x=x, h=h, z_a=z_a, z_c=z_c, z_g=z_g, o_a=o_a, a=a, y_conv=y_conv, o_b=o_b, cqn=cqn, ckvn=ckvn,
                 v=v, q_cat=q_cat, k_cat=k_cat, o_c=o_c, lse=lse, y_a=y_a, y_b=y_b, y_c=y_c, merged=merged, m=m)
    return x_new, saved, carried


def _mixer_bwd(dxo, s, p, rope_tabs, dims, carry):
    gmw, cw, qr, kvr = dims["gmw"], dims["cw"], dims["qr"], dims["kvr"]
    cf, s1, s2 = rope_tabs
    scale = (NOPE + ROPE) ** -0.5
    t_dim = dxo.shape[0]
    g = {}
    dm, g["norm_post"] = _resid_bwd(s["m"], dxo, p["norm_post"], 1.0, "mix_resid_bwd")
    dmerged = _mm(dm, p["w_out"], "nt", F32, "mix_out_dx")
    g["w_out"] = _mm(s["merged"], dm, "tn", F32, "mix_out_dw")
    dy_a, dy_b, dy_c, dz_g = _merge_bwd(s["z_g"], s["y_a"], s["y_b"], s["y_c"], dmerged, "merge_bwd")
    do_a = _mm(dy_a, p["wb_a"], "nt", F32, "branch_a_dx")
    do_b = _mm(dy_b, p["wb_b"], "nt", F32, "branch_b_dx")
    do_c = _mm(dy_c, p["wb_c"], "nt", BF16, "branch_c_dx")
    g["w_branch"] = jnp.concatenate([_mm(s["o_a"], dy_a, "tn", F32, "branch_a_dw"),
                                     _mm(s["o_b"], dy_b, "tn", F32, "branch_b_dw"),
                                     _mm(s["o_c"], dy_c, "tn", F32, "branch_c_dw")], axis=0)
    delta = _attn_delta(do_c, s["o_c"], "attn_delta")
    dq_cat, dk_cat, dv, *carried = _attn_bwd(s["q_cat"], s["k_cat"], s["v"], do_c, s["lse"].reshape(N_HEADS, 1, t_dim),
                                             delta.reshape(N_HEADS, 1, t_dim),
                                             "attn_bwd_comm" if carry else "attn_bwd", carry)
    dq_pre, dk_nope, dkr = _rope_bwd(dq_cat, dk_cat, cf, s1, s2, scale, "rope_bwd")
    dcq = _mm(dq_pre, p["w_uq"], "nt", F32, "mla_uq_dx")
    g["w_uq"] = _mm(s["cqn"], dq_pre, "tn", F32, "mla_uq_dw")
    dckv_k = _mm(dk_nope, p["w_uk"], "nt", F32, "mla_uk_dx")
    dckv_v = _mm(dv, p["w_uv"], "nt", F32, "mla_uv_dx")
    g["w_uk"] = _mm(s["ckvn"], dk_nope, "tn", F32, "mla_uk_dw")
    g["w_uv"] = _mm(s["ckvn"], dv, "tn", F32, "mla_uv_dw")
    dz_c, g["q_norm"], g["kv_norm"] = _mla_norm_bwd(s["z_c"], dcq, dckv_k, dckv_v, dkr, p["q_norm"], p["kv_norm"],
                                                    qr, kvr, "mla_norm_bwd")
    dy_conv, g["conv_ln_g"], g["conv_ln_b"] = _conv_ln_bwd(s["y_conv"], do_b, p["conv_ln_g"], p["conv_ln_b"], "conv_ln_bwd")
    dz_cv, g["conv_w"], g["conv_b"] = _conv_bwd(dy_conv, s["a"], s["z_a"], p["conv_w"], gmw, "conv_bwd")
    dz_gm, g["gm_ln_g"], g["gm_ln_b"], g["gm_w_s"], g["gm_b_s"] = _gm_bwd(
        s["z_a"], do_a, p["gm_ln_g"], p["gm_ln_b"], p["gm_w_s"], p["gm_w_s_t"], p["gm_b_s"], gmw, "gm_bwd")
    dh_gm = _mm(dz_gm, p["w_gm"], "nt", F32, "mix_in_gm_dx")
    dh_cv = _mm(dz_cv, p["w_cv"], "nt", F32, "mix_in_cv_dx")
    dh_c = _mm(dz_c, p["w_c"], "nt", F32, "mix_in_c_dx")
    dh_g = _mm(dz_g, p["w_g"], "nt", F32, "mix_in_g_dx")
    g["w_gm"] = _mm(s["h"], dz_gm, "tn", F32, "mix_in_gm_dw")
    g["w_cv"] = _mm(s["h"], dz_cv, "tn", F32, "mix_in_cv_dw")
    g["w_c"] = _mm(s["h"], dz_c, "tn", F32, "mix_in_c_dw")
    g["w_g"] = _mm(s["h"], dz_g, "tn", F32, "mix_in_g_dw")
    dx, g["norm_pre"] = _norm_bwd(s["x"], [dh_gm, dh_cv, dh_c, dh_g], dxo, p["norm_pre"], "mix_norm_bwd")
    return dx, g, carried


def _ffn_params(full, small, k, l):
    row = lambda a: a[l][None, :]
    return dict(norm_pre=row(small[f"{k}_norm_pre"]), norm_post=row(small[f"{k}_norm_post"]),
                w_in=full[f"{k}_w_in"], w_out=full[f"{k}_w_out"])


def _mix_params(full, small, conv_w_full, dims, l):
    gmw, cw, qr, kvr, d = dims["gmw"], dims["cw"], dims["qr"], dims["kvr"], dims["d"]
    row = lambda a: a[l][None, :]
    w_in = full["mix_w_in"]
    a_end = 2 * gmw + 2 * cw
    c_end = a_end + qr + kvr + ROPE
    w_c = jnp.concatenate([w_in[:, a_end:c_end], jnp.zeros((d, 128 - ROPE), w_in.dtype)], axis=1)
    w_uq = full["mla_w_uq"].reshape(qr, N_HEADS, NOPE + ROPE)
    w_uq = jnp.concatenate([w_uq, jnp.zeros((qr, N_HEADS, HQ - NOPE - ROPE), w_uq.dtype)], axis=2).reshape(qr, N_HEADS * HQ)
    w_ukv = full["mla_w_ukv"].reshape(kvr, N_HEADS, NOPE + VDIM)
    w_b = full["mix_w_branch"]
    w_s = small["gm_w_s"][l]
    return dict(norm_pre=row(small["mix_norm_pre"]), norm_post=row(small["mix_norm_post"]),
                w_a=w_in[:, :a_end], w_gm=w_in[:, :2 * gmw], w_cv=w_in[:, 2 * gmw:a_end], w_c=w_c, w_g=w_in[:, c_end:],
                gm_ln_g=row(small["gm_ln_g"]), gm_ln_b=row(small["gm_ln_b"]), gm_w_s=w_s,
                gm_w_s_t=jnp.transpose(w_s, (0, 2, 1)), gm_b_s=small["gm_b_s"][l][:, :, None],
                conv_w=jnp.concatenate([conv_w_full, jnp.zeros((CONV_HALO - CONV_K, cw), F32)], axis=0),
                conv_b=row(small["conv_b"]), conv_ln_g=row(small["conv_ln_g"]), conv_ln_b=row(small["conv_ln_b"]),
                q_norm=row(small["mla_q_norm"]), kv_norm=row(small["mla_kv_norm"]),
                w_uq=w_uq, w_uk=w_ukv[:, :, :NOPE].reshape(kvr, N_HEADS * NOPE),
                w_uv=w_ukv[:, :, NOPE:].reshape(kvr, N_HEADS * VDIM),
                wb_a=w_b[:gmw], wb_b=w_b[gmw:gmw + cw], wb_c=w_b[gmw + cw:], w_out=full["mix_w_out"])


def _mix_big_grads(gm, dims):
    qr, kvr = dims["qr"], dims["kvr"]
    return {
        "mix_w_in": jnp.concatenate([gm["w_gm"], gm["w_cv"], gm["w_c"][:, :qr + kvr + ROPE], gm["w_g"]], axis=1),
        "mla_w_uq": gm["w_uq"].reshape(qr, N_HEADS, HQ)[:, :, :NOPE + ROPE].reshape(qr, N_HEADS * (NOPE + ROPE)),
        "mla_w_ukv": jnp.concatenate([gm["w_uk"].reshape(kvr, N_HEADS, NOPE), gm["w_uv"].reshape(kvr, N_HEADS, VDIM)],
                                     axis=2).reshape(kvr, N_HEADS * (NOPE + VDIM)),
        "mix_w_branch": gm["w_branch"], "mix_w_out": gm["w_out"], "conv_w": gm["conv_w"][:CONV_K],
    }


def _small_grads(g1, gm, g2):
    return {
        "ffn1_norm_pre": g1["norm_pre"][0], "ffn1_norm_post": g1["norm_post"][0],
        "ffn2_norm_pre": g2["norm_pre"][0], "ffn2_norm_post": g2["norm_post"][0],
        "mix_norm_pre": gm["norm_pre"][0], "mix_norm_post": gm["norm_post"][0],
        "gm_ln_g": gm["gm_ln_g"][0], "gm_ln_b": gm["gm_ln_b"][0], "gm_w_s": gm["gm_w_s"], "gm_b_s": gm["gm_b_s"][:, :, 0],
        "conv_b": gm["conv_b"][0], "conv_ln_g": gm["conv_ln_g"][0], "conv_ln_b": gm["conv_ln_b"][0],
        "mla_q_norm": gm["q_norm"][0], "mla_kv_norm": gm["kv_norm"][0],
    }


def _rope_tables(positions):
    inv_freq = ROPE_THETA ** (-jnp.arange(0, ROPE, 2, dtype=F32) / ROPE)
    ang = positions.astype(F32)[:, None] * inv_freq
    cos, sin = jnp.cos(ang), jnp.sin(ang)
    z = lambda w: jnp.zeros((positions.shape[0], w), F32)
    return (jnp.concatenate([cos, cos, z(64)], axis=1), jnp.concatenate([-sin, z(96)], axis=1),
            jnp.concatenate([z(32), sin, z(64)], axis=1))


def _pad_rows(flat, mult):
    n = flat.shape[0]
    pad = (-n) % mult
    return jnp.concatenate([flat, jnp.zeros((pad,), flat.dtype)]) if pad else flat


def kernel(x, positions, ffn1_norm_pre, ffn1_norm_post, ffn1_w_in, ffn1_w_out, mix_norm_pre, mix_norm_post, mix_w_in, gm_ln_g, gm_ln_b, gm_w_s, gm_b_s, conv_w, conv_b, conv_ln_g, conv_ln_b, mla_q_norm, mla_w_uq, mla_kv_norm, mla_w_ukv, mix_w_branch, mix_w_out, ffn2_norm_pre, ffn2_norm_post, ffn2_w_in, ffn2_w_out, loss_target, m_ffn1_norm_pre, m_ffn1_norm_post, m_ffn1_w_in, m_ffn1_w_out, m_mix_norm_pre, m_mix_norm_post, m_mix_w_in, m_gm_ln_g, m_gm_ln_b, m_gm_w_s, m_gm_b_s, m_conv_w, m_conv_b, m_conv_ln_g, m_conv_ln_b, m_mla_q_norm, m_mla_w_uq, m_mla_kv_norm, m_mla_w_ukv, m_mix_w_branch, m_mix_w_out, m_ffn2_norm_pre, m_ffn2_norm_post, m_ffn2_w_in, m_ffn2_w_out, v_ffn1_norm_pre, v_ffn1_norm_post, v_ffn1_w_in, v_ffn1_w_out, v_mix_norm_pre, v_mix_norm_post, v_mix_w_in, v_gm_ln_g, v_gm_ln_b, v_gm_w_s, v_gm_b_s, v_conv_w, v_conv_b, v_conv_ln_g, v_conv_ln_b, v_mla_q_norm, v_mla_w_uq, v_mla_kv_norm, v_mla_w_ukv, v_mix_w_branch, v_mix_w_out, v_ffn2_norm_pre, v_ffn2_norm_post, v_ffn2_w_in, v_ffn2_w_out):
    w = dict(zip(WEIGHTS, (ffn1_norm_pre, ffn1_norm_post, ffn1_w_in, ffn1_w_out, mix_norm_pre, mix_norm_post, mix_w_in, gm_ln_g, gm_ln_b, gm_w_s, gm_b_s, conv_w, conv_b, conv_ln_g, conv_ln_b, mla_q_norm, mla_w_uq, mla_kv_norm, mla_w_ukv, mix_w_branch, mix_w_out, ffn2_norm_pre, ffn2_norm_post, ffn2_w_in, ffn2_w_out)))
    mom_m = dict(zip(WEIGHTS, (m_ffn1_norm_pre, m_ffn1_norm_post, m_ffn1_w_in, m_ffn1_w_out, m_mix_norm_pre, m_mix_norm_post, m_mix_w_in, m_gm_ln_g, m_gm_ln_b, m_gm_w_s, m_gm_b_s, m_conv_w, m_conv_b, m_conv_ln_g, m_conv_ln_b, m_mla_q_norm, m_mla_w_uq, m_mla_kv_norm, m_mla_w_ukv, m_mix_w_branch, m_mix_w_out, m_ffn2_norm_pre, m_ffn2_norm_post, m_ffn2_w_in, m_ffn2_w_out)))
    mom_v = dict(zip(WEIGHTS, (v_ffn1_norm_pre, v_ffn1_norm_post, v_ffn1_w_in, v_ffn1_w_out, v_mix_norm_pre, v_mix_norm_post, v_mix_w_in, v_gm_ln_g, v_gm_ln_b, v_gm_w_s, v_gm_b_s, v_conv_w, v_conv_b, v_conv_ln_g, v_conv_ln_b, v_mla_q_norm, v_mla_w_uq, v_mla_kv_norm, v_mla_w_ukv, v_mix_w_branch, v_mix_w_out, v_ffn2_norm_pre, v_ffn2_norm_post, v_ffn2_w_in, v_ffn2_w_out)))
    n_layers = ffn1_norm_pre.shape[0]
    t_dim, d = x.shape[1], x.shape[2]
    dims = dict(d=d, gmw=gm_ln_g.shape[1], cw=conv_ln_g.shape[1], qr=mla_q_norm.shape[1], kvr=mla_kv_norm.shape[1])
    x0 = x.reshape(t_dim, d)
    target = loss_target.reshape(t_dim, d)
    rope_tabs = _rope_tables(positions.reshape(t_dim))

    conv_all = _unshard_conv(_comm([conv_w], False, "gather_conv")[0])
    gather = lambda names, l: ([w[k][l].astype(BF16) for k in names], False)
    full_of = lambda names, gathered: {k: _unshard(k, g) for k, g in zip(names, gathered)}
    last = n_layers - 1
    got_ffn1 = _comm(gather(FFN1_W, 0)[0], False, "gather_ffn1")
    got_mix = None
    params, saved = [], []
    xc = x0
    for l in range(n_layers):
        p1 = _ffn_params(full_of(FFN1_W, got_ffn1), w, "ffn1", l)
        xc, s1, got_a, got_b = _ffn_fwd(xc, p1, "ffn1", gather(MIX_W[:1], 0) if l == 0 else None,
                                        gather(MIX_W[1:], 0) if l == 0 else None)
        if l == 0:
            got_mix = got_a + got_b
        pm = _mix_params(full_of(MIX_W, got_mix), w, conv_all[l], dims, l)
        xc, sm, got = _mixer_fwd(xc, pm, rope_tabs, dims, gather(FFN2_W, l) if l == last else
                                 (gather(FFN2_W, l)[0] + gather(MIX_W, l + 1)[0], False))
        got_ffn2, got_mix = got[:len(FFN2_W)], got[len(FFN2_W):]
        p2 = _ffn_params(full_of(FFN2_W, got_ffn2), w, "ffn2", l)
        xc, s2, got_a, got_b = _ffn_fwd(xc, p2, "ffn2", gather(FFN1_W[:1], l + 1) if l < last else None,
                                        gather(FFN1_W[1:], l + 1) if l < last else None)
        got_ffn1 = got_a + got_b
        params.append((p1, pm, p2))
        saved.append((s1, sm, s2))
    dx, loss_part = _loss_fwd_bwd(xc, target, "loss")
    loss = lax.psum(loss_part[0, 0], ("x", "y", "c"))

    big_out = {k: [None] * n_layers for k in BIG + ("conv_w",)}
    small_parts = [None] * n_layers
    send = lambda names, grads: [_to_shards(k, grads[k]).astype(BF16) for k in names]

    def update(l, names, recv):
        for k, r in zip(names, recv):
            wl = w[k][l]
            r2 = (lambda a: a.reshape(-1, a.shape[-1]))
            outs = _adam(r2(wl), r.reshape(N_DEV, -1, wl.shape[-1]), r2(mom_m[k][l]), r2(mom_v[k][l]), f"adam_{k}")
            big_out[k][l] = [o.reshape(wl.shape) for o in outs]

    above_mix, above_ffn1 = [], []
    for l in reversed(range(n_layers)):
        p1, pm, p2 = params[l]
        s1, sm, s2 = saved[l]
        dx, g2, got_out, got_in = _ffn_bwd(dx, s2, p2, "ffn2", (above_ffn1[1:], True) if above_ffn1 else None,
                                           (above_ffn1[:1], True) if above_ffn1 else None)
        if above_ffn1:
            update(l + 1, FFN1_W, got_in + got_out)
        mine = [g2["w_in"], g2["w_out"]]
        dx, gm, recv = _mixer_bwd(dx, sm, pm, rope_tabs, dims, (mine + above_mix, True))
        update(l, FFN2_W, recv[:len(FFN2_W)])
        if above_mix:
            update(l + 1, MIX_G, recv[len(FFN2_W):])
        mix_out = send(MIX_G, _mix_big_grads(gm, dims))
        if l > 0:
            dx, g1, _, _ = _ffn_bwd(dx, s1, p1, "ffn1")
        else:
            dx, g1, got_a, got_b = _ffn_bwd(dx, s1, p1, "ffn1", (mix_out[1:], True), (mix_out[:1], True))
            update(0, MIX_G, got_b + got_a)
        ffn1_out = [g1["w_in"], g1["w_out"]]
        above_mix, above_ffn1 = mix_out, ffn1_out
        small_parts[l] = _small_grads(g1, gm, g2)
    update(0, FFN1_W, _comm(ffn1_out, True, "exchange_ffn1"))
    grad_x = dx.reshape(x.shape)

    flat = lambda tree: _pad_rows(jnp.concatenate([tree[k].reshape(-1) for k in SMALL]), 256 * 128).reshape(-1, 128)
    g_small = flat({k: jnp.stack([small_parts[l][k] for l in range(n_layers)]) for k in SMALL})
    parts = _comm([g_small], False, "gather_small_grads")[0]
    s_outs = _adam(flat(w), parts, flat(mom_m), flat(mom_v), "adam_small")
    small_out = {k: [] for k in SMALL}
    for o in s_outs:
        o = o.reshape(-1)
        off = 0
        for k in SMALL:
            n = int(np.prod(w[k].shape))
            small_out[k].append(o[off:off + n].reshape(w[k].shape))
            off += n

    def out(which, k):
        if k in SMALL:
            return small_out[k][which]
        return jnp.stack([big_out[k][l][which] for l in range(n_layers)])

    return (loss, grad_x, *[out(0, k) for k in WEIGHTS], *[out(1, k) for k in WEIGHTS],
            *[out(2, k) for k in WEIGHTS], *[out(3, k) for k in WEIGHTS])


def _unshard_conv(g):
    n_dev, n_layers, k, c = g.shape
    return jnp.transpose(g, (1, 2, 0, 3)).reshape(n_layers, k, n_dev * c)
```

```python
import functools

import numpy as np
import jax
import jax.numpy as jnp
from jax import lax
from jax.experimental import pallas as pl
from jax.experimental.pallas import tpu as pltpu

F32 = jnp.float32
BF16 = jnp.bfloat16

N_DEV = 8
N_HEADS = 16
NOPE = 128
ROPE = 64
VDIM = 128
HQ = 256
GROUPS = 4
CHUNK = 128
CONV_K = 31
CONV_HALO = 32
EPS = 1e-6
ROPE_THETA = 10000.0
ADAM_LR = 0.001
ADAM_B1 = 0.9
ADAM_B2 = 0.999
ADAM_EPS = 1e-08
ADAM_WD = 0.01
ADAM_STEP = 10
NEG = -1e30

V7X_VMEM_LIMIT = 56 * 1024 * 1024
MM_TM, MM_TN, MM_TK = 1024, 1024, 4096
ROW_TILE = 256
ROW_TILE_WIDE = 256
ATT_TILE = 1024
ATT_FWD_HEADS = 2
ATT_BWD_HEADS = 2
CONV_TILE = 256
GM_TILE = 256
ADAM_BLOCK_ELEMS = 128 * 1024

BIG = ("ffn1_w_in", "ffn1_w_out", "mix_w_in", "mla_w_uq", "mla_w_ukv", "mix_w_branch", "mix_w_out",
       "ffn2_w_in", "ffn2_w_out")
FFN1_W = ("ffn1_w_in", "ffn1_w_out")
MIX_W = ("mix_w_in", "mla_w_uq", "mla_w_ukv", "mix_w_branch", "mix_w_out")
FFN2_W = ("ffn2_w_in", "ffn2_w_out")
MIX_G = MIX_W + ("conv_w",)
COL_SHARDED = ("ffn1_w_in", "mix_w_in", "mla_w_uq", "mla_w_ukv", "ffn2_w_in", "conv_w")
SMALL = ("ffn1_norm_pre", "ffn1_norm_post", "mix_norm_pre", "mix_norm_post", "gm_ln_g", "gm_ln_b", "gm_w_s",
         "gm_b_s", "conv_b", "conv_ln_g", "conv_ln_b", "mla_q_norm", "mla_kv_norm", "ffn2_norm_pre",
         "ffn2_norm_post")
WEIGHTS = ("ffn1_norm_pre", "ffn1_norm_post", "ffn1_w_in", "ffn1_w_out", "mix_norm_pre", "mix_norm_post",
           "mix_w_in", "gm_ln_g", "gm_ln_b", "gm_w_s", "gm_b_s", "conv_w", "conv_b", "conv_ln_g", "conv_ln_b",
           "mla_q_norm", "mla_w_uq", "mla_kv_norm", "mla_w_ukv", "mix_w_branch", "mix_w_out", "ffn2_norm_pre",
           "ffn2_norm_post", "ffn2_w_in", "ffn2_w_out")


def _cparams(sem):
    return pltpu.CompilerParams(dimension_semantics=sem, vmem_limit_bytes=V7X_VMEM_LIMIT)


def _pick(dim, pref):
    if dim <= pref:
        return dim
    t = pref
    while t >= 128:
        if dim % t == 0:
            return t
        t -= 128
    return dim


def _sigmoid(x):
    return 1.0 / (1.0 + jnp.exp(-x))


_GELU_C = 0.7978845608028654


def _gelu(x):
    t = jnp.tanh(_GELU_C * (x + 0.044715 * x * x * x))
    return 0.5 * x * (1.0 + t)


def _gelu_grad(x):
    t = jnp.tanh(_GELU_C * (x + 0.044715 * x * x * x))
    return 0.5 * (1.0 + t) + 0.5 * x * (1.0 - t * t) * _GELU_C * (1.0 + 3.0 * 0.044715 * x * x)


def _rms(x):
    r = lax.rsqrt(jnp.mean(x * x, axis=-1, keepdims=True) + EPS)
    return x * r, r


def _rms_bwd(xn, r, t):
    return r * (t - xn * jnp.mean(t * xn, axis=-1, keepdims=True))


def _colsum(x):
    return jnp.sum(x, axis=0, keepdims=True)


_DIMS = {"nn": (((1,), (0,)), ((), ())), "nt": (((1,), (1,)), ((), ())), "tn": (((0,), (0,)), ((), ()))}


def _mm(a, b, mode, out_dtype, name, carry=None, shard_out=None):
    if mode == "tn":
        k_dim, m_dim = a.shape
    else:
        m_dim, k_dim = a.shape
    n_dim = b.shape[0] if mode == "nt" else b.shape[1]
    tm = _pick(m_dim // N_DEV if shard_out == "row" else m_dim, MM_TM)
    tn = _pick(n_dim // N_DEV if shard_out == "col" else n_dim, MM_TN)
    tk = _pick(k_dim, MM_TK)
    grid = (m_dim // tm, n_dim // tn, k_dim // tk)
    nk = grid[2]
    dims = _DIMS[mode]
    c_arrays, c_exchange = carry if carry else ((), False)
    nc = len(c_arrays)

    def at_step(first):
        conds = [pl.program_id(q) == (0 if first else grid[q] - 1) for q in range(3)]
        return jnp.logical_and(jnp.logical_and(conds[0], conds[1]), conds[2])

    def body(a_ref, b_ref, *rest):
        c_ins, o_ref, c_outs = rest[:nc], rest[nc], rest[nc + 1:2 * nc + 1]
        scratch = rest[2 * nc + 1:]
        sems = scratch[1:] if nk > 1 else scratch
        if nc:
            @pl.when(at_step(True))
            def _():
                _comm_start(c_ins, c_outs, *sems, c_exchange)

        prod = lax.dot_general(a_ref[...], b_ref[...], dims, preferred_element_type=F32)
        if nk == 1:
            o_ref[...] = prod.astype(o_ref.dtype)
        else:
            acc_ref, k = scratch[0], pl.program_id(2)

            @pl.when(k == 0)
            def _():
                acc_ref[...] = prod

            @pl.when(jnp.logical_and(k > 0, k < nk - 1))
            def _():
                acc_ref[...] += prod

            @pl.when(k == nk - 1)
            def _():
                o_ref[...] = (acc_ref[...] + prod).astype(o_ref.dtype)

        if nc:
            @pl.when(at_step(False))
            def _():
                _comm_wait(c_ins, c_outs, *sems, c_exchange)

    if mode == "tn":
        a_spec = pl.BlockSpec((tk, tm), lambda i, j, k: (k, i))
    else:
        a_spec = pl.BlockSpec((tm, tk), lambda i, j, k: (i, k))
    if mode == "nt":
        b_spec = pl.BlockSpec((tn, tk), lambda i, j, k: (j, k))
    else:
        b_spec = pl.BlockSpec((tk, tn), lambda i, j, k: (k, j))
    any_spec = pl.BlockSpec(memory_space=pl.ANY)
    if shard_out == "col":
        per = n_dim // N_DEV // tn
        o_shape, o_spec = (N_DEV, m_dim, n_dim // N_DEV), pl.BlockSpec((None, tm, tn), lambda i, j, k: (j // per, i, j % per))
    elif shard_out == "row":
        per = m_dim // N_DEV // tm
        o_shape, o_spec = (N_DEV, m_dim // N_DEV, n_dim), pl.BlockSpec((None, tm, tn), lambda i, j, k: (i // per, i % per, j))
    else:
        o_shape, o_spec = (m_dim, n_dim), pl.BlockSpec((tm, tn), lambda i, j, k: (i, j))
    outs = pl.pallas_call(
        body, out_shape=[jax.ShapeDtypeStruct(o_shape, out_dtype)] + _comm_out_shape(c_arrays, c_exchange),
        grid=grid, in_specs=[a_spec, b_spec] + [any_spec] * nc,
        out_specs=[o_spec] + [any_spec] * nc,
        scratch_shapes=([pltpu.VMEM((tm, tn), F32)] if nk > 1 else []) + _comm_sems(nc),
        compiler_params=_cparams(("arbitrary",) * 3 if nc else ("parallel", "parallel", "arbitrary")),
        name=name)(a, b, *c_arrays)
    return (outs[0], outs[1:]) if nc else outs[0]


def _rowwise(fn, name, n_rows, tr, row_ins, full_ins, row_outs, acc_outs=()):
    tr = min(tr, n_rows)
    n_ri, n_fi, n_ro = len(row_ins), len(full_ins), len(row_outs)

    def body(*refs):
        i = pl.program_id(0)
        ri, fi = refs[:n_ri], refs[n_ri:n_ri + n_fi]
        ro, ao = refs[n_ri + n_fi:n_ri + n_fi + n_ro], refs[n_ri + n_fi + n_ro:]

        @pl.when(i == 0)
        def _():
            for r in ao:
                r[...] = jnp.zeros_like(r)

        fn(i, ri, fi, ro, ao)

    in_specs = [pl.BlockSpec((tr, w), functools.partial(lambda c, i: (i, c), cb)) for _, w, cb in row_ins]
    in_specs += [pl.BlockSpec(a.shape, functools.partial(lambda nd, i: (0,) * nd, a.ndim)) for a in full_ins]
    out_specs = [pl.BlockSpec((tr, w), lambda i: (i, 0)) for w, _ in row_outs]
    out_specs += [pl.BlockSpec(s, functools.partial(lambda nd, i: (0,) * nd, len(s))) for s, _ in acc_outs]
    out_shape = [jax.ShapeDtypeStruct((n_rows, w), d) for w, d in row_outs]
    out_shape += [jax.ShapeDtypeStruct(s, d) for s, d in acc_outs]
    return pl.pallas_call(
        body, out_shape=out_shape, grid=(n_rows // tr,), in_specs=in_specs, out_specs=out_specs,
        compiler_params=_cparams(("arbitrary",)), name=name)(*[a for a, _, _ in row_ins], *full_ins)


def _norm_fwd(x, g, name):
    t_dim, d = x.shape

    def fn(i, ri, fi, ro, ao):
        xn, _ = _rms(ri[0][...])
        ro[0][...] = (xn * fi[0][...]).astype(BF16)

    return _rowwise(fn, name, t_dim, ROW_TILE, [(x, d, 0)], [g], [(d, BF16)])[0]


def _resid_fwd(x, y, g, coef, name):
    t_dim, d = x.shape

    def fn(i, ri, fi, ro, ao):
        yn, _ = _rms(ri[1][...])
        ro[0][...] = ri[0][...] + coef * (yn * fi[0][...])

    return _rowwise(fn, name, t_dim, ROW_TILE, [(x, d, 0), (y, d, 0)], [g], [(d, F32)])[0]


def _resid_bwd(y, dxo, g, coef, name):
    t_dim, d = y.shape

    def fn(i, ri, fi, ro, ao):
        yn, r = _rms(ri[0][...])
        dyn = coef * ri[1][...]
        ao[0][...] += _colsum(dyn * yn)
        ro[0][...] = _rms_bwd(yn, r, dyn * fi[0][...]).astype(BF16)

    return _rowwise(fn, name, t_dim, ROW_TILE, [(y, d, 0), (dxo, d, 0)], [g], [(d, BF16)], [((1, d), F32)])


def _norm_bwd(x, dhs, dxo, g, name):
    t_dim, d = x.shape
    n = len(dhs)

    def fn(i, ri, fi, ro, ao):
        xn, r = _rms(ri[0][...])
        dh = ri[2][...]
        for q in range(1, n):
            dh = dh + ri[2 + q][...]
        ao[0][...] += _colsum(dh * xn)
        ro[0][...] = ri[1][...] + _rms_bwd(xn, r, dh * fi[0][...])

    return _rowwise(fn, name, t_dim, ROW_TILE, [(x, d, 0), (dxo, d, 0)] + [(a, d, 0) for a in dhs], [g],
                    [(d, F32)], [((1, d), F32)])


def _mm_swiglu(h, w_in, name, carry=None):
    m_dim, k_dim = h.shape
    f = w_in.shape[1] // 2
    tm, tn = _pick(m_dim, MM_TM), _pick(f, MM_TN)
    grid = (m_dim // tm, f // tn)
    c_arrays, c_exchange = carry if carry else ((), False)
    nc = len(c_arrays)

    def at_step(first):
        return jnp.logical_and(pl.program_id(0) == (0 if first else grid[0] - 1),
                               pl.program_id(1) == (0 if first else grid[1] - 1))

    def body(h_ref, wg_ref, wu_ref, *rest):
        c_ins, (gate_ref, up_ref, act_ref), c_outs = rest[:nc], rest[nc:nc + 3], rest[nc + 3:2 * nc + 3]
        sems = rest[2 * nc + 3:]
        if nc:
            @pl.when(at_step(True))
            def _():
                _comm_start(c_ins, c_outs, *sems, c_exchange)

        gate = jnp.dot(h_ref[...], wg_ref[...], preferred_element_type=F32)
        up = jnp.dot(h_ref[...], wu_ref[...], preferred_element_type=F32)
        gate_ref[...] = gate.astype(BF16)
        up_ref[...] = up.astype(BF16)
        act_ref[...] = (gate * _sigmoid(gate) * up).astype(BF16)

        if nc:
            @pl.when(at_step(False))
            def _():
                _comm_wait(c_ins, c_outs, *sems, c_exchange)

    any_spec = pl.BlockSpec(memory_space=pl.ANY)
    up_blocks = f // tn
    out_blk = pl.BlockSpec((tm, tn), lambda i, j: (i, j))
    outs = pl.pallas_call(
        body, out_shape=[jax.ShapeDtypeStruct((m_dim, f), BF16)] * 3 + _comm_out_shape(c_arrays, c_exchange),
        grid=grid,
        in_specs=[pl.BlockSpec((tm, k_dim), lambda i, j: (i, 0)), pl.BlockSpec((k_dim, tn), lambda i, j: (0, j)),
                  pl.BlockSpec((k_dim, tn), lambda i, j: (0, j + up_blocks))] + [any_spec] * nc,
        out_specs=[out_blk] * 3 + [any_spec] * nc, scratch_shapes=_comm_sems(nc),
        compiler_params=_cparams(("arbitrary", "arbitrary") if nc else ("parallel", "parallel")),
        name=name)(h, w_in, w_in, *c_arrays)
    return outs[0], outs[1], outs[2], outs[3:]


def _swiglu_bwd(gate, up, dact, name):
    t_dim, f = gate.shape

    def fn(i, ri, fi, ro, ao):
        g, u = ri[0][...].astype(F32), ri[1][...].astype(F32)
        da = ri[2][...].astype(F32)
        s = _sigmoid(g)
        ro[0][:, :f] = (da * u * (s * (1.0 + g * (1.0 - s)))).astype(BF16)
        ro[0][:, f:] = (da * (g * s)).astype(BF16)

    return _rowwise(fn, name, t_dim, ROW_TILE_WIDE, [(gate, f, 0), (up, f, 0), (dact, f, 0)], [], [(2 * f, BF16)])[0]


def _loss_fwd_bwd(y, target, name):
    t_dim, d = y.shape

    def fn(i, ri, fi, ro, ao):
        err = ri[0][...] - ri[1][...]
        ao[0][...] += _colsum(jnp.sum(err * err, axis=1, keepdims=True)) * (0.5 / d)
        ro[0][...] = err * (1.0 / d)

    return _rowwise(fn, name, t_dim, ROW_TILE, [(y, d, 0), (target, d, 0)], [], [(d, F32)], [((1, 1), F32)])


def _ln_stats(v):
    mu = jnp.mean(v, axis=-1, keepdims=True)
    xc = v - mu
    rstd = lax.rsqrt(jnp.mean(xc * xc, axis=-1, keepdims=True) + EPS)
    return xc * rstd, rstd


def _tril_mask(upper=False):
    row = lax.broadcasted_iota(jnp.int32, (CHUNK, CHUNK), 0)
    col = lax.broadcasted_iota(jnp.int32, (CHUNK, CHUNK), 1)
    return row <= col if upper else row >= col


def _gm_fwd(z_a, ln_g, ln_b, w_s, b_s, gmw, name):
    t_dim = z_a.shape[0]
    gw = gmw // GROUPS
    tr = min(GM_TILE, t_dim)

    def fn(i, ri, fi, ro, ao):
        lng, lnb, ws_ref, bs_ref = fi
        u = _gelu(ri[0][...].astype(F32))
        vhat, _ = _ln_stats(_gelu(ri[1][...].astype(F32)))
        vn = (vhat * lng[...] + lnb[...]).astype(BF16)
        mask = _tril_mask()
        for g in range(GROUPS):
            wg = jnp.where(mask, ws_ref[g], 0.0).astype(BF16)
            for c in range(tr // CHUNK):
                rows, cols = slice(c * CHUNK, (c + 1) * CHUNK), slice(g * gw, (g + 1) * gw)
                s = jnp.dot(wg, vn[rows, cols], preferred_element_type=F32) + bs_ref[g]
                ro[0][rows, cols] = (u[rows, cols] * s).astype(BF16)

    return _rowwise(fn, name, t_dim, tr, [(z_a, gmw, 0), (z_a, gmw, 1)], [ln_g, ln_b, w_s, b_s], [(gmw, BF16)])[0]


def _gm_bwd(z_a, do_a, ln_g, ln_b, w_s, w_s_t, b_s, gmw, name):
    t_dim = z_a.shape[0]
    gw = gmw // GROUPS
    tr = min(GM_TILE, t_dim)

    def fn(i, ri, fi, ro, ao):
        lng, lnb, ws_ref, wst_ref, bs_ref = fi
        d_lng, d_lnb, d_ws, d_bs = ao
        a_u, a_v, do = ri[0][...].astype(F32), ri[1][...].astype(F32), ri[2][...]
        u = _gelu(a_u)
        vhat, rstd = _ln_stats(_gelu(a_v))
        vn = (vhat * lng[...] + lnb[...]).astype(BF16)
        mask = _tril_mask()
        wgs = [jnp.where(mask, ws_ref[g], 0.0).astype(BF16) for g in range(GROUPS)]
        wgts = [jnp.where(_tril_mask(upper=True), wst_ref[g], 0.0).astype(BF16) for g in range(GROUPS)]
        for c in range(tr // CHUNK):
            rows = slice(c * CHUNK, (c + 1) * CHUNK)
            dvn_parts = []
            for g in range(GROUPS):
                cols = slice(g * gw, (g + 1) * gw)
                vn_blk = vn[rows, cols]
                s = jnp.dot(wgs[g], vn_blk, preferred_element_type=F32) + bs_ref[g]
                ro[0][rows, cols] = (do[rows, cols] * s * _gelu_grad(a_u[rows, cols])).astype(BF16)
                ds = do[rows, cols] * u[rows, cols]
                d_bs[g] += jnp.sum(ds, axis=1, keepdims=True)
                dsb = ds.astype(BF16)
                dw = lax.dot_general(dsb, vn_blk, _DIMS["nt"], preferred_element_type=F32)
                d_ws[g] += jnp.where(mask, dw, 0.0)
                dvn_parts.append(jnp.dot(wgts[g], dsb, preferred_element_type=F32))
            dvn = jnp.concatenate(dvn_parts, axis=1)
            vh, rs = vhat[rows], rstd[rows]
            d_lng[...] += _colsum(dvn * vh)
            d_lnb[...] += _colsum(dvn)
            dvh = dvn * lng[...]
            dv = rs * (dvh - jnp.mean(dvh, axis=-1, keepdims=True) - vh * jnp.mean(dvh * vh, axis=-1, keepdims=True))
            ro[0][rows, gmw:] = (dv * _gelu_grad(a_v[rows])).astype(BF16)

    return _rowwise(fn, name, t_dim, tr, [(z_a, gmw, 0), (z_a, gmw, 1), (do_a, gmw, 0)],
                    [ln_g, ln_b, w_s, w_s_t, b_s], [(2 * gmw, BF16)],
                    [((1, gmw), F32), ((1, gmw), F32), ((GROUPS, CHUNK, CHUNK), F32), ((GROUPS, CHUNK, 1), F32)])


def _glu_fwd(z_a, gmw, cw, name):
    t_dim = z_a.shape[0]
    cb = (2 * gmw) // cw

    def fn(i, ri, fi, ro, ao):
        ro[0][...] = ri[0][...].astype(F32) * _sigmoid(ri[1][...].astype(F32))

    return _rowwise(fn, name, t_dim, ROW_TILE, [(z_a, cw, cb), (z_a, cw, cb + 1)], [], [(cw, F32)])[0]


def _by_sublane_shift(offsets):
    groups = {}
    for o in offsets:
        groups.setdefault(o % 8, []).append(o)
    return sorted(groups.items())


def _conv_fwd(a, w, b, ln_g, ln_b, name):
    t_dim, c_dim = a.shape
    tr = min(CONV_TILE, t_dim)
    hb = tr // CONV_HALO

    def body(cur_ref, prev_ref, w_ref, b_ref, g_ref, be_ref, y_ref, o_ref, buf):
        i = pl.program_id(0)
        buf[0:CONV_HALO, :] = jnp.where(i > 0, prev_ref[...], 0.0)
        buf[CONV_HALO:, :] = cur_ref[...]
        for cs in range(c_dim // 128):
            lanes = pl.ds(cs * 128, 128)
            acc = jnp.zeros((tr, 128), F32)
            for k in range(CONV_K):
                acc = acc + w_ref[k:k + 1, lanes] * buf[pl.ds(k + 2, tr), lanes]
            y_ref[:, lanes] = acc + b_ref[:, lanes]
        n_hat, _ = _ln_stats(y_ref[...])
        n = n_hat * g_ref[...] + be_ref[...]
        o_ref[...] = (n * _sigmoid(n)).astype(BF16)

    full = lambda arr: pl.BlockSpec(arr.shape, lambda i: (0, 0))
    return pl.pallas_call(
        body, out_shape=[jax.ShapeDtypeStruct((t_dim, c_dim), F32), jax.ShapeDtypeStruct((t_dim, c_dim), BF16)],
        grid=(t_dim // tr,),
        in_specs=[pl.BlockSpec((tr, c_dim), lambda i: (i, 0)),
                  pl.BlockSpec((CONV_HALO, c_dim), lambda i: (jnp.maximum(i * hb - 1, 0), 0)),
                  full(w), full(b), full(ln_g), full(ln_b)],
        out_specs=[pl.BlockSpec((tr, c_dim), lambda i: (i, 0)), pl.BlockSpec((tr, c_dim), lambda i: (i, 0))],
        scratch_shapes=[pltpu.VMEM((tr + CONV_HALO, c_dim), F32)],
        compiler_params=_cparams(("arbitrary",)), name=name)(a, a, w, b, ln_g, ln_b)


def _conv_ln_bwd(y, do_b, ln_g, ln_b, name):
    t_dim, c_dim = y.shape

    def fn(i, ri, fi, ro, ao):
        n_hat, rstd = _ln_stats(ri[0][...])
        n = n_hat * fi[0][...] + fi[1][...]
        s = _sigmoid(n)
        dn = ri[1][...] * (s * (1.0 + n * (1.0 - s)))
        ao[0][...] += _colsum(dn * n_hat)
        ao[1][...] += _colsum(dn)
        dnh = dn * fi[0][...]
        ro[0][...] = rstd * (dnh - jnp.mean(dnh, axis=-1, keepdims=True)
                             - n_hat * jnp.mean(dnh * n_hat, axis=-1, keepdims=True))

    return _rowwise(fn, name, t_dim, ROW_TILE, [(y, c_dim, 0), (do_b, c_dim, 0)], [ln_g, ln_b], [(c_dim, F32)],
                    [((1, c_dim), F32), ((1, c_dim), F32)])


def _conv_bwd(dy, a, z_a, w, gmw, name):
    t_dim, c_dim = a.shape
    tr = min(CONV_TILE, t_dim)
    hb = tr // CONV_HALO
    n_halo = t_dim // CONV_HALO
    nb = t_dim // tr
    cb = (2 * gmw) // c_dim

    def body(dy_ref, dyn_ref, a_ref, ap_ref, val_ref, gate_ref, w_ref, dvg_ref, dw_ref, db_ref, dbuf, abuf, da_buf):
        i = pl.program_id(0)

        @pl.when(i == 0)
        def _():
            dw_ref[...] = jnp.zeros_like(dw_ref)
            db_ref[...] = jnp.zeros_like(db_ref)

        dbuf[0:tr, :] = dy_ref[...]
        dbuf[tr:, :] = jnp.where(i < nb - 1, dyn_ref[...], 0.0)
        abuf[0:CONV_HALO, :] = jnp.where(i > 0, ap_ref[...], 0.0)
        abuf[CONV_HALO:, :] = a_ref[...]
        db_ref[...] += _colsum(dy_ref[...])
        for cs in range(c_dim // 128):
            lanes = pl.ds(cs * 128, 128)
            dyc = dy_ref[:, lanes]
            acc = jnp.zeros((tr, 128), F32)
            for k in range(CONV_K):
                acc = acc + w_ref[k:k + 1, lanes] * dbuf[pl.ds(CONV_K - 1 - k, tr), lanes]
            for r, offs in _by_sublane_shift([k + 2 for k in range(CONV_K)]):
                shifted = abuf[pl.ds(r, tr + CONV_HALO - (8 if r else 0)), lanes]
                for o in offs:
                    dw_ref[o - 2:o - 1, lanes] += _colsum(dyc * shifted[o - r:o - r + tr])
            da_buf[:, lanes] = acc
        da = da_buf[...]
        s = _sigmoid(gate_ref[...].astype(F32))
        dvg_ref[:, :c_dim] = (da * s).astype(BF16)
        dvg_ref[:, c_dim:] = (da * val_ref[...].astype(F32) * s * (1.0 - s)).astype(BF16)

    row = lambda cblk: pl.BlockSpec((tr, c_dim), functools.partial(lambda c, i: (i, c), cblk))
    return pl.pallas_call(
        body,
        out_shape=[jax.ShapeDtypeStruct((t_dim, 2 * c_dim), BF16),
                   jax.ShapeDtypeStruct((CONV_HALO, c_dim), F32), jax.ShapeDtypeStruct((1, c_dim), F32)],
        grid=(nb,),
        in_specs=[row(0), pl.BlockSpec((CONV_HALO, c_dim), lambda i: (jnp.minimum((i + 1) * hb, n_halo - 1), 0)),
                  row(0), pl.BlockSpec((CONV_HALO, c_dim), lambda i: (jnp.maximum(i * hb - 1, 0), 0)),
                  row(cb), row(cb + 1), pl.BlockSpec(w.shape, lambda i: (0, 0))],
        out_specs=[pl.BlockSpec((tr, 2 * c_dim), lambda i: (i, 0)), pl.BlockSpec((CONV_HALO, c_dim), lambda i: (0, 0)),
                   pl.BlockSpec((1, c_dim), lambda i: (0, 0))],
        scratch_shapes=[pltpu.VMEM((tr + CONV_HALO, c_dim), F32), pltpu.VMEM((tr + CONV_HALO, c_dim), F32),
                        pltpu.VMEM((tr, c_dim), F32)],
        compiler_params=_cparams(("arbitrary",)), name=name)(dy, dy, a, a, z_a, z_a, w)


def _mla_norm_fwd(z_c, q_g, kv_g, qr, kvr, name):
    t_dim, cwid = z_c.shape

    def fn(i, ri, fi, ro, ao):
        cq, _ = _rms(ri[0][:, :qr])
        ckv, _ = _rms(ri[0][:, qr:qr + kvr])
        ro[0][...] = (cq * fi[0][...]).astype(BF16)
        ro[1][...] = (ckv * fi[1][...]).astype(BF16)

    return _rowwise(fn, name, t_dim, ROW_TILE, [(z_c, cwid, 0)], [q_g, kv_g], [(qr, BF16), (kvr, BF16)])


def _rope(t, cf, s1, s2):
    return t * cf + pltpu.roll(t, 96, 1) * s1 + pltpu.roll(t, 32, 1) * s2


def _rope_t(g, cf, s1, s2):
    return g * cf + pltpu.roll(g * s1, 32, 1) + pltpu.roll(g * s2, 96, 1)


def _rope_fwd(q_pre, k_nope, z_c, cf, s1, s2, rope_blk, scale, name):
    t_dim = q_pre.shape[0]

    def fn(i, ri, fi, ro, ao):
        c, a, b = ri[3][...], ri[4][...], ri[5][...]
        kt = _rope(ri[2][...], c, a, b).astype(BF16)
        for h in range(N_HEADS):
            ro[0][:, h * HQ:h * HQ + NOPE] = (ri[0][:, h * HQ:h * HQ + NOPE].astype(F32) * scale).astype(BF16)
            ro[0][:, h * HQ + NOPE:(h + 1) * HQ] = (
                _rope(ri[0][:, h * HQ + NOPE:(h + 1) * HQ].astype(F32), c, a, b) * scale).astype(BF16)
            ro[1][:, h * HQ:h * HQ + NOPE] = ri[1][:, h * NOPE:(h + 1) * NOPE].astype(BF16)
            ro[1][:, h * HQ + NOPE:(h + 1) * HQ] = kt

    return _rowwise(fn, name, t_dim, ROW_TILE_WIDE,
                    [(q_pre, N_HEADS * HQ, 0), (k_nope, N_HEADS * NOPE, 0), (z_c, 128, rope_blk),
                     (cf, 128, 0), (s1, 128, 0), (s2, 128, 0)], [],
                    [(N_HEADS * HQ, BF16), (N_HEADS * HQ, BF16)])


def _rope_bwd(dq_cat, dk_cat, cf, s1, s2, scale, name):
    t_dim = dq_cat.shape[0]

    def fn(i, ri, fi, ro, ao):
        c, a, b = ri[2][...], ri[3][...], ri[4][...]
        dkt = jnp.zeros((ri[0].shape[0], 128), F32)
        for h in range(N_HEADS):
            ro[0][:, h * HQ:h * HQ + NOPE] = (ri[0][:, h * HQ:h * HQ + NOPE] * scale).astype(BF16)
            ro[0][:, h * HQ + NOPE:(h + 1) * HQ] = _rope_t(ri[0][:, h * HQ + NOPE:(h + 1) * HQ] * scale, c, a, b).astype(BF16)
            ro[1][:, h * NOPE:(h + 1) * NOPE] = ri[1][:, h * HQ:h * HQ + NOPE].astype(BF16)
            dkt = dkt + ri[1][:, h * HQ + NOPE:(h + 1) * HQ].astype(F32)
        ro[2][...] = _rope_t(dkt, c, a, b)

    return _rowwise(fn, name, t_dim, ROW_TILE_WIDE,
                    [(dq_cat, N_HEADS * HQ, 0), (dk_cat, N_HEADS * HQ, 0), (cf, 128, 0), (s1, 128, 0), (s2, 128, 0)],
                    [], [(N_HEADS * HQ, BF16), (N_HEADS * NOPE, BF16), (128, F32)])


def _mla_norm_bwd(z_c, dcq, dckv_k, dckv_v, dkr, q_g, kv_g, qr, kvr, name):
    t_dim, cwid = z_c.shape

    def fn(i, ri, fi, ro, ao):
        cq, rq = _rms(ri[0][:, :qr])
        ckv, rkv = _rms(ri[0][:, qr:qr + kvr])
        dq = ri[1][...]
        dkv = ri[2][...] + ri[3][...]
        ao[0][...] += _colsum(dq * cq)
        ao[1][...] += _colsum(dkv * ckv)
        ro[0][:, :qr] = _rms_bwd(cq, rq, dq * fi[0][...]).astype(BF16)
        ro[0][:, qr:qr + kvr] = _rms_bwd(ckv, rkv, dkv * fi[1][...]).astype(BF16)
        ro[0][:, qr + kvr:] = ri[4][...].astype(BF16)

    return _rowwise(fn, name, t_dim, ROW_TILE,
                    [(z_c, cwid, 0), (dcq, qr, 0), (dckv_k, kvr, 0), (dckv_v, kvr, 0), (dkr, 128, 0)], [q_g, kv_g],
                    [(cwid, BF16)], [((1, qr), F32), ((1, kvr), F32)])


def _causal_pairs(n, by_key):
    if by_key:
        pairs = [(i, j) for j in range(n) for i in range(j, n)]
    else:
        pairs = [(i, j) for i in range(n) for j in range(i + 1)]
    return (np.array([p[0] for p in pairs], np.int32), np.array([p[1] for p in pairs], np.int32))


def _attn_fwd(q, k, v, name, carry=None):
    t_dim = q.shape[0]
    tq = min(ATT_TILE, t_dim)
    nh = ATT_FWD_HEADS
    qi, kj = _causal_pairs(t_dim // tq, by_key=False)
    n_steps = len(qi)
    c_arrays, c_exchange = carry if carry else ((), False)
    nc = len(c_arrays)

    def body(qi_ref, kj_ref, q_ref, k_ref, v_ref, *rest):
        c_ins, (o_ref, lse_ref), c_outs = rest[:nc], rest[nc:nc + 2], rest[nc + 2:2 * nc + 2]
        m_sc, l_sc, acc_sc = rest[2 * nc + 2:2 * nc + 5]
        sems = rest[2 * nc + 5:]
        s_id = pl.program_id(1)
        i, j = qi_ref[s_id], kj_ref[s_id]
        if nc:
            @pl.when(jnp.logical_and(pl.program_id(0) == 0, s_id == 0))
            def _():
                _comm_start(c_ins, c_outs, *sems, c_exchange)

        @pl.when(j == 0)
        def _():
            m_sc[...] = jnp.full_like(m_sc, NEG)
            l_sc[...] = jnp.zeros_like(l_sc)
            acc_sc[...] = jnp.zeros_like(acc_sc)

        def step(masked):
            for hh in range(nh):
                s = lax.dot_general(q_ref[:, hh * HQ:(hh + 1) * HQ], k_ref[:, hh * HQ:(hh + 1) * HQ], _DIMS["nt"],
                                    preferred_element_type=F32)
                if masked:
                    row = lax.broadcasted_iota(jnp.int32, (tq, tq), 0)
                    col = lax.broadcasted_iota(jnp.int32, (tq, tq), 1)
                    s = jnp.where(col <= row, s, NEG)
                m_prev = m_sc[hh]
                m_new = jnp.maximum(m_prev, jnp.max(s, axis=1, keepdims=True))
                alpha = jnp.exp(m_prev - m_new)
                p = jnp.exp(s - m_new)
                l_sc[hh] = alpha * l_sc[hh] + jnp.sum(p, axis=1, keepdims=True)
                acc_sc[hh] = alpha * acc_sc[hh] + jnp.dot(p.astype(BF16), v_ref[:, hh * VDIM:(hh + 1) * VDIM],
                                                          preferred_element_type=F32)
                m_sc[hh] = m_new

        @pl.when(j < i)
        def _():
            step(False)

        @pl.when(j == i)
        def _():
            step(True)
            for hh in range(nh):
                o_ref[:, hh * VDIM:(hh + 1) * VDIM] = (acc_sc[hh] / l_sc[hh]).astype(BF16)
                lse_ref[hh] = m_sc[hh] + jnp.log(l_sc[hh])

        if nc:
            @pl.when(jnp.logical_and(pl.program_id(0) == N_HEADS // nh - 1, s_id == n_steps - 1))
            def _():
                _comm_wait(c_ins, c_outs, *sems, c_exchange)

    any_spec = pl.BlockSpec(memory_space=pl.ANY)
    grid_spec = pltpu.PrefetchScalarGridSpec(
        num_scalar_prefetch=2, grid=(N_HEADS // nh, n_steps),
        in_specs=[pl.BlockSpec((tq, nh * HQ), lambda h, s, qi, kj: (qi[s], h)),
                  pl.BlockSpec((tq, nh * HQ), lambda h, s, qi, kj: (kj[s], h)),
                  pl.BlockSpec((tq, nh * VDIM), lambda h, s, qi, kj: (kj[s], h))] + [any_spec] * nc,
        out_specs=[pl.BlockSpec((tq, nh * VDIM), lambda h, s, qi, kj: (qi[s], h)),
                   pl.BlockSpec((nh, tq, 1), lambda h, s, qi, kj: (h, qi[s], 0))] + [any_spec] * nc,
        scratch_shapes=[pltpu.VMEM((nh, tq, 1), F32), pltpu.VMEM((nh, tq, 1), F32), pltpu.VMEM((nh, tq, VDIM), F32)]
        + _comm_sems(nc))
    return pl.pallas_call(
        body, grid_spec=grid_spec,
        out_shape=[jax.ShapeDtypeStruct((t_dim, N_HEADS * VDIM), BF16), jax.ShapeDtypeStruct((N_HEADS, t_dim, 1), F32)]
        + _comm_out_shape(c_arrays, c_exchange),
        compiler_params=_cparams(("arbitrary", "arbitrary")), name=name)(
            jnp.asarray(qi), jnp.asarray(kj), q, k, v, *c_arrays)


def _attn_delta(do, o, name):
    t_dim = do.shape[0]
    tr = min(ATT_TILE, t_dim)

    def body(do_ref, o_ref, d_ref):
        d_ref[0] = jnp.sum(do_ref[...].astype(F32) * o_ref[...].astype(F32), axis=1, keepdims=True)

    return pl.pallas_call(
        body, out_shape=jax.ShapeDtypeStruct((N_HEADS, t_dim, 1), F32), grid=(N_HEADS, t_dim // tr),
        in_specs=[pl.BlockSpec((tr, VDIM), lambda h, i: (i, h)), pl.BlockSpec((tr, VDIM), lambda h, i: (i, h))],
        out_specs=pl.BlockSpec((1, tr, 1), lambda h, i: (h, i, 0)),
        compiler_params=_cparams(("parallel", "parallel")), name=name)(do, o)


def _attn_bwd(q, k, v, do, lse_row, delta_row, name, carry=None):
    t_dim = q.shape[0]
    tq = min(ATT_TILE, t_dim)
    nq = t_dim // tq
    nh = ATT_BWD_HEADS
    qi, kj = _causal_pairs(nq, by_key=True)
    n_steps = len(qi)
    c_arrays, c_exchange = carry if carry else ((), False)
    nc = len(c_arrays)

    def body(qi_ref, kj_ref, q_ref, k_ref, v_ref, do_ref, lse_ref, dl_ref, *rest):
        c_ins, (dq_ref, dk_ref, dv_ref), c_outs = rest[:nc], rest[nc:nc + 3], rest[nc + 3:2 * nc + 3]
        dk_sc, dv_sc = rest[2 * nc + 3:2 * nc + 5]
        sems = rest[2 * nc + 5:]
        s_id = pl.program_id(1)
        i, j = qi_ref[s_id], kj_ref[s_id]
        if nc:
            @pl.when(jnp.logical_and(pl.program_id(0) == 0, s_id == 0))
            def _():
                _comm_start(c_ins, c_outs, *sems, c_exchange)

        @pl.when(s_id == 0)
        def _():
            dq_ref[...] = jnp.zeros_like(dq_ref)

        rows = pl.ds(pl.multiple_of(i * tq, tq), tq)

        def step(masked):
            for hh in range(nh):
                qh, kh = q_ref[:, hh * HQ:(hh + 1) * HQ], k_ref[:, hh * HQ:(hh + 1) * HQ]
                vh, doh = v_ref[:, hh * VDIM:(hh + 1) * VDIM], do_ref[:, hh * VDIM:(hh + 1) * VDIM]
                s_t = lax.dot_general(kh, qh, _DIMS["nt"], preferred_element_type=F32)
                p_t = jnp.exp(s_t - lse_ref[hh])
                if masked:
                    row = lax.broadcasted_iota(jnp.int32, (tq, tq), 0)
                    col = lax.broadcasted_iota(jnp.int32, (tq, tq), 1)
                    p_t = jnp.where(row <= col, p_t, 0.0)
                dv = jnp.dot(p_t.astype(BF16), doh, preferred_element_type=F32)
                dp_t = lax.dot_general(vh, doh, _DIMS["nt"], preferred_element_type=F32)
                ds_t = (p_t * (dp_t - dl_ref[hh])).astype(BF16)
                dk = jnp.dot(ds_t, qh, preferred_element_type=F32)
                if masked:
                    dv_sc[hh] = dv
                    dk_sc[hh] = dk
                else:
                    dv_sc[hh] += dv
                    dk_sc[hh] += dk
                dq_ref[rows, hh * HQ:(hh + 1) * HQ] += lax.dot_general(ds_t, kh, _DIMS["tn"], preferred_element_type=F32)

        @pl.when(i == j)
        def _():
            step(True)

        @pl.when(i != j)
        def _():
            step(False)

        @pl.when(i == nq - 1)
        def _():
            for hh in range(nh):
                dk_ref[:, hh * HQ:(hh + 1) * HQ] = dk_sc[hh].astype(BF16)
                dv_ref[:, hh * VDIM:(hh + 1) * VDIM] = dv_sc[hh].astype(BF16)

        if nc:
            @pl.when(jnp.logical_and(pl.program_id(0) == N_HEADS // nh - 1, s_id == n_steps - 1))
            def _():
                _comm_wait(c_ins, c_outs, *sems, c_exchange)

    any_spec = pl.BlockSpec(memory_space=pl.ANY)
    grid_spec = pltpu.PrefetchScalarGridSpec(
        num_scalar_prefetch=2, grid=(N_HEADS // nh, n_steps),
        in_specs=[pl.BlockSpec((tq, nh * HQ), lambda h, s, qi, kj: (qi[s], h)),
                  pl.BlockSpec((tq, nh * HQ), lambda h, s, qi, kj: (kj[s], h)),
                  pl.BlockSpec((tq, nh * VDIM), lambda h, s, qi, kj: (kj[s], h)),
                  pl.BlockSpec((tq, nh * VDIM), lambda h, s, qi, kj: (qi[s], h)),
                  pl.BlockSpec((nh, 1, tq), lambda h, s, qi, kj: (h, 0, qi[s])),
                  pl.BlockSpec((nh, 1, tq), lambda h, s, qi, kj: (h, 0, qi[s]))] + [any_spec] * nc,
        out_specs=[pl.BlockSpec((t_dim, nh * HQ), lambda h, s, qi, kj: (0, h)),
                   pl.BlockSpec((tq, nh * HQ), lambda h, s, qi, kj: (kj[s], h)),
                   pl.BlockSpec((tq, nh * VDIM), lambda h, s, qi, kj: (kj[s], h))] + [any_spec] * nc,
        scratch_shapes=[pltpu.VMEM((nh, tq, HQ), F32), pltpu.VMEM((nh, tq, VDIM), F32)] + _comm_sems(nc))
    return pl.pallas_call(
        body, grid_spec=grid_spec,
        out_shape=[jax.ShapeDtypeStruct((t_dim, N_HEADS * HQ), F32), jax.ShapeDtypeStruct((t_dim, N_HEADS * HQ), BF16),
                   jax.ShapeDtypeStruct((t_dim, N_HEADS * VDIM), BF16)] + _comm_out_shape(c_arrays, c_exchange),
        compiler_params=_cparams(("arbitrary", "arbitrary")), name=name)(
            jnp.asarray(qi), jnp.asarray(kj), q, k, v, do, lse_row, delta_row, *c_arrays)


def _merge_fwd(z_g, y_a, y_b, y_c, name):
    t_dim, d = y_a.shape

    def fn(i, ri, fi, ro, ao):
        acc = _sigmoid(ri[0][:, :d].astype(F32)) * ri[1][...].astype(F32)
        acc = acc + _sigmoid(ri[0][:, d:2 * d].astype(F32)) * ri[2][...].astype(F32)
        acc = acc + _sigmoid(ri[0][:, 2 * d:].astype(F32)) * ri[3][...].astype(F32)
        ro[0][...] = acc.astype(BF16)

    return _rowwise(fn, name, t_dim, ROW_TILE_WIDE, [(z_g, 3 * d, 0), (y_a, d, 0), (y_b, d, 0), (y_c, d, 0)], [],
                    [(d, BF16)])[0]


def _merge_bwd(z_g, y_a, y_b, y_c, dmerged, name):
    t_dim, d = y_a.shape

    def fn(i, ri, fi, ro, ao):
        dm = ri[4][...]
        for q in range(3):
            s = _sigmoid(ri[0][:, q * d:(q + 1) * d].astype(F32))
            ro[q][...] = (s * dm).astype(BF16)
            ro[3][:, q * d:(q + 1) * d] = (dm * ri[1 + q][...].astype(F32) * s * (1.0 - s)).astype(BF16)

    return _rowwise(fn, name, t_dim, ROW_TILE_WIDE,
                    [(z_g, 3 * d, 0), (y_a, d, 0), (y_b, d, 0), (y_c, d, 0), (dmerged, d, 0)], [],
                    [(d, BF16), (d, BF16), (d, BF16), (3 * d, BF16)])


def _adam(w, parts, m, v, name):
    r_dim, c_dim = w.shape
    limit = max(16, ADAM_BLOCK_ELEMS // c_dim // 16 * 16)
    tr = r_dim
    if r_dim > limit:
        tr = next((t for t in range(limit, 15, -16) if r_dim % t == 0), r_dim)

    def body(w_ref, p_ref, m_ref, v_ref, g_out, d_out, m_out, v_out):
        g = p_ref[0].astype(F32)
        for s in range(1, N_DEV):
            g = g + p_ref[s].astype(F32)
        m_new = ADAM_B1 * m_ref[...] + (1.0 - ADAM_B1) * g
        v_new = ADAM_B2 * v_ref[...] + (1.0 - ADAM_B2) * (g * g)
        m_hat = m_new / (1.0 - ADAM_B1 ** ADAM_STEP)
        v_hat = v_new / (1.0 - ADAM_B2 ** ADAM_STEP)
        g_out[...] = g
        d_out[...] = -ADAM_LR * (m_hat / (jnp.sqrt(v_hat) + ADAM_EPS) + ADAM_WD * w_ref[...])
        m_out[...] = m_new
        v_out[...] = v_new

    blk = pl.BlockSpec((tr, c_dim), lambda i: (i, 0))
    return pl.pallas_call(
        body, out_shape=[jax.ShapeDtypeStruct((r_dim, c_dim), F32)] * 4, grid=(r_dim // tr,),
        in_specs=[blk, pl.BlockSpec((N_DEV, tr, c_dim), lambda i: (0, i, 0)), blk, blk], out_specs=[blk] * 4,
        compiler_params=_cparams(("parallel",)), name=name)(w, parts, m, v)


def _me_and_peers():
    x, y, c = lax.axis_index("x"), lax.axis_index("y"), lax.axis_index("c")
    me = 4 * x + 2 * y + c
    peers = []
    for k in range(1, N_DEV):
        px, py, pc = x ^ (k >> 2), y ^ ((k >> 1) & 1), c ^ (k & 1)
        peers.append(((px, py, pc), 4 * px + 2 * py + pc))
    return me, peers


def _comm_copies(ins, outs, send_sems, recv_sems, local_sems, exchange, with_arrivals):
    me, peers = _me_and_peers()

    def src(w, dest_idx):
        return ins[w].at[dest_idx] if exchange else ins[w]

    def remote(w, k, dev, src_ref, dst_ref):
        return pltpu.make_async_remote_copy(
            src_ref=src_ref, dst_ref=dst_ref, send_sem=send_sems.at[w * (N_DEV - 1) + k],
            recv_sem=recv_sems.at[w * (N_DEV - 1) + k], device_id=dev, device_id_type=pl.DeviceIdType.MESH)

    local = [pltpu.make_async_copy(src(w, me), outs[w].at[me], local_sems.at[w]) for w in range(len(ins))]
    sends, arrivals = [], []
    for w in range(len(ins)):
        for k, (dev, idx) in enumerate(peers):
            sends.append(remote(w, k, dev, src(w, idx), outs[w].at[me]))
            if with_arrivals:
                arrivals.append(remote(w, k, dev, src(w, idx), outs[w].at[idx]))
    return local, sends, arrivals


def _comm_start(ins, outs, send_sems, recv_sems, local_sems, exchange):
    local, sends, _ = _comm_copies(ins, outs, send_sems, recv_sems, local_sems, exchange, False)
    for cp in local + sends:
        cp.start()


def _comm_wait(ins, outs, send_sems, recv_sems, local_sems, exchange):
    local, sends, arrivals = _comm_copies(ins, outs, send_sems, recv_sems, local_sems, exchange, True)
    for cp in arrivals:
        cp.wait_recv()
    for cp in sends:
        cp.wait_send()
    for cp in local:
        cp.wait()


def _comm_sems(n):
    if not n:
        return []
    return [pltpu.SemaphoreType.DMA((n * (N_DEV - 1),)), pltpu.SemaphoreType.DMA((n * (N_DEV - 1),)),
            pltpu.SemaphoreType.DMA((n,))]


def _comm_out_shape(arrays, exchange):
    return [jax.ShapeDtypeStruct(a.shape if exchange else (N_DEV,) + a.shape, a.dtype) for a in arrays]


def _comm(arrays, exchange, name):
    n = len(arrays)
    hbm = pl.BlockSpec(memory_space=pltpu.HBM)

    def body(*refs):
        ins, outs, sems = refs[:n], refs[n:2 * n], refs[2 * n:]
        _comm_start(ins, outs, *sems, exchange)
        _comm_wait(ins, outs, *sems, exchange)

    return pl.pallas_call(
        body, out_shape=_comm_out_shape(arrays, exchange), in_specs=[hbm] * n, out_specs=[hbm] * n,
        scratch_shapes=_comm_sems(n), name=name)(*arrays)


def _unshard(name, g):
    if name in COL_SHARDED:
        return jnp.transpose(g, (1, 0, 2)).reshape(g.shape[1], N_DEV * g.shape[2])
    return g.reshape(N_DEV * g.shape[1], g.shape[2])


def _to_shards(name, full):
    if name in COL_SHARDED:
        r, c = full.shape
        return jnp.transpose(full.reshape(r, N_DEV, c // N_DEV), (1, 0, 2))
    return full.reshape(N_DEV, full.shape[0] // N_DEV, full.shape[1])


def _mmc(a, b, mode, out_dtype, name, carry):
    if carry:
        return _mm(a, b, mode, out_dtype, name + "_comm", carry)
    return _mm(a, b, mode, out_dtype, name), []


def _ffn_fwd(x, p, tag, carry_in=None, carry_out=None):
    h = _norm_fwd(x, p["norm_pre"], f"{tag}_norm")
    gate, up, act, got_in = _mm_swiglu(h, p["w_in"], f"{tag}_in_comm" if carry_in else f"{tag}_in", carry_in)
    y, got_out = _mmc(act, p["w_out"], "nn", F32, f"{tag}_out", carry_out)
    x_new = _resid_fwd(x, y, p["norm_post"], 0.5, f"{tag}_resid")
    return x_new, dict(x=x, h=h, gate=gate, up=up, act=act, y=y), list(got_in), list(got_out)


def _ffn_bwd(dxo, s, p, tag, carry_out=None, carry_in=None):
    dy, dg_post = _resid_bwd(s["y"], dxo, p["norm_post"], 0.5, f"{tag}_resid_bwd")
    dact, got_out = _mmc(dy, p["w_out"], "nt", BF16, f"{tag}_out_dx", carry_out)
    dw_out = _mm(s["act"], dy, "tn", BF16, f"{tag}_out_dw", shard_out="row")
    dgu = _swiglu_bwd(s["gate"], s["up"], dact, f"{tag}_act_bwd")
    dh, got_in = _mmc(dgu, p["w_in"], "nt", F32, f"{tag}_in_dx", carry_in)
    dw_in = _mm(s["h"], dgu, "tn", BF16, f"{tag}_in_dw", shard_out="col")
    dx, dg_pre = _norm_bwd(s["x"], [dh], dxo, p["norm_pre"], f"{tag}_norm_bwd")
    return dx, dict(norm_pre=dg_pre, norm_post=dg_post, w_in=dw_in, w_out=dw_out), got_out, got_in


def _mixer_fwd(x, p, rope_tabs, dims, carry):
    gmw, cw, qr, kvr = dims["gmw"], dims["cw"], dims["qr"], dims["kvr"]
    cf, s1, s2 = rope_tabs
    scale = (NOPE + ROPE) ** -0.5
    h = _norm_fwd(x, p["norm_pre"], "mix_norm")
    z_a = _mm(h, p["w_a"], "nn", BF16, "mix_in_a")
    z_c = _mm(h, p["w_c"], "nn", F32, "mix_in_c")
    z_g = _mm(h, p["w_g"], "nn", BF16, "mix_in_g")
    o_a = _gm_fwd(z_a, p["gm_ln_g"], p["gm_ln_b"], p["gm_w_s"], p["gm_b_s"], gmw, "gm_fwd")
    a = _glu_fwd(z_a, gmw, cw, "glu_fwd")
    y_conv, o_b = _conv_fwd(a, p["conv_w"], p["conv_b"], p["conv_ln_g"], p["conv_ln_b"], "conv_fwd")
    cqn, ckvn = _mla_norm_fwd(z_c, p["q_norm"], p["kv_norm"], qr, kvr, "mla_norm")
    q_pre = _mm(cqn, p["w_uq"], "nn", BF16, "mla_uq")
    k_nope = _mm(ckvn, p["w_uk"], "nn", BF16, "mla_uk")
    v = _mm(ckvn, p["w_uv"], "nn", BF16, "mla_uv")
    q_cat, k_cat = _rope_fwd(q_pre, k_nope, z_c, cf, s1, s2, (qr + kvr) // 128, scale, "rope_fwd")
    o_c, lse, *carried = _attn_fwd(q_cat, k_cat, v, "attn_fwd_comm" if carry else "attn_fwd", carry)
    y_a = _mm(o_a, p["wb_a"], "nn", BF16, "branch_a")
    y_b = _mm(o_b, p["wb_b"], "nn", BF16, "branch_b")
    y_c = _mm(o_c, p["wb_c"], "nn", BF16, "branch_c")
    merged = _merge_fwd(z_g, y_a, y_b, y_c, "merge_fwd")
    m = _mm(merged, p["w_out"], "nn", F32, "mix_out")
    x_new = _resid_fwd(x, m, p["norm_post"], 1.0, "mix_resid")
    saved = dict(x=x, h=h, z_a=z_a, z_c=z_c, z_g=z_g, o_a=o_a, a=a, y_conv=y_conv, o_b=o_b, cqn=cqn, ckvn=ckvn,
                 v=v, q_cat=q_cat, k_cat=k_cat, o_c=o_c, lse=lse, y_a=y_a, y_b=y_b, y_c=y_c, merged=merged, m=m)
    return x_new, saved, carried


def _mixer_bwd(dxo, s, p, rope_tabs, dims, carry):
    gmw, cw, qr, kvr = dims["gmw"], dims["cw"], dims["qr"], dims["kvr"]
    cf, s1, s2 = rope_tabs
    scale = (NOPE + ROPE) ** -0.5
    t_dim = dxo.shape[0]
    g = {}
    dm, g["norm_post"] = _resid_bwd(s["m"], dxo, p["norm_post"], 1.0, "mix_resid_bwd")
    dmerged = _mm(dm, p["w_out"], "nt", F32, "mix_out_dx")
    g["w_out"] = _mm(s["merged"], dm, "tn", F32, "mix_out_dw")
    dy_a, dy_b, dy_c, dz_g = _merge_bwd(s["z_g"], s["y_a"], s["y_b"], s["y_c"], dmerged, "merge_bwd")
    do_a = _mm(dy_a, p["wb_a"], "nt", F32, "branch_a_dx")
    do_b = _mm(dy_b, p["wb_b"], "nt", F32, "branch_b_dx")
    do_c = _mm(dy_c, p["wb_c"], "nt", BF16, "branch_c_dx")
    g["w_branch"] = jnp.concatenate([_mm(s["o_a"], dy_a, "tn", F32, "branch_a_dw"),
                                     _mm(s["o_b"], dy_b, "tn", F32, "branch_b_dw"),
                                     _mm(s["o_c"], dy_c, "tn", F32, "branch_c_dw")], axis=0)
    delta = _attn_delta(do_c, s["o_c"], "attn_delta")
    dq_cat, dk_cat, dv, *carried = _attn_bwd(s["q_cat"], s["k_cat"], s["v"], do_c, s["lse"].reshape(N_HEADS, 1, t_dim),
                                             delta.reshape(N_HEADS, 1, t_dim),
                                             "attn_bwd_comm" if carry else "attn_bwd", carry)
    dq_pre, dk_nope, dkr = _rope_bwd(dq_cat, dk_cat, cf, s1, s2, scale, "rope_bwd")
    dcq = _mm(dq_pre, p["w_uq"], "nt", F32, "mla_uq_dx")
    g["w_uq"] = _mm(s["cqn"], dq_pre, "tn", F32, "mla_uq_dw")
    dckv_k = _mm(dk_nope, p["w_uk"], "nt", F32, "mla_uk_dx")
    dckv_v = _mm(dv, p["w_uv"], "nt", F32, "mla_uv_dx")
    g["w_uk"] = _mm(s["ckvn"], dk_nope, "tn", F32, "mla_uk_dw")
    g["w_uv"] = _mm(s["ckvn"], dv, "tn", F32, "mla_uv_dw")
    dz_c, g["q_norm"], g["kv_norm"] = _mla_norm_bwd(s["z_c"], dcq, dckv_k, dckv_v, dkr, p["q_norm"], p["kv_norm"],
                                                    qr, kvr, "mla_norm_bwd")
    dy_conv, g["conv_ln_g"], g["conv_ln_b"] = _conv_ln_bwd(s["y_conv"], do_b, p["conv_ln_g"], p["conv_ln_b"], "conv_ln_bwd")
    dz_cv, g["conv_w"], g["conv_b"] = _conv_bwd(dy_conv, s["a"], s["z_a"], p["conv_w"], gmw, "conv_bwd")
    dz_gm, g["gm_ln_g"], g["gm_ln_b"], g["gm_w_s"], g["gm_b_s"] = _gm_bwd(
        s["z_a"], do_a, p["gm_ln_g"], p["gm_ln_b"], p["gm_w_s"], p["gm_w_s_t"], p["gm_b_s"], gmw, "gm_bwd")
    dh_gm = _mm(dz_gm, p["w_gm"], "nt", F32, "mix_in_gm_dx")
    dh_cv = _mm(dz_cv, p["w_cv"], "nt", F32, "mix_in_cv_dx")
    dh_c = _mm(dz_c, p["w_c"], "nt", F32, "mix_in_c_dx")
    dh_g = _mm(dz_g, p["w_g"], "nt", F32, "mix_in_g_dx")
    g["w_gm"] = _mm(s["h"], dz_gm, "tn", F32, "mix_in_gm_dw")
    g["w_cv"] = _mm(s["h"], dz_cv, "tn", F32, "mix_in_cv_dw")
    g["w_c"] = _mm(s["h"], dz_c, "tn", F32, "mix_in_c_dw")
    g["w_g"] = _mm(s["h"], dz_g, "tn", F32, "mix_in_g_dw")
    dx, g["norm_pre"] = _norm_bwd(s["x"], [dh_gm, dh_cv, dh_c, dh_g], dxo, p["norm_pre"], "mix_norm_bwd")
    return dx, g, carried


def _ffn_params(full, small, k, l):
    row = lambda a: a[l][None, :]
    return dict(norm_pre=row(small[f"{k}_norm_pre"]), norm_post=row(small[f"{k}_norm_post"]),
                w_in=full[f"{k}_w_in"], w_out=full[f"{k}_w_out"])


def _mix_params(full, small, conv_w_full, dims, l):
    gmw, cw, qr, kvr, d = dims["gmw"], dims["cw"], dims["qr"], dims["kvr"], dims["d"]
    row = lambda a: a[l][None, :]
    w_in = full["mix_w_in"]
    a_end = 2 * gmw + 2 * cw
    c_end = a_end + qr + kvr + ROPE
    w_c = jnp.concatenate([w_in[:, a_end:c_end], jnp.zeros((d, 128 - ROPE), w_in.dtype)], axis=1)
    w_uq = full["mla_w_uq"].reshape(qr, N_HEADS, NOPE + ROPE)
    w_uq = jnp.concatenate([w_uq, jnp.zeros((qr, N_HEADS, HQ - NOPE - ROPE), w_uq.dtype)], axis=2).reshape(qr, N_HEADS * HQ)
    w_ukv = full["mla_w_ukv"].reshape(kvr, N_HEADS, NOPE + VDIM)
    w_b = full["mix_w_branch"]
    w_s = small["gm_w_s"][l]
    return dict(norm_pre=row(small["mix_norm_pre"]), norm_post=row(small["mix_norm_post"]),
                w_a=w_in[:, :a_end], w_gm=w_in[:, :2 * gmw], w_cv=w_in[:, 2 * gmw:a_end], w_c=w_c, w_g=w_in[:, c_end:],
                gm_ln_g=row(small["gm_ln_g"]), gm_ln_b=row(small["gm_ln_b"]), gm_w_s=w_s,
                gm_w_s_t=jnp.transpose(w_s, (0, 2, 1)), gm_b_s=small["gm_b_s"][l][:, :, None],
                conv_w=jnp.concatenate([conv_w_full, jnp.zeros((CONV_HALO - CONV_K, cw), F32)], axis=0),
                conv_b=row(small["conv_b"]), conv_ln_g=row(small["conv_ln_g"]), conv_ln_b=row(small["conv_ln_b"]),
                q_norm=row(small["mla_q_norm"]), kv_norm=row(small["mla_kv_norm"]),
                w_uq=w_uq, w_uk=w_ukv[:, :, :NOPE].reshape(kvr, N_HEADS * NOPE),
                w_uv=w_ukv[:, :, NOPE:].reshape(kvr, N_HEADS * VDIM),
                wb_a=w_b[:gmw], wb_b=w_b[gmw:gmw + cw], wb_c=w_b[gmw + cw:], w_out=full["mix_w_out"])


def _mix_big_grads(gm, dims):
    qr, kvr = dims["qr"], dims["kvr"]
    return {
        "mix_w_in": jnp.concatenate([gm["w_gm"], gm["w_cv"], gm["w_c"][:, :qr + kvr + ROPE], gm["w_g"]], axis=1),
        "mla_w_uq": gm["w_uq"].reshape(qr, N_HEADS, HQ)[:, :, :NOPE + ROPE].reshape(qr, N_HEADS * (NOPE + ROPE)),
        "mla_w_ukv": jnp.concatenate([gm["w_uk"].reshape(kvr, N_HEADS, NOPE), gm["w_uv"].reshape(kvr, N_HEADS, VDIM)],
                                     axis=2).reshape(kvr, N_HEADS * (NOPE + VDIM)),
        "mix_w_branch": gm["w_branch"], "mix_w_out": gm["w_out"], "conv_w": gm["conv_w"][:CONV_K],
    }


def _small_grads(g1, gm, g2):
    return {
        "ffn1_norm_pre": g1["norm_pre"][0], "ffn1_norm_post": g1["norm_post"][0],
        "ffn2_norm_pre": g2["norm_pre"][0], "ffn2_norm_post": g2["norm_post"][0],
        "mix_norm_pre": gm["norm_pre"][0], "mix_norm_post": gm["norm_post"][0],
        "gm_ln_g": gm["gm_ln_g"][0], "gm_ln_b": gm["gm_ln_b"][0], "gm_w_s": gm["gm_w_s"], "gm_b_s": gm["gm_b_s"][:, :, 0],
        "conv_b": gm["conv_b"][0], "conv_ln_g": gm["conv_ln_g"][0], "conv_ln_b": gm["conv_ln_b"][0],
        "mla_q_norm": gm["q_norm"][0], "mla_kv_norm": gm["kv_norm"][0],
    }


def _rope_tables(positions):
    inv_freq = ROPE_THETA ** (-jnp.arange(0, ROPE, 2, dtype=F32) / ROPE)
    ang = positions.astype(F32)[:, None] * inv_freq
    cos, sin = jnp.cos(ang), jnp.sin(ang)
    z = lambda w: jnp.zeros((positions.shape[0], w), F32)
    return (jnp.concatenate([cos, cos, z(64)], axis=1), jnp.concatenate([-sin, z(96)], axis=1),
            jnp.concatenate([z(32), sin, z(64)], axis=1))


def _pad_rows(flat, mult):
    n = flat.shape[0]
    pad = (-n) % mult
    return jnp.concatenate([flat, jnp.zeros((pad,), flat.dtype)]) if pad else flat


def kernel(x, positions, ffn1_norm_pre, ffn1_norm_post, ffn1_w_in, ffn1_w_out, mix_norm_pre, mix_norm_post, mix_w_in, gm_ln_g, gm_ln_b, gm_w_s, gm_b_s, conv_w, conv_b, conv_ln_g, conv_ln_b, mla_q_norm, mla_w_uq, mla_kv_norm, mla_w_ukv, mix_w_branch, mix_w_out, ffn2_norm_pre, ffn2_norm_post, ffn2_w_in, ffn2_w_out, loss_target, m_ffn1_norm_pre, m_ffn1_norm_post, m_ffn1_w_in, m_ffn1_w_out, m_mix_norm_pre, m_mix_norm_post, m_mix_w_in, m_gm_ln_g, m_gm_ln_b, m_gm_w_s, m_gm_b_s, m_conv_w, m_conv_b, m_conv_ln_g, m_conv_ln_b, m_mla_q_norm, m_mla_w_uq, m_mla_kv_norm, m_mla_w_ukv, m_mix_w_branch, m_mix_w_out, m_ffn2_norm_pre, m_ffn2_norm_post, m_ffn2_w_in, m_ffn2_w_out, v_ffn1_norm_pre, v_ffn1_norm_post, v_ffn1_w_in, v_ffn1_w_out, v_mix_norm_pre, v_mix_norm_post, v_mix_w_in, v_gm_ln_g, v_gm_ln_b, v_gm_w_s, v_gm_b_s, v_conv_w, v_conv_b, v_conv_ln_g, v_conv_ln_b, v_mla_q_norm, v_mla_w_uq, v_mla_kv_norm, v_mla_w_ukv, v_mix_w_branch, v_mix_w_out, v_ffn2_norm_pre, v_ffn2_norm_post, v_ffn2_w_in, v_ffn2_w_out):
    w = dict(zip(WEIGHTS, (ffn1_norm_pre, ffn1_norm_post, ffn1_w_in, ffn1_w_out, mix_norm_pre, mix_norm_post, mix_w_in, gm_ln_g, gm_ln_b, gm_w_s, gm_b_s, conv_w, conv_b, conv_ln_g, conv_ln_b, mla_q_norm, mla_w_uq, mla_kv_norm, mla_w_ukv, mix_w_branch, mix_w_out, ffn2_norm_pre, ffn2_norm_post, ffn2_w_in, ffn2_w_out)))
    mom_m = dict(zip(WEIGHTS, (m_ffn1_norm_pre, m_ffn1_norm_post, m_ffn1_w_in, m_ffn1_w_out, m_mix_norm_pre, m_mix_norm_post, m_mix_w_in, m_gm_ln_g, m_gm_ln_b, m_gm_w_s, m_gm_b_s, m_conv_w, m_conv_b, m_conv_ln_g, m_conv_ln_b, m_mla_q_norm, m_mla_w_uq, m_mla_kv_norm, m_mla_w_ukv, m_mix_w_branch, m_mix_w_out, m_ffn2_norm_pre, m_ffn2_norm_post, m_ffn2_w_in, m_ffn2_w_out)))
    mom_v = dict(zip(WEIGHTS, (v_ffn1_norm_pre, v_ffn1_norm_post, v_ffn1_w_in, v_ffn1_w_out, v_mix_norm_pre, v_mix_norm_post, v_mix_w_in, v_gm_ln_g, v_gm_ln_b, v_gm_w_s, v_gm_b_s, v_conv_w, v_conv_b, v_conv_ln_g, v_conv_ln_b, v_mla_q_norm, v_mla_w_uq, v_mla_kv_norm, v_mla_w_ukv, v_mix_w_branch, v_mix_w_out, v_ffn2_norm_pre, v_ffn2_norm_post, v_ffn2_w_in, v_ffn2_w_out)))
    n_layers = ffn1_norm_pre.shape[0]
    t_dim, d = x.shape[1], x.shape[2]
    dims = dict(d=d, gmw=gm_ln_g.shape[1], cw=conv_ln_g.shape[1], qr=mla_q_norm.shape[1], kvr=mla_kv_norm.shape[1])
    x0 = x.reshape(t_dim, d)
    target = loss_target.reshape(t_dim, d)
    rope_tabs = _rope_tables(positions.reshape(t_dim))

    conv_all = _unshard_conv(_comm([conv_w], False, "gather_conv")[0])
    gather = lambda names, l: ([w[k][l].astype(BF16) for k in names], False)
    full_of = lambda names, gathered: {k: _unshard(k, g) for k, g in zip(names, gathered)}
    last = n_layers - 1
    got_ffn1 = _comm(gather(FFN1_W, 0)[0], False, "gather_ffn1")
    got_mix = None
    params, saved = [], []
    xc = x0
    for l in range(n_layers):
        p1 = _ffn_params(full_of(FFN1_W, got_ffn1), w, "ffn1", l)
        xc, s1, got_a, got_b = _ffn_fwd(xc, p1, "ffn1", gather(MIX_W[:1], 0) if l == 0 else None,
                                        gather(MIX_W[1:], 0) if l == 0 else None)
        if l == 0:
            got_mix = got_a + got_b
        pm = _mix_params(full_of(MIX_W, got_mix), w, conv_all[l], dims, l)
        xc, sm, got = _mixer_fwd(xc, pm, rope_tabs, dims, gather(FFN2_W, l) if l == last else
                                 (gather(FFN2_W, l)[0] + gather(MIX_W, l + 1)[0], False))
        got_ffn2, got_mix = got[:len(FFN2_W)], got[len(FFN2_W):]
        p2 = _ffn_params(full_of(FFN2_W, got_ffn2), w, "ffn2", l)
        xc, s2, got_a, got_b = _ffn_fwd(xc, p2, "ffn2", gather(FFN1_W[:1], l + 1) if l < last else None,
                                        gather(FFN1_W[1:], l + 1) if l < last else None)
        got_ffn1 = got_a + got_b
        params.append((p1, pm, p2))
        saved.append((s1, sm, s2))
    dx, loss_part = _loss_fwd_bwd(xc, target, "loss")
    loss = lax.psum(loss_part[0, 0], ("x", "y", "c"))

    big_out = {k: [None] * n_layers for k in BIG + ("conv_w",)}
    small_parts = [None] * n_layers
    send = lambda names, grads: [_to_shards(k, grads[k]).astype(BF16) for k in names]

    def update(l, names, recv):
        for k, r in zip(names, recv):
            wl = w[k][l]
            r2 = (lambda a: a.reshape(-1, a.shape[-1]))
            outs = _adam(r2(wl), r.reshape(N_DEV, -1, wl.shape[-1]), r2(mom_m[k][l]), r2(mom_v[k][l]), f"adam_{k}")
            big_out[k][l] = [o.reshape(wl.shape) for o in outs]

    above_mix, above_ffn1 = [], []
    for l in reversed(range(n_layers)):
        p1, pm, p2 = params[l]
        s1, sm, s2 = saved[l]
        dx, g2, _, got_in = _ffn_bwd(dx, s2, p2, "ffn2", None, (above_ffn1[:1], True) if above_ffn1 else None)
        mine = [g2["w_in"], g2["w_out"]]
        dx, gm, recv = _mixer_bwd(dx, sm, pm, rope_tabs, dims, (mine + above_mix + above_ffn1[1:], True))
        update(l, FFN2_W, recv[:len(FFN2_W)])
        if above_mix:
            update(l + 1, MIX_G, recv[len(FFN2_W):len(FFN2_W) + len(MIX_G)])
            update(l + 1, FFN1_W, list(got_in) + list(recv[len(FFN2_W) + len(MIX_G):]))
        mix_out = send(MIX_G, _mix_big_grads(gm, dims))
        if l > 0:
            dx, g1, _, _ = _ffn_bwd(dx, s1, p1, "ffn1")
        else:
            dx, g1, got_a, got_b = _ffn_bwd(dx, s1, p1, "ffn1", (mix_out[1:], True), (mix_out[:1], True))
            update(0, MIX_G, got_b + got_a)
        ffn1_out = [g1["w_in"], g1["w_out"]]
        above_mix, above_ffn1 = mix_out, ffn1_out
        small_parts[l] = _small_grads(g1, gm, g2)
    update(0, FFN1_W, _comm(ffn1_out, True, "exchange_ffn1"))
    grad_x = dx.reshape(x.shape)

    flat = lambda tree: _pad_rows(jnp.concatenate([tree[k].reshape(-1) for k in SMALL]), 256 * 128).reshape(-1, 128)
    g_small = flat({k: jnp.stack([small_parts[l][k] for l in range(n_layers)]) for k in SMALL})
    parts = _comm([g_small], False, "gather_small_grads")[0]
    s_outs = _adam(flat(w), parts, flat(mom_m), flat(mom_v), "adam_small")
    small_out = {k: [] for k in SMALL}
    for o in s_outs:
        o = o.reshape(-1)
        off = 0
        for k in SMALL:
            n = int(np.prod(w[k].shape))
            small_out[k].append(o[off:off + n].reshape(w[k].shape))
            off += n

    def out(which, k):
        if k in SMALL:
            return small_out[k][which]
        return jnp.stack([big_out[k][l][which] for l in range(n_layers)])

    return (loss, grad_x, *[out(0, k) for k in WEIGHTS], *[out(1, k) for k in WEIGHTS],
            *[out(2, k) for k in WEIGHTS], *[out(3, k) for k in WEIGHTS])


def _unshard_conv(g):
    n_dev, n_layers, k, c = g.shape
    return jnp.transpose(g, (1, 2, 0, 3)).reshape(n_layers, k, n_dev * c)
```
